```python
import math
import functools
import jax
import jax.numpy as jnp
from jax import lax
import numpy as np

D_MODEL = 1024
BATCH = 8
SEQ = 4096
DEPTH = 2
DEC_BATCH = 32
DEC_SEQ = 1
PAST_LEN = 16384
PAGE_SIZE = 128

HEAD_DIM = 64
GROUP_WIDTH = D_MODEL // 4
A_HEADS = GROUP_WIDTH // HEAD_DIM
NSA_HEADS = GROUP_WIDTH // HEAD_DIM
DSA_HEADS = GROUP_WIDTH // HEAD_DIM
CONV_CH = GROUP_WIDTH
RWKV_W_RANK = 32
RWKV_A_RANK = 32
RWKV_G_RANK = 64
RWKV_DECAY_SCALE = 0.606531
RWKV_GN_EPS = 64e-5
NSA_CMP_BLOCK = 32
NSA_CMP_HIDDEN = 128
NSA_SEL_BLOCK = 64
NSA_TOPN = 16
NSA_LOCAL_BLOCKS = 2
NSA_WINDOW = 512
DSA_TOPK = 256
IDX_HEADS = 4
IDX_DIM = 32
CONV_K = 3
N_EXPERTS = 32
TOP_K = 4
EXPERT_FF = D_MODEL
SWIGLU_LIMIT = 7.0
SWIGLU_ALPHA = 1.702
MOE_MAX_BLOCK = 256
Q_BLOCK = 128
DN_ALPHA = (2 * DEPTH) ** 0.25
DN_BETA = (8 * DEPTH) ** -0.25
NEG_INF = -1e30
FORCE_SCORE = 1e4
A_COLS = 3 * GROUP_WIDTH + RWKV_W_RANK + RWKV_A_RANK + RWKV_G_RANK
B_COLS = GROUP_WIDTH + 6 * HEAD_DIM + 3 * NSA_HEADS
C_COLS = GROUP_WIDTH + 2 * HEAD_DIM + IDX_HEADS * IDX_DIM + IDX_DIM + IDX_HEADS
D_COLS = 3 * CONV_CH
IN_COLS = A_COLS + B_COLS + C_COLS + D_COLS

kernel_name = 'hybrid_rwkv7_nsa_dsa_shortconv_moe_step'


def _take(z, sizes):
    offs = np.cumsum([0] + list(sizes))
    return [z[..., int(offs[i]):int(offs[i + 1])] for i in range(len(sizes))]


def _layernorm(x, g=None, b=None, eps=1e-5):
    xf = x.astype(jnp.float32)
    mu = jnp.mean(xf, axis=-1, keepdims=True)
    var = jnp.mean(jnp.square(xf - mu), axis=-1, keepdims=True)
    y = (xf - mu) * lax.rsqrt(var + eps)
    if g is not None:
        y = y * g.astype(jnp.float32) + b.astype(jnp.float32)
    return y.astype(x.dtype)


def _masked_softmax(s, valid):
    p = jax.nn.softmax(jnp.where(valid, s, NEG_INF), axis=-1)
    return jnp.where(valid, p, 0.0)


def _alibi_slopes(n):
    return 2.0 ** (-8.0 * jnp.arange(1, n + 1, dtype=jnp.float32) / n)


def _rows_fetch(src):
    def fetch(pos):
        b = pos.shape[0]
        flat = jnp.clip(pos.reshape(b, -1), 0, src.shape[1] - 1)
        return src[jnp.arange(b)[:, None], flat].reshape(pos.shape + src.shape[2:])
    return fetch


def _paged_fetch(pool, page_table, new_rows):
    n_past = page_table.shape[1] * PAGE_SIZE
    n_new = new_rows.shape[1]

    def fetch(pos):
        b = pos.shape[0]
        flat = pos.reshape(b, -1)
        bi = jnp.arange(b)[:, None]
        pc = jnp.clip(flat, 0, n_past - 1)
        past = pool[page_table[bi, pc // PAGE_SIZE], pc % PAGE_SIZE]
        new = new_rows[bi, jnp.clip(flat - n_past, 0, n_new - 1)].astype(past.dtype)
        is_past = (flat < n_past).reshape(flat.shape + (1,) * (past.ndim - 2))
        return jnp.where(is_past, past, new).reshape(pos.shape + past.shape[2:])
    return fetch


def _rwkv_mixer(za, shift_prev, s0, lp):
    b, t, _ = za.shape
    f32 = jnp.float32
    z_prev = jnp.concatenate([shift_prev[:, None].astype(za.dtype), za[:, :-1]], axis=1)
    zs = za + lp['rwkv_mu'] * (z_prev - za)
    r, k, v, w_lo, a_lo, g_lo = _take(zs, [GROUP_WIDTH] * 3 + [RWKV_W_RANK, RWKV_A_RANK, RWKV_G_RANK])
    decay = jnp.exp(-RWKV_DECAY_SCALE * jax.nn.sigmoid((lp['rwkv_w0'] + jnp.tanh(w_lo) @ lp['rwkv_w2']).astype(f32)))
    a = jax.nn.sigmoid((lp['rwkv_a0'] + a_lo @ lp['rwkv_a2']).astype(f32))
    g = jax.nn.sigmoid(g_lo) @ lp['rwkv_g2']
    heads = lambda u: u.astype(f32).reshape(b, t, A_HEADS, HEAD_DIM)
    r, k, v, decay, a = heads(r), heads(k), heads(v), heads(decay), heads(a)
    kk = k * lp['rwkv_kk']
    kk = kk / jnp.maximum(jnp.sqrt(jnp.sum(kk * kk, axis=-1, keepdims=True)), 1e-12)
    k = k * (1.0 + (a - 1.0) * lp['rwkv_ka'])

    def step(s, inp):
        r_t, k_t, v_t, w_t, kk_t, a_t = inp
        s = (s * w_t[:, :, None, :]
             - jnp.einsum('bhvk,bhk->bhv', s, kk_t)[..., None] * (kk_t * a_t)[:, :, None, :]
             + v_t[..., None] * k_t[:, :, None, :])
        return s, jnp.einsum('bhvk,bhk->bhv', s, r_t)

    xs = tuple(jnp.swapaxes(u, 0, 1) for u in (r, k, v, decay, kk, a))
    s_fin, y = lax.scan(step, s0.astype(f32), xs)
    y = jnp.swapaxes(y, 0, 1)
    mu = jnp.mean(y, axis=-1, keepdims=True)
    var = jnp.mean(jnp.square(y - mu), axis=-1, keepdims=True)
    yn = ((y - mu) * lax.rsqrt(var + RWKV_GN_EPS) * lp['rwkv_gn_g'].reshape(A_HEADS, HEAD_DIM)
          + lp['rwkv_gn_b'].reshape(A_HEADS, HEAD_DIM))
    bonus = jnp.sum(r * k * lp['rwkv_rk'], axis=-1, keepdims=True) * v
    out = (yn + bonus).reshape(b, t, GROUP_WIDTH) * g
    return out.astype(za.dtype), s_fin, za[:, -1]


def _nsa_split(zb):
    b, t, _ = zb.shape
    q, kc, ks, kw, gl = _take(zb, [GROUP_WIDTH, 2 * HEAD_DIM, 2 * HEAD_DIM, 2 * HEAD_DIM, 3 * NSA_HEADS])
    kv = lambda u: u.reshape(b, t, 2, HEAD_DIM)
    gates = jax.nn.sigmoid(gl.astype(jnp.float32)).reshape(b, t, NSA_HEADS, 3)
    return q.reshape(b, t, NSA_HEADS, HEAD_DIM), kv(kc), kv(ks), kv(kw), gates


def _nsa_compress(rows, lp):
    b, l = rows.shape[:2]
    nc = l // NSA_CMP_BLOCK
    blk = rows[:, :nc * NSA_CMP_BLOCK].astype(jnp.float32).reshape(b, nc, NSA_CMP_BLOCK, 2, HEAD_DIM) + lp['nsa_pe']
    flat = jnp.swapaxes(blk, 2, 3).reshape(b, nc, 2, NSA_CMP_BLOCK * HEAD_DIM)
    hid = jax.nn.gelu(jnp.einsum('bnci,cih->bnch', flat, lp['nsa_w1']))
    return jnp.einsum('bnch,chd->bncd', hid, lp['nsa_w2'])


def _nsa_attend(q, gates, qpos, kvc, fetch_slc, kvw, wpos, slopes, n_keys):
    f32 = jnp.float32
    b, tq = q.shape[:2]
    q = q.astype(f32) * HEAD_DIM ** -0.5
    kvc = kvc.astype(f32)
    kvw = kvw.astype(f32)
    sl = slopes[:, None, None]
    nc = kvc.shape[1]
    cend = jnp.arange(nc) * NSA_CMP_BLOCK + (NSA_CMP_BLOCK - 1)
    s_c = jnp.einsum('bqhd,bnd->bhqn', q, kvc[:, :, 0]) - sl * (qpos[:, None] - cend[None, :]).astype(f32)
    p_c = _masked_softmax(s_c, cend[None, :] <= qpos[:, None])
    o_c = jnp.einsum('bhqn,bnd->bqhd', p_c, kvc[:, :, 1])
    ratio = NSA_SEL_BLOCK // NSA_CMP_BLOCK
    n_sel = -(-n_keys // NSA_SEL_BLOCK)
    imp = jnp.pad(p_c.sum(1), ((0, 0), (0, 0), (0, n_sel * ratio - nc))).reshape(b, tq, n_sel, ratio).sum(-1)
    blk = jnp.arange(n_sel)[None, :]
    cur = (qpos // NSA_SEL_BLOCK)[:, None]
    forced = (blk == 0) | (blk > cur - NSA_LOCAL_BLOCKS)
    score = jnp.where(blk > cur, NEG_INF, jnp.where(forced, FORCE_SCORE, imp))
    n_top = min(NSA_TOPN, n_sel)
    _, sel = lax.top_k(score, n_top)
    kpos = (sel[..., None] * NSA_SEL_BLOCK + jnp.arange(NSA_SEL_BLOCK)).reshape(b, tq, n_top * NSA_SEL_BLOCK)
    kv_s = fetch_slc(kpos).astype(f32)
    s_s = (jnp.einsum('bqhd,bqnd->bhqn', q, kv_s[..., 0, :])
           - sl * (qpos[None, None, :, None] - kpos[:, None]).astype(f32))
    p_s = _masked_softmax(s_s, (kpos <= qpos[None, :, None])[:, None])
    o_s = jnp.einsum('bhqn,bqnd->bqhd', p_s, kv_s[..., 1, :])
    dw = qpos[:, None] - wpos[None, :]
    s_w = jnp.einsum('bqhd,bnd->bhqn', q, kvw[:, :, 0]) - sl * dw.astype(f32)
    p_w = _masked_softmax(s_w, (dw >= 0) & (dw <= NSA_WINDOW) & (wpos[None, :] >= 0))
    o_w = jnp.einsum('bhqn,bnd->bqhd', p_w, kvw[:, :, 1])
    return gates[..., 0:1] * o_c + gates[..., 1:2] * o_s + gates[..., 2:3] * o_w


def _dsa_split(zc):
    b, t, _ = zc.shape
    q, kv, qi, ki, wi = _take(zc, [GROUP_WIDTH, 2 * HEAD_DIM, IDX_HEADS * IDX_DIM, IDX_DIM, IDX_HEADS])
    return (q.reshape(b, t, DSA_HEADS, HEAD_DIM), kv.reshape(b, t, 2, HEAD_DIM),
            qi.reshape(b, t, IDX_HEADS, IDX_DIM), ki, wi)


def _dsa_attend(q, qi, wi, qpos, kidx, fetch_kv, slopes, n_keys):
    f32 = jnp.float32
    l = kidx.shape[1]
    rel = jax.nn.relu(jnp.einsum('bqhi,bsi->bqhs', qi.astype(f32), kidx.astype(f32)))
    score = jnp.einsum('bqh,bqhs->bqs', wi.astype(f32), rel)
    score = jnp.where(jnp.arange(l)[None, None, :] <= qpos[None, :, None], score, NEG_INF)
    k_sel = min(DSA_TOPK, n_keys // 4)
    _, sel = lax.top_k(score, k_sel)
    kv = fetch_kv(sel).astype(f32)
    s = (jnp.einsum('bqhd,bqnd->bhqn', q.astype(f32) * HEAD_DIM ** -0.5, kv[..., 0, :])
         - slopes[:, None, None] * (qpos[None, None, :, None] - sel[:, None]).astype(f32))
    p = _masked_softmax(s, (sel <= qpos[None, :, None])[:, None])
    return jnp.einsum('bhqn,bqnd->bqhd', p, kv[..., 1, :])


def _conv_mixer(zd, buf, conv_w):
    t = zd.shape[1]
    b_gate, c_gate, xin = _take(zd, [CONV_CH] * 3)
    ue = jnp.concatenate([buf.astype(zd.dtype), c_gate * xin], axis=1)
    y = sum(ue[:, j:j + t] * conv_w[j] for j in range(CONV_K))
    return b_gate * y, ue[:, t:]


def _moe_block_rows(n_assign):
    per_expert = max(1, n_assign // N_EXPERTS)
    return min(MOE_MAX_BLOCK, max(8, 1 << (per_expert.bit_length() - 1)))


def _moe(h, lp):
    b, t, d = h.shape
    n = b * t
    x = h.reshape(n, d)
    logits = (x @ lp['moe_wr'] + lp['moe_br']).astype(jnp.float32)
    top_v, top_e = lax.top_k(logits, TOP_K)
    gate = jax.nn.softmax(top_v, axis=-1)
    m = n * TOP_K
    blk = _moe_block_rows(m)
    cap = -(-(m + N_EXPERTS * (blk - 1)) // blk) * blk
    e_flat = top_e.reshape(m)
    order = jnp.argsort(e_flat)
    e_s = e_flat[order]
    tok_s = (jnp.arange(m, dtype=jnp.int32) // TOP_K)[order]
    g_s = gate.reshape(m)[order]
    counts = jnp.bincount(e_flat, length=N_EXPERTS)
    padded = (counts + blk - 1) // blk * blk
    pend = jnp.cumsum(padded)
    dest = (pend - padded)[e_s] + jnp.arange(m) - (jnp.cumsum(counts) - counts)[e_s]
    row_tok = jnp.zeros((cap,), jnp.int32).at[dest].set(tok_s)
    blk_exp = jnp.minimum(jnp.searchsorted(pend, jnp.arange(cap // blk) * blk, side='right'), N_EXPERTS - 1)
    wgu, bgu, wd, bd = lp['moe_wgu'], lp['moe_bgu'], lp['moe_wd'], lp['moe_bd']

    def expert_block(args):
        rows, e = args
        gu = x[rows] @ wgu[e] + bgu[e]
        gt, up = jnp.split(gu, 2, axis=-1)
        gt = jnp.minimum(gt, SWIGLU_LIMIT)
        up = jnp.clip(up, -SWIGLU_LIMIT, SWIGLU_LIMIT)
        return ((up + 1.0) * gt * jax.nn.sigmoid(SWIGLU_ALPHA * gt)) @ wd[e] + bd[e]

    out = lax.map(expert_block, (row_tok.reshape(cap // blk, blk), blk_exp)).reshape(cap, d)
    y = jnp.zeros((n, d), jnp.float32).at[tok_s].add(g_s[:, None] * out[dest].astype(jnp.float32))
    return y.reshape(b, t, d).astype(h.dtype)


def _layer(x, c, lp, mixer_fn):
    mod = jax.nn.silu(c) @ lp['w_ada'] + lp['b_ada']
    sh1, sc1, g1, sh2, sc2, g2 = [u[:, None, :] for u in jnp.split(mod, 6, axis=-1)]
    h = _layernorm(x) * (1.0 + sc1) + sh1
    mix, states = mixer_fn(h @ lp['w_in'])
    x = _layernorm(DN_ALPHA * x + g1 * (mix @ lp['w_out']), lp['ln1_g'], lp['ln1_b'])
    h = _layernorm(x) * (1.0 + sc2) + sh2
    x = _layernorm(DN_ALPHA * x + g2 * _moe(h, lp), lp['ln2_g'], lp['ln2_b'])
    return x, states


def _mixers_prompt(z, lp, sl_nsa, sl_dsa):
    b, t, _ = z.shape
    za, zb, zc, zd = _take(z, [A_COLS, B_COLS, C_COLS, D_COLS])
    ya, s_rwkv, shift = _rwkv_mixer(za, jnp.zeros((b, A_COLS), z.dtype),
                                    jnp.zeros((b, A_HEADS, HEAD_DIM, HEAD_DIM), jnp.float32), lp)
    q, kv_c, kv_s, kv_w, gates = _nsa_split(zb)
    kvc = _nsa_compress(kv_c, lp)
    fetch_s = _rows_fetch(kv_s)
    kvw_pad = jnp.pad(kv_w, ((0, 0), (NSA_WINDOW, 0), (0, 0), (0, 0)))
    qc, kv_d, qi, ki, wi = _dsa_split(zc)
    fetch_d = _rows_fetch(kv_d)
    starts = jnp.arange(t // Q_BLOCK, dtype=jnp.int32) * Q_BLOCK

    def nsa_block(q0):
        cut = lambda u: lax.dynamic_slice_in_dim(u, q0, Q_BLOCK, axis=1)
        kvw = lax.dynamic_slice_in_dim(kvw_pad, q0, NSA_WINDOW + Q_BLOCK, axis=1)
        wpos = q0 - NSA_WINDOW + jnp.arange(NSA_WINDOW + Q_BLOCK)
        return _nsa_attend(cut(q), cut(gates), q0 + jnp.arange(Q_BLOCK), kvc, fetch_s, kvw, wpos, sl_nsa, t)

    def dsa_block(q0):
        cut = lambda u: lax.dynamic_slice_in_dim(u, q0, Q_BLOCK, axis=1)
        return _dsa_attend(cut(qc), cut(qi), cut(wi), q0 + jnp.arange(Q_BLOCK), ki, fetch_d, sl_dsa, t)

    unblock = lambda o: jnp.swapaxes(o, 0, 1).reshape(b, t, GROUP_WIDTH).astype(z.dtype)
    yb = unblock(lax.map(nsa_block, starts))
    yc = unblock(lax.map(dsa_block, starts))
    yd, conv_buf = _conv_mixer(zd, jnp.zeros((b, CONV_K - 1, CONV_CH), z.dtype), lp['conv_w'])
    mix = jnp.concatenate([ya, yb, yc, yd], axis=-1)
    return mix, (s_rwkv, shift, kv_c, kv_s, kv_w[:, -min(NSA_WINDOW, t):], kv_d, ki, conv_buf)


def _mixers_sample(z, lp, sl_nsa, sl_dsa, s_rwkv, s_shift, c_nsa_cmp, c_nsa_slc, s_win,
                   c_dsa_kv, c_dsa_idx, s_conv, page_table):
    b, t, _ = z.shape
    n_past = page_table.shape[1] * PAGE_SIZE
    n_buf = s_win.shape[1]
    qpos = n_past + jnp.arange(t)
    za, zb, zc, zd = _take(z, [A_COLS, B_COLS, C_COLS, D_COLS])
    ya, s_rwkv_new, shift = _rwkv_mixer(za, s_shift, s_rwkv, lp)
    q, kv_c, kv_s, kv_w, gates = _nsa_split(zb)
    past_c = c_nsa_cmp[page_table].reshape(b, n_past, 2, HEAD_DIM)
    kvc = _nsa_compress(jnp.concatenate([past_c, kv_c.astype(past_c.dtype)], axis=1), lp)
    kvw = jnp.concatenate([s_win, kv_w.astype(s_win.dtype)], axis=1)
    wpos = n_past - n_buf + jnp.arange(n_buf + t)
    yb = _nsa_attend(q, gates, qpos, kvc, _paged_fetch(c_nsa_slc, page_table, kv_s), kvw, wpos, sl_nsa, n_past + t)
    qc, kv_d, qi, ki, wi = _dsa_split(zc)
    past_i = c_dsa_idx[page_table].reshape(b, n_past, IDX_DIM)
    ki_all = jnp.concatenate([past_i, ki.astype(past_i.dtype)], axis=1)
    yc = _dsa_attend(qc, qi, wi, qpos, ki_all, _paged_fetch(c_dsa_kv, page_table, kv_d), sl_dsa, n_past + t)
    yd, conv_buf = _conv_mixer(zd, s_conv, lp['conv_w'])
    flat = lambda o: o.reshape(b, t, GROUP_WIDTH).astype(z.dtype)
    mix = jnp.concatenate([ya, flat(yb), flat(yc), yd], axis=-1)
    return mix, (s_rwkv_new, shift, kv_c, kv_s, kvw[:, -n_buf:], kv_d, ki, conv_buf)


def setup_inputs(seed: int = 0) -> dict:
    key = jax.random.key(seed)
    keys = jax.random.split(key, 64)
    counter = iter(range(64))
    f32 = jnp.float32

    def nrm(shape, std):
        return std * jax.random.normal(keys[next(counter)], shape, f32)

    n_pages = PAST_LEN // PAGE_SIZE
    n_pool = (5 * DEC_BATCH * n_pages + 3) // 4
    n_buf = min(NSA_WINDOW, PAST_LEN)
    page_table = jax.random.permutation(keys[next(counter)], n_pool)[:DEC_BATCH * n_pages]
    page_table = page_table.reshape(DEC_BATCH, n_pages).astype(jnp.int32)
    d = D_MODEL
    return {
        'x_prompt': nrm((BATCH, SEQ, d), 1.0),
        'x_sample': nrm((DEC_BATCH, DEC_SEQ, d), 1.0),
        'state_rwkv': nrm((DEPTH, DEC_BATCH, A_HEADS, HEAD_DIM, HEAD_DIM), 0.3),
        'state_rwkv_shift': nrm((DEPTH, DEC_BATCH, A_COLS), 1.0),
        'cache_nsa_cmp': nrm((DEPTH, n_pool, PAGE_SIZE, 2, HEAD_DIM), 1.0),
        'cache_nsa_slc': nrm((DEPTH, n_pool, PAGE_SIZE, 2, HEAD_DIM), 1.0),
        'state_nsa_win': nrm((DEPTH, DEC_BATCH, n_buf, 2, HEAD_DIM), 1.0),
        'cache_dsa_kv': nrm((DEPTH, n_pool, PAGE_SIZE, 2, HEAD_DIM), 1.0),
        'cache_dsa_idx': nrm((DEPTH, n_pool, PAGE_SIZE, IDX_DIM), 1.0),
        'state_conv': nrm((DEPTH, DEC_BATCH, CONV_K - 1, CONV_CH), 1.0),
        'page_table': page_table,
        'c_prompt': nrm((BATCH, d), 1.0),
        'c_sample': nrm((DEC_BATCH, d), 1.0),
        'w_ada': nrm((DEPTH, d, 6 * d), 0.5 * d ** -0.5),
        'b_ada': nrm((DEPTH, 6 * d), 0.02),
        'w_in': nrm((DEPTH, d, IN_COLS), d ** -0.5),
        'w_out': nrm((DEPTH, d, d), DN_BETA * d ** -0.5),
        'ln1_g': 1.0 + nrm((DEPTH, d), 0.05),
        'ln1_b': nrm((DEPTH, d), 0.02),
        'ln2_g': 1.0 + nrm((DEPTH, d), 0.05),
        'ln2_b': nrm((DEPTH, d), 0.02),
        'rwkv_mu': jax.random.uniform(keys[next(counter)], (DEPTH, A_COLS), f32),
        'rwkv_w0': nrm((DEPTH, GROUP_WIDTH), 0.5),
        'rwkv_w2': nrm((DEPTH, RWKV_W_RANK, GROUP_WIDTH), RWKV_W_RANK ** -0.5),
        'rwkv_a0': nrm((DEPTH, GROUP_WIDTH), 0.5),
        'rwkv_a2': nrm((DEPTH, RWKV_A_RANK, GROUP_WIDTH), RWKV_A_RANK ** -0.5),
        'rwkv_g2': nrm((DEPTH, RWKV_G_RANK, GROUP_WIDTH), RWKV_G_RANK ** -0.5),
        'rwkv_kk': 0.85 + nrm((DEPTH, A_HEADS, HEAD_DIM), 0.05),
        'rwkv_ka': 1.0 + nrm((DEPTH, A_HEADS, HEAD_DIM), 0.05),
        'rwkv_rk': nrm((DEPTH, A_HEADS, HEAD_DIM), 0.1),
        'rwkv_gn_g': 1.0 + nrm((DEPTH, GROUP_WIDTH), 0.05),
        'rwkv_gn_b': nrm((DEPTH, GROUP_WIDTH), 0.02),
        'nsa_pe': nrm((DEPTH, NSA_CMP_BLOCK, 2, HEAD_DIM), 0.02),
        'nsa_w1': nrm((DEPTH, 2, NSA_CMP_BLOCK * HEAD_DIM, NSA_CMP_HIDDEN), (NSA_CMP_BLOCK * HEAD_DIM) ** -0.5),
        'nsa_w2': nrm((DEPTH, 2, NSA_CMP_HIDDEN, HEAD_DIM), NSA_CMP_HIDDEN ** -0.5),
        'conv_w': nrm((DEPTH, CONV_K, CONV_CH), 0.5),
        'moe_wr': nrm((DEPTH, d, N_EXPERTS), d ** -0.5),
        'moe_br': nrm((DEPTH, N_EXPERTS), 0.01),
        'moe_wgu': nrm((DEPTH, N_EXPERTS, d, 2 * EXPERT_FF), d ** -0.5),
        'moe_bgu': nrm((DEPTH, N_EXPERTS, 2 * EXPERT_FF), 0.01),
        'moe_wd': nrm((DEPTH, N_EXPERTS, EXPERT_FF, d), DN_BETA * EXPERT_FF ** -0.5),
        'moe_bd': nrm((DEPTH, N_EXPERTS, d), 0.01),
    }


def reference(x_prompt, x_sample, state_rwkv, state_rwkv_shift, cache_nsa_cmp, cache_nsa_slc,
              state_nsa_win, cache_dsa_kv, cache_dsa_idx, state_conv, page_table, c_prompt, c_sample,
              w_ada, b_ada, w_in, w_out, ln1_g, ln1_b, ln2_g, ln2_b,
              rwkv_mu, rwkv_w0, rwkv_w2, rwkv_a0, rwkv_a2, rwkv_g2, rwkv_kk, rwkv_ka, rwkv_rk,
              rwkv_gn_g, rwkv_gn_b, nsa_pe, nsa_w1, nsa_w2, conv_w,
              moe_wr, moe_br, moe_wgu, moe_bgu, moe_wd, moe_bd):
    slopes = _alibi_slopes(NSA_HEADS + DSA_HEADS)
    sl_nsa, sl_dsa = slopes[0::2], slopes[1::2]
    hp, hs = x_prompt, x_sample
    states_p, states_s = [], []
    for l in range(DEPTH):
        lp = {
            'w_ada': w_ada[l], 'b_ada': b_ada[l], 'w_in': w_in[l], 'w_out': w_out[l],
            'ln1_g': ln1_g[l], 'ln1_b': ln1_b[l], 'ln2_g': ln2_g[l], 'ln2_b': ln2_b[l],
            'rwkv_mu': rwkv_mu[l], 'rwkv_w0': rwkv_w0[l], 'rwkv_w2': rwkv_w2[l],
            'rwkv_a0': rwkv_a0[l], 'rwkv_a2': rwkv_a2[l], 'rwkv_g2': rwkv_g2[l],
            'rwkv_kk': rwkv_kk[l], 'rwkv_ka': rwkv_ka[l], 'rwkv_rk': rwkv_rk[l],
            'rwkv_gn_g': rwkv_gn_g[l], 'rwkv_gn_b': rwkv_gn_b[l],
            'nsa_pe': nsa_pe[l], 'nsa_w1': nsa_w1[l], 'nsa_w2': nsa_w2[l], 'conv_w': conv_w[l],
            'moe_wr': moe_wr[l], 'moe_br': moe_br[l], 'moe_wgu': moe_wgu[l], 'moe_bgu': moe_bgu[l],
            'moe_wd': moe_wd[l], 'moe_bd': moe_bd[l],
        }
        mix_p = functools.partial(_mixers_prompt, lp=lp, sl_nsa=sl_nsa, sl_dsa=sl_dsa)
        mix_s = functools.partial(
            _mixers_sample, lp=lp, sl_nsa=sl_nsa, sl_dsa=sl_dsa, s_rwkv=state_rwkv[l],
            s_shift=state_rwkv_shift[l], c_nsa_cmp=cache_nsa_cmp[l], c_nsa_slc=cache_nsa_slc[l],
            s_win=state_nsa_win[l], c_dsa_kv=cache_dsa_kv[l], c_dsa_idx=cache_dsa_idx[l],
            s_conv=state_conv[l], page_table=page_table)
        hp, st_p = _layer(hp, c_prompt, lp, mix_p)
        hs, st_s = _layer(hs, c_sample, lp, mix_s)
        states_p.append(st_p)
        states_s.append(st_s)
    rw_p, sh_p, ncmp_p, nslc_p, nwin_p, dkv_p, didx_p, conv_p = [jnp.stack(v) for v in zip(*states_p)]
    rw_s, sh_s, ncmp_s, nslc_s, nwin_s, dkv_s, didx_s, conv_s = [jnp.stack(v) for v in zip(*states_s)]
    return (hp, hs, rw_p, sh_p, ncmp_p, nslc_p, nwin_p, dkv_p, didx_p, conv_p,
            rw_s, sh_s, ncmp_s, nslc_s, nwin_s, dkv_s, didx_s, conv_s)
```

```python
import functools

import jax
import jax.numpy as jnp
import numpy as np
from jax import lax
from jax.experimental import pallas as pl
from jax.experimental.pallas import tpu as pltpu

F32 = jnp.float32
BF16 = jnp.bfloat16

HEAD_DIM = 64
N_HEADS = 4
GROUP_WIDTH = N_HEADS * HEAD_DIM
RWKV_DECAY_SCALE = 0.606531
RWKV_GN_EPS = 64e-5
NSA_CMP_BLOCK = 32
NSA_SEL_BLOCK = 64
NSA_TOPN = 16
NSA_LOCAL_BLOCKS = 2
NSA_WINDOW = 512
DSA_TOPK = 256
IDX_HEADS = 4
IDX_DIM = 32
CONV_K = 3
TOP_K = 4
SWIGLU_LIMIT = 7.0
SWIGLU_ALPHA = 1.702
NEG_INF = -1e30
FORCE_SCORE = 1e4
Q_BLOCK = 128
PAGE_SIZE = 128

LANES = 128
A_COLS = 3 * GROUP_WIDTH + 32 + 32 + 64
B_COLS = GROUP_WIDTH + 6 * HEAD_DIM + 3 * N_HEADS
C_COLS = GROUP_WIDTH + 2 * HEAD_DIM + IDX_HEADS * IDX_DIM + IDX_DIM + IDX_HEADS
D_COLS = 3 * GROUP_WIDTH


def _pad_to(n, m):
    return -(-n // m) * m


A_PAD, B_PAD, C_PAD, D_PAD = (_pad_to(c, LANES) for c in (A_COLS, B_COLS, C_COLS, D_COLS))
VMEM_LIMIT = 56 * 1024 * 1024

_NT = (((1,), (1,)), ((), ()))


def _alibi_slopes():
    n = 2 * N_HEADS
    s = [2.0 ** (-8.0 * i / n) for i in range(1, n + 1)]
    return s[0::2], s[1::2]


def _ln_rows(x, eps=1e-5):
    mu = jnp.mean(x, axis=-1, keepdims=True)
    xc = x - mu
    var = jnp.mean(xc * xc, axis=-1, keepdims=True)
    return xc * lax.rsqrt(var + eps)


def _cparams(sem):
    return pltpu.CompilerParams(dimension_semantics=sem, vmem_limit_bytes=VMEM_LIMIT)


def _inproj_kernel(x_ref, sc_ref, sh_ref, w_ref, za_ref, zb_ref, zc_ref, zd_ref):
    h = _ln_rows(x_ref[0]) * (1.0 + sc_ref[0]) + sh_ref[0]
    z = jnp.dot(h.astype(BF16), w_ref[...], preferred_element_type=F32)
    o = 0
    for ref, w in ((za_ref, A_PAD), (zb_ref, B_PAD), (zc_ref, C_PAD), (zd_ref, D_PAD)):
        ref[0] = z[:, o:o + w]
        o += w


def _inproj(x, sc, sh, w_pad):
    b, t, d = x.shape
    tm = min(t, 512)
    mod_rows = sc.shape[1]
    mod_blk = (1, tm, d) if mod_rows == t and t > 1 else (1, 1, d)
    mod_map = (lambda i, j: (i, j, 0)) if mod_rows == t and t > 1 else (lambda i, j: (i, 0, 0))
    widths = (A_PAD, B_PAD, C_PAD, D_PAD)
    return pl.pallas_call(
        _inproj_kernel,
        grid=(b, t // tm),
        in_specs=[pl.BlockSpec((1, tm, d), lambda i, j: (i, j, 0)),
                  pl.BlockSpec(mod_blk, mod_map),
                  pl.BlockSpec(mod_blk, mod_map),
                  pl.BlockSpec(w_pad.shape, lambda i, j: (0, 0))],
        out_specs=[pl.BlockSpec((1, tm, w), lambda i, j: (i, j, 0)) for w in widths],
        out_shape=[jax.ShapeDtypeStruct((b, t, w), F32) for w in widths],
        compiler_params=_cparams(("parallel", "parallel")),
        name="inproj",
    )(x, sc, sh, w_pad)


def _rwkv_scan_kernel(r_ref, w_ref, k_ref, v_ref, kk_ref, kka_ref, s0_ref, y_ref, sfin_ref,
                      s_scr, y_scr, *, bg, tc, unroll):
    c = pl.program_id(1)
    rows = bg * HEAD_DIM

    @pl.when(c == 0)
    def _():
        s_scr[...] = s0_ref[...].reshape(rows, GROUP_WIDTH)

    li = lax.broadcasted_iota(jnp.int32, (2 * GROUP_WIDTH, GROUP_WIDTH), 0)
    lj = lax.broadcasted_iota(jnp.int32, (2 * GROUP_WIDTH, GROUP_WIDTH), 1)
    seg_ones2 = jnp.where((li % GROUP_WIDTH) // HEAD_DIM == lj // HEAD_DIM, 1.0, 0.0).astype(BF16)
    seg_ones1 = seg_ones2[:GROUP_WIDTH]
    lane = lax.broadcasted_iota(jnp.int32, (rows, GROUP_WIDTH), 1)
    row = lax.broadcasted_iota(jnp.int32, (rows, GROUP_WIDTH), 0)
    diag = jnp.where(row % HEAD_DIM == lane % HEAD_DIM, 1.0, 0.0)

    def seg_sum(x):
        hi = x.astype(BF16)
        lo = (x - hi.astype(F32)).astype(BF16)
        return jnp.dot(jnp.concatenate([hi, lo], axis=1), seg_ones2, preferred_element_type=F32)

    def bcast_rows(ref, t):
        return jnp.concatenate(
            [jnp.broadcast_to(ref[b, pl.ds(t, 1), :], (HEAD_DIM, GROUP_WIDTH)) for b in range(bg)], axis=0)

    def step(t):
        s = s_scr[...]
        kk = bcast_rows(kk_ref, t)
        sa = seg_sum(s * kk)
        vcol = seg_sum(bcast_rows(v_ref, t) * diag)
        s = s * bcast_rows(w_ref, t) - sa * bcast_rows(kka_ref, t) + vcol * bcast_rows(k_ref, t)
        s_scr[...] = s
        y = jnp.dot((s * bcast_rows(r_ref, t)).astype(BF16), seg_ones1, preferred_element_type=F32)
        y_scr[...] = jnp.where(lane % HEAD_DIM == t, y, y_scr[...])

    def outer(i, carry):
        for u in range(unroll):
            step(i * unroll + u)
        return carry

    lax.fori_loop(0, tc // unroll, outer, 0)
    y_ref[...] = y_scr[...].reshape(bg, 1, HEAD_DIM, GROUP_WIDTH)

    @pl.when(c == pl.num_programs(1) - 1)
    def _():
        sfin_ref[...] = s_scr[...].reshape(bg, HEAD_DIM, GROUP_WIDTH)


def _rwkv_scan(r, w, k, v, kk, kka, s0):
    b, t, gw = r.shape
    bg = 8 if b % 8 == 0 else b
    tc = min(t, HEAD_DIM)
    unroll = 4 if tc % 4 == 0 else 1
    nc = t // tc
    s0l = s0.transpose(0, 2, 1, 3).reshape(b, HEAD_DIM, gw)
    seq = pl.BlockSpec((bg, tc, gw), lambda i, c: (i, c, 0))
    st = pl.BlockSpec((bg, HEAD_DIM, gw), lambda i, c: (i, 0, 0))
    y_raw, s_fin = pl.pallas_call(
        functools.partial(_rwkv_scan_kernel, bg=bg, tc=tc, unroll=unroll),
        grid=(b // bg, nc),
        in_specs=[seq] * 6 + [st],
        out_specs=[pl.BlockSpec((bg, 1, HEAD_DIM, gw), lambda i, c: (i, c, 0, 0)), st],
        out_shape=[jax.ShapeDtypeStruct((b, nc, HEAD_DIM, gw), F32),
                   jax.ShapeDtypeStruct((b, HEAD_DIM, gw), F32)],
        scratch_shapes=[pltpu.VMEM((bg * HEAD_DIM, gw), F32), pltpu.VMEM((bg * HEAD_DIM, gw), F32)],
        compiler_params=_cparams(("parallel", "arbitrary")),
        name="rwkv_scan",
    )(r, w, k, v, kk, kka, s0l)
    y = y_raw.reshape(b, nc, HEAD_DIM, N_HEADS, HEAD_DIM)[..., :tc]
    y = y.transpose(0, 1, 4, 3, 2).reshape(b, t, N_HEADS, HEAD_DIM)
    s_fin = s_fin.reshape(b, HEAD_DIM, N_HEADS, HEAD_DIM).transpose(0, 2, 1, 3)
    return y, s_fin


def _rwkv_mixer(za, shift_prev, s0, lp):
    b, t, _ = za.shape
    z_prev = jnp.concatenate([shift_prev[:, None], za[:, :-1]], axis=1)
    zs = za + lp['rwkv_mu'] * (z_prev - za)
    gw = GROUP_WIDTH
    r, k, v = zs[..., :gw], zs[..., gw:2 * gw], zs[..., 2 * gw:3 * gw]
    o = 3 * gw
    w_lo, a_lo, g_lo = zs[..., o:o + 32], zs[..., o + 32:o + 64], zs[..., o + 64:o + 128]
    decay = jnp.exp(-RWKV_DECAY_SCALE * jax.nn.sigmoid(lp['rwkv_w0'] + jnp.tanh(w_lo) @ lp['rwkv_w2']))
    a = jax.nn.sigmoid(lp['rwkv_a0'] + a_lo @ lp['rwkv_a2'])
    g = jax.nn.sigmoid(g_lo) @ lp['rwkv_g2']
    heads = lambda u: u.reshape(b, t, N_HEADS, HEAD_DIM)
    kk = heads(k) * lp['rwkv_kk']
    kk = kk / jnp.maximum(jnp.sqrt(jnp.sum(kk * kk, axis=-1, keepdims=True)), 1e-12)
    kk = kk.reshape(b, t, gw)
    k = k * (1.0 + (a - 1.0) * lp['rwkv_ka'].reshape(gw))
    y, s_fin = _rwkv_scan(r, decay, k, v, kk, kk * a, s0)
    mu = jnp.mean(y, axis=-1, keepdims=True)
    var = jnp.mean(jnp.square(y - mu), axis=-1, keepdims=True)
    yn = ((y - mu) * lax.rsqrt(var + RWKV_GN_EPS) * lp['rwkv_gn_g'].reshape(N_HEADS, HEAD_DIM)
          + lp['rwkv_gn_b'].reshape(N_HEADS, HEAD_DIM))
    bonus = jnp.sum(heads(r) * heads(k) * lp['rwkv_rk'], axis=-1, keepdims=True) * heads(v)
    out = (yn + bonus).reshape(b, t, gw) * g
    return out, s_fin, za[:, -1, :A_COLS]


def _head_q(q2, h, lane):
    pair = q2[:, (h // 2) * LANES:(h // 2 + 1) * LANES]
    if h % 2:
        pair = pltpu.roll(pair, HEAD_DIM, 1)
    return jnp.where(lane < HEAD_DIM, pair, 0.0).astype(BF16)


def _masked_attend(qh, kv, bias_row, mask):
    s = lax.dot_general(qh, kv, _NT, preferred_element_type=F32)
    s = jnp.where(mask, s + bias_row, NEG_INF)
    m = jnp.max(s, axis=1, keepdims=True)
    e = jnp.where(mask, jnp.exp(s - m), 0.0)
    l = jnp.sum(e, axis=1, keepdims=True)
    inv = 1.0 / jnp.maximum(l, 1e-30)
    o = jnp.dot(e.astype(BF16), kv, preferred_element_type=F32) * inv
    return e, inv, o


def _pair_out(o_even, o_odd, lane):
    return jnp.where(lane < HEAD_DIM, pltpu.roll(o_even, HEAD_DIM, 1), o_odd)


def _nsa_kernel(q_ref, ksv_ref, kwv_ref, gl_ref, kvc_ref, exp_ref, o_ref, *, t, n_win, n_top, slopes):
    qb = Q_BLOCK
    q0 = pl.program_id(1) * qb
    lane = lax.broadcasted_iota(jnp.int32, (qb, LANES), 1)
    qpos = q0 + lax.broadcasted_iota(jnp.int32, (qb, 1), 0)
    q2 = q_ref[0] * (HEAD_DIM ** -0.5)
    gates = jax.nn.sigmoid(gl_ref[0])

    kvc = kvc_ref[0].astype(BF16)
    cl = lax.broadcasted_iota(jnp.int32, (1, LANES), 1)
    cend = (2 * (cl % HEAD_DIM) + cl // HEAD_DIM) * NSA_CMP_BLOCK + (NSA_CMP_BLOCK - 1)
    cmask = cend <= qpos
    cendf = cend.astype(F32)
    imp = jnp.zeros((qb, LANES), F32)
    o_c = []
    for h in range(N_HEADS):
        e, inv, o = _masked_attend(_head_q(q2, h, lane), kvc, slopes[h] * cendf, cmask)
        imp = imp + e * inv
        o_c.append(o)
    imp = imp + pltpu.roll(imp, HEAD_DIM, 1)

    cur = qpos // NSA_SEL_BLOCK
    forced = (lane == 0) | (lane > cur - NSA_LOCAL_BLOCKS)
    score = jnp.where(lane > cur, NEG_INF, jnp.where(forced, FORCE_SCORE, imp))
    score = jnp.where(lane < HEAD_DIM, score, -3e38)
    rank = jnp.zeros((qb, LANES), F32)
    for i in range(NSA_SEL_BLOCK):
        col = score[:, i:i + 1]
        beats = (col > score) | ((col == score) & (lane > i))
        rank = rank + jnp.where(beats, 1.0, 0.0)
    sel = jnp.where((rank < n_top) & (lane < HEAD_DIM), 1.0, 0.0).astype(BF16)
    kpos = lax.broadcasted_iota(jnp.int32, (1, t), 1)
    smask = (jnp.dot(sel, exp_ref[...], preferred_element_type=F32) > 0.5) & (kpos <= qpos)
    kposf = kpos.astype(F32)

    kst = pl.multiple_of(jnp.maximum(q0 + qb - n_win, 0), qb)
    kw = kwv_ref[0, pl.ds(kst, n_win), :].astype(BF16)
    wpos = kst + lax.broadcasted_iota(jnp.int32, (1, n_win), 1)
    dw = qpos - wpos
    wmask = (dw >= 0) & (dw <= NSA_WINDOW)
    wposf = wpos.astype(F32)

    ks = ksv_ref[0].astype(BF16)
    outs = []
    for h in range(N_HEADS):
        qh = _head_q(q2, h, lane)
        _, _, o_s = _masked_attend(qh, ks, slopes[h] * kposf, smask)
        _, _, o_w = _masked_attend(qh, kw, slopes[h] * wposf, wmask)
        g = [gates[:, 3 * h + j:3 * h + j + 1] for j in range(3)]
        outs.append(g[0] * o_c[h] + g[1] * o_s + g[2] * o_w)
    o_ref[0] = jnp.concatenate([_pair_out(outs[0], outs[1], lane), _pair_out(outs[2], outs[3], lane)], axis=1)


def _nsa_prompt(zb, kvc_perm, slopes):
    b, t, _ = zb.shape
    assert t % Q_BLOCK == 0 and t // NSA_CMP_BLOCK <= LANES
    n_win = min(NSA_WINDOW + Q_BLOCK, t)
    n_sel = -(-t // NSA_SEL_BLOCK)
    expand = (np.arange(LANES)[:, None] == (np.arange(t)[None, :] // NSA_SEL_BLOCK)).astype(np.float32)
    expand = jnp.asarray(expand, BF16)
    full = lambda c: pl.BlockSpec((1, t, LANES), lambda i, j, c=c: (i, 0, c))
    return pl.pallas_call(
        functools.partial(_nsa_kernel, t=t, n_win=n_win, n_top=min(NSA_TOPN, n_sel), slopes=slopes),
        grid=(b, t // Q_BLOCK),
        in_specs=[pl.BlockSpec((1, Q_BLOCK, 2 * LANES), lambda i, j: (i, j, 0)),
                  full(3), full(4),
                  pl.BlockSpec((1, Q_BLOCK, LANES), lambda i, j: (i, j, 5)),
                  pl.BlockSpec((1, LANES, LANES), lambda i, j: (i, 0, 0)),
                  pl.BlockSpec((LANES, t), lambda i, j: (0, 0))],
        out_specs=pl.BlockSpec((1, Q_BLOCK, GROUP_WIDTH), lambda i, j: (i, j, 0)),
        out_shape=jax.ShapeDtypeStruct((b, t, GROUP_WIDTH), F32),
        compiler_params=_cparams(("parallel", "parallel")),
        name="nsa_prompt",
    )(zb, zb, zb, zb, kvc_perm, expand)


def _nsa_compress(rows, lp):
    b, l = rows.shape[:2]
    nc = l // NSA_CMP_BLOCK
    blk = rows[:, :nc * NSA_CMP_BLOCK].reshape(b, nc, NSA_CMP_BLOCK, 2, HEAD_DIM) + lp['nsa_pe']
    flat = jnp.swapaxes(blk, 2, 3).reshape(b, nc, 2, NSA_CMP_BLOCK * HEAD_DIM)
    hid = jax.nn.gelu(jnp.einsum('bnci,cih->bnch', flat, lp['nsa_w1']))
    return jnp.einsum('bnch,chd->bncd', hid, lp['nsa_w2'])


def _dsa_kernel(q_ref, kv_ref, qi_ref, kiw_ref, wiq_ref, o_ref, sel_scr, *, t, k_sel, slopes):
    qb = Q_BLOCK
    q0 = pl.program_id(1) * qb
    lane = lax.broadcasted_iota(jnp.int32, (qb, LANES), 1)
    qpos = q0 + lax.broadcasted_iota(jnp.int32, (qb, 1), 0)
    kpos = lax.broadcasted_iota(jnp.int32, (1, t), 1)
    causal = kpos <= qpos

    kiw = kiw_ref[0]
    qi = qi_ref[0]
    wq = wiq_ref[0]
    score = jnp.zeros((qb, t), F32)
    for h in range(IDX_HEADS):
        qh = qi if h == 0 else pltpu.roll(qi, LANES - IDX_DIM * h, 1)
        qh = jnp.where(lane < IDX_DIM, qh, 0.0)
        rel = lax.dot_general(qh, kiw, _NT, preferred_element_type=F32, precision=lax.Precision.HIGHEST)
        score = score + wq[:, IDX_DIM + h:IDX_DIM + h + 1] * jnp.maximum(rel, 0.0)
    score = jnp.where(score == 0.0, 0.0, score)
    score = jnp.where(causal, score, NEG_INF)

    bits = pltpu.bitcast(score, jnp.int32)
    skey = bits ^ ((bits >> 31) & 0x7FFFFFFF)
    kf = float(k_sel)

    def count_ge(c):
        return jnp.sum(jnp.where(skey >= c, 1.0, 0.0), axis=1, keepdims=True)

    int_min = jnp.int32(-2 ** 31)
    tau = jnp.where(count_ge(jnp.zeros((qb, 1), jnp.int32)) >= kf, jnp.int32(0), int_min)

    def bit_step(j, tau):
        cand = tau + lax.shift_left(jnp.int32(1), 30 - j)
        return jnp.where(count_ge(cand) >= kf, cand, tau)

    tau = lax.fori_loop(0, 31, bit_step, tau)
    need = kf - jnp.sum(jnp.where(skey > tau, 1.0, 0.0), axis=1, keepdims=True)
    ui = lax.broadcasted_iota(jnp.int32, (LANES, LANES), 0)
    uj = lax.broadcasted_iota(jnp.int32, (LANES, LANES), 1)
    upper = jnp.where(ui <= uj, 1.0, 0.0).astype(BF16)
    carry = jnp.zeros((qb, 1), F32)
    for c in range(t // LANES):
        sl = slice(c * LANES, (c + 1) * LANES)
        kc = skey[:, sl]
        eqc = kc == tau
        eqf = jnp.where(eqc, 1.0, 0.0)
        pre = jnp.dot(eqf.astype(BF16), upper, preferred_element_type=F32) + carry
        sel_scr[:, sl] = jnp.where((kc > tau) | (eqc & (pre <= need)), 1.0, 0.0)
        carry = carry + jnp.sum(eqf, axis=1, keepdims=True)
    mask = (sel_scr[...] > 0.5) & causal

    q2 = q_ref[0] * (HEAD_DIM ** -0.5)
    kv = kv_ref[0].astype(BF16)
    kposf = kpos.astype(F32)
    outs = [_masked_attend(_head_q(q2, h, lane), kv, slopes[h] * kposf, mask)[2] for h in range(N_HEADS)]
    o_ref[0] = jnp.concatenate([_pair_out(outs[0], outs[1], lane), _pair_out(outs[2], outs[3], lane)], axis=1)


def _dsa_prompt(zc, slopes):
    b, t, _ = zc.shape
    assert t % Q_BLOCK == 0
    k_sel = min(DSA_TOPK, t // 4)
    full = lambda c: pl.BlockSpec((1, t, LANES), lambda i, j, c=c: (i, 0, c))
    qblk = lambda c: pl.BlockSpec((1, Q_BLOCK, LANES), lambda i, j, c=c: (i, j, c))
    return pl.pallas_call(
        functools.partial(_dsa_kernel, t=t, k_sel=k_sel, slopes=slopes),
        grid=(b, t // Q_BLOCK),
        in_specs=[pl.BlockSpec((1, Q_BLOCK, 2 * LANES), lambda i, j: (i, j, 0)),
                  full(2), qblk(3), full(4), qblk(4)],
        out_specs=pl.BlockSpec((1, Q_BLOCK, GROUP_WIDTH), lambda i, j: (i, j, 0)),
        out_shape=jax.ShapeDtypeStruct((b, t, GROUP_WIDTH), F32),
        scratch_shapes=[pltpu.VMEM((Q_BLOCK, t), F32)],
        compiler_params=_cparams(("parallel", "parallel")),
        name="dsa_prompt",
    )(zc, zc, zc, zc, zc)


def _outproj_kernel(ya_ref, yb_ref, yc_ref, yd_ref, x_ref, g1_ref, sc2_ref, sh2_ref, wo_ref, lg_ref, lb_ref,
                    wr_ref, br_ref, x1_ref, h2_ref, logit_ref, *, alpha):
    mix = jnp.zeros(x_ref.shape[1:], F32)
    for i, ref in enumerate((ya_ref, yb_ref, yc_ref, yd_ref)):
        mix = mix + jnp.dot(ref[0].astype(BF16), wo_ref[i], preferred_element_type=F32)
    x1 = _ln_rows(alpha * x_ref[0] + g1_ref[0] * mix) * lg_ref[...] + lb_ref[...]
    x1_ref[0] = x1
    h2 = _ln_rows(x1) * (1.0 + sc2_ref[0]) + sh2_ref[0]
    h2_ref[0] = h2.astype(BF16)
    logit_ref[0] = jnp.dot(h2, wr_ref[...], preferred_element_type=F32,
                           precision=lax.Precision.HIGHEST) + br_ref[...]


def _outproj(ys, x, g1, sc2, sh2, wo, ln_g, ln_b, wr_pad, br_pad, alpha):
    b, t, d = x.shape
    tm = min(t, 512)
    per_row = g1.shape[1] == t and t > 1
    mod_blk = (1, tm, d) if per_row else (1, 1, d)
    mod_map = (lambda i, j: (i, j, 0)) if per_row else (lambda i, j: (i, 0, 0))
    mod = pl.BlockSpec(mod_blk, mod_map)
    row = lambda w: pl.BlockSpec((1, tm, w), lambda i, j: (i, j, 0))
    const = lambda a: pl.BlockSpec(a.shape, lambda i, j: (0,) * a.ndim)
    return pl.pallas_call(
        functools.partial(_outproj_kernel, alpha=alpha),
        grid=(b, t // tm),
        in_specs=[row(GROUP_WIDTH)] * 4 + [row(d), mod, mod, mod, const(wo), const(ln_g), const(ln_b),
                                           const(wr_pad), const(br_pad)],
        out_specs=[row(d), row(d), row(LANES)],
        out_shape=[jax.ShapeDtypeStruct((b, t, d), F32), jax.ShapeDtypeStruct((b, t, d), BF16),
                   jax.ShapeDtypeStruct((b, t, LANES), F32)],
        compiler_params=_cparams(("parallel", "parallel")),
        name="outproj",
    )(*ys, x, g1, sc2, sh2, wo, ln_g, ln_b, wr_pad, br_pad)


def _moe_kernel(be_ref, nu_ref, x_ref, wgu_ref, bgu_ref, wd_ref, bd_ref, o_ref):
    i = pl.program_id(0)
    ff = wd_ref.shape[1]

    @pl.when(i < nu_ref[0])
    def _():
        gu = jnp.dot(x_ref[...], wgu_ref[0], preferred_element_type=F32) + bgu_ref[0]
        gt = jnp.minimum(gu[:, :ff], SWIGLU_LIMIT)
        up = jnp.clip(gu[:, ff:], -SWIGLU_LIMIT, SWIGLU_LIMIT)
        act = (up + 1.0) * gt * jax.nn.sigmoid(SWIGLU_ALPHA * gt)
        o_ref[...] = jnp.dot(act.astype(BF16), wd_ref[0], preferred_element_type=F32) + bd_ref[0]

    @pl.when(i >= nu_ref[0])
    def _():
        o_ref[...] = jnp.zeros(o_ref.shape, F32)


def _moe_experts(xg, blk_exp, n_used, wgu, bgu, wd, bd, blk):
    cap, d = xg.shape
    e, _, ff2 = wgu.shape
    ff = ff2 // 2
    grid_spec = pltpu.PrefetchScalarGridSpec(
        num_scalar_prefetch=2,
        grid=(cap // blk,),
        in_specs=[pl.BlockSpec((blk, d), lambda i, be, nu: (i, 0)),
                  pl.BlockSpec((1, d, ff2), lambda i, be, nu: (be[i], 0, 0)),
                  pl.BlockSpec((1, 1, ff2), lambda i, be, nu: (be[i], 0, 0)),
                  pl.BlockSpec((1, ff, d), lambda i, be, nu: (be[i], 0, 0)),
                  pl.BlockSpec((1, 1, d), lambda i, be, nu: (be[i], 0, 0))],
        out_specs=pl.BlockSpec((blk, d), lambda i, be, nu: (i, 0)),
    )
    return pl.pallas_call(
        _moe_kernel,
        grid_spec=grid_spec,
        out_shape=jax.ShapeDtypeStruct((cap, d), F32),
        compiler_params=_cparams(("arbitrary",)),
        name="moe_experts",
    )(blk_exp, n_used, xg, wgu, bgu.reshape(e, 1, ff2), wd, bd.reshape(e, 1, d))


def _moe(h2, logits, mw, blk):
    n, d = h2.shape
    n_exp = logits.shape[1]
    top_v, top_e = lax.top_k(logits, TOP_K)
    gate = jax.nn.softmax(top_v, axis=-1)
    m = n * TOP_K
    cap = -(-(m + n_exp * (blk - 1)) // blk) * blk
    e_flat = top_e.reshape(m)
    onehot = (e_flat[:, None] == jnp.arange(n_exp)[None, :]).astype(jnp.int32)
    csum = jnp.cumsum(onehot, axis=0)
    counts = csum[-1]
    rank = jnp.take_along_axis(csum, e_flat[:, None], axis=1)[:, 0] - 1
    padded = (counts + blk - 1) // blk * blk
    pend = jnp.cumsum(padded)
    pstart = pend - padded
    cstart = jnp.cumsum(counts) - counts
    pos = pstart[e_flat] + rank
    order = jnp.argsort(e_flat)
    nblk = cap // blk
    blk_exp = jnp.minimum(jnp.searchsorted(pend, jnp.arange(nblk) * blk, side='right'), n_exp - 1).astype(jnp.int32)
    slot = jnp.arange(cap)
    e_p = blk_exp[slot // blk]
    r_p = slot - pstart[e_p]
    src = order[jnp.clip(cstart[e_p] + r_p, 0, m - 1)]
    row_tok = jnp.where(r_p < counts[e_p], src // TOP_K, 0)
    n_used = (pend[-1] // blk).astype(jnp.int32).reshape(1)
    out = _moe_experts(jnp.take(h2, row_tok, axis=0), blk_exp, n_used, *mw, blk)
    picked = jnp.take(out, pos, axis=0).reshape(n, TOP_K, d)
    return jnp.sum(gate[..., None] * picked, axis=1)


def _moe_block_rows(n_assign, n_exp):
    per_expert = max(1, n_assign // n_exp)
    return min(256, max(16, 1 << (per_expert.bit_length() - 1)))


def _resln_kernel(x_ref, y_ref, g_ref, lg_ref, lb_ref, o_ref, *, alpha):
    o_ref[0] = _ln_rows(alpha * x_ref[0] + g_ref[0] * y_ref[0]) * lg_ref[...] + lb_ref[...]


def _resln(x, y, g, ln_g, ln_b, alpha):
    b, t, d = x.shape
    tm = min(t, 512)
    per_row = g.shape[1] == t and t > 1
    mod = pl.BlockSpec((1, tm, d) if per_row else (1, 1, d),
                       (lambda i, j: (i, j, 0)) if per_row else (lambda i, j: (i, 0, 0)))
    row = pl.BlockSpec((1, tm, d), lambda i, j: (i, j, 0))
    const = pl.BlockSpec((1, d), lambda i, j: (0, 0))
    return pl.pallas_call(
        functools.partial(_resln_kernel, alpha=alpha),
        grid=(b, t // tm),
        in_specs=[row, row, mod, const, const],
        out_specs=row,
        out_shape=jax.ShapeDtypeStruct((b, t, d), F32),
        compiler_params=_cparams(("parallel", "parallel")),
        name="resln",
    )(x, y, g, ln_g, ln_b)


def _conv_mixer(zd, buf, conv_w):
    t = zd.shape[1]
    ch = GROUP_WIDTH
    b_gate, c_gate, xin = zd[..., :ch], zd[..., ch:2 * ch], zd[..., 2 * ch:3 * ch]
    ue = jnp.concatenate([buf, c_gate * xin], axis=1)
    y = sum(ue[:, j:j + t] * conv_w[j] for j in range(CONV_K))
    return b_gate * y, ue[:, t:]


def _masked_softmax(s, valid):
    p = jax.nn.softmax(jnp.where(valid, s, NEG_INF), axis=-1)
    return jnp.where(valid, p, 0.0)


def _paged_fetch(pool, page_table, new_rows):
    n_past = page_table.shape[1] * PAGE_SIZE
    n_new = new_rows.shape[1]

    def fetch(pos):
        b = pos.shape[0]
        flat = pos.reshape(b, -1)
        bi = jnp.arange(b)[:, None]
        pc = jnp.clip(flat, 0, n_past - 1)
        past = pool[page_table[bi, pc // PAGE_SIZE], pc % PAGE_SIZE]
        new = new_rows[bi, jnp.clip(flat - n_past, 0, n_new - 1)]
        is_past = (flat < n_past).reshape(flat.shape + (1,) * (past.ndim - 2))
        return jnp.where(is_past, past, new).reshape(pos.shape + past.shape[2:])
    return fetch


def _nsa_attend_sample(q, gates, qpos, kvc, fetch_slc, kvw, wpos, slopes, n_keys):
    b, tq = q.shape[:2]
    q = q * HEAD_DIM ** -0.5
    sl = jnp.asarray(slopes, F32)[:, None, None]
    nc = kvc.shape[1]
    cend = jnp.arange(nc) * NSA_CMP_BLOCK + (NSA_CMP_BLOCK - 1)
    s_c = jnp.einsum('bqhd,bnd->bhqn', q, kvc[:, :, 0]) - sl * (qpos[:, None] - cend[None, :]).astype(F32)
    p_c = _masked_softmax(s_c, cend[None, :] <= qpos[:, None])
    o_c = jnp.einsum('bhqn,bnd->bqhd', p_c, kvc[:, :, 1])
    ratio = NSA_SEL_BLOCK // NSA_CMP_BLOCK
    n_sel = -(-n_keys // NSA_SEL_BLOCK)
    imp = jnp.pad(p_c.sum(1), ((0, 0), (0, 0), (0, n_sel * ratio - nc))).reshape(b, tq, n_sel, ratio).sum(-1)
    blk = jnp.arange(n_sel)[None, :]
    cur = (qpos // NSA_SEL_BLOCK)[:, None]
    forced = (blk == 0) | (blk > cur - NSA_LOCAL_BLOCKS)
    score = jnp.where(blk > cur, NEG_INF, jnp.where(forced, FORCE_SCORE, imp))
    n_top = min(NSA_TOPN, n_sel)
    _, sel = lax.top_k(score, n_top)
    kpos = (sel[..., None] * NSA_SEL_BLOCK + jnp.arange(NSA_SEL_BLOCK)).reshape(b, tq, n_top * NSA_SEL_BLOCK)
    kv_s = fetch_slc(kpos)
    s_s = (jnp.einsum('bqhd,bqnd->bhqn', q, kv_s[..., 0, :])
           - sl * (qpos[None, None, :, None] - kpos[:, None]).astype(F32))
    p_s = _masked_softmax(s_s, (kpos <= qpos[None, :, None])[:, None])
    o_s = jnp.einsum('bhqn,bqnd->bqhd', p_s, kv_s[..., 1, :])
    dw = qpos[:, None] - wpos[None, :]
    s_w = jnp.einsum('bqhd,bnd->bhqn', q, kvw[:, :, 0]) - sl * dw.astype(F32)
    p_w = _masked_softmax(s_w, (dw >= 0) & (dw <= NSA_WINDOW) & (wpos[None, :] >= 0))
    o_w = jnp.einsum('bhqn,bnd->bqhd', p_w, kvw[:, :, 1])
    return gates[..., 0:1] * o_c + gates[..., 1:2] * o_s + gates[..., 2:3] * o_w


def _dsa_attend_sample(q, qi, wi, qpos, kidx, fetch_kv, slopes, n_keys):
    l = kidx.shape[1]
    sl = jnp.asarray(slopes, F32)[:, None, None]
    rel = jax.nn.relu(jnp.einsum('bqhi,bsi->bqhs', qi, kidx, precision=lax.Precision.HIGHEST))
    score = jnp.sum(wi[..., None] * rel, axis=2)
    score = jnp.where(jnp.arange(l)[None, None, :] <= qpos[None, :, None], score, NEG_INF)
    k_sel = min(DSA_TOPK, n_keys // 4)
    _, sel = lax.top_k(score, k_sel)
    kv = fetch_kv(sel)
    s = (jnp.einsum('bqhd,bqnd->bhqn', q * HEAD_DIM ** -0.5, kv[..., 0, :])
         - sl * (qpos[None, None, :, None] - sel[:, None]).astype(F32))
    p = _masked_softmax(s, (sel <= qpos[None, :, None])[:, None])
    return jnp.einsum('bhqn,bqnd->bqhd', p, kv[..., 1, :])


def _kv_rows(u):
    return u.reshape(u.shape[0], u.shape[1], 2, HEAD_DIM)


def _mixers_prompt(za, zb, zc, zd, lp, sl_nsa, sl_dsa):
    b, t, _ = za.shape
    gw = GROUP_WIDTH
    ya, s_rwkv, shift = _rwkv_mixer(za, jnp.zeros((b, A_PAD), F32),
                                    jnp.zeros((b, N_HEADS, HEAD_DIM, HEAD_DIM), F32), lp)
    kv_c = _kv_rows(zb[..., gw:gw + 2 * HEAD_DIM])
    kv_s = _kv_rows(zb[..., gw + 2 * HEAD_DIM:gw + 4 * HEAD_DIM])
    kv_w = _kv_rows(zb[..., gw + 4 * HEAD_DIM:gw + 6 * HEAD_DIM])
    kvc = _nsa_compress(kv_c, lp).reshape(b, -1, 2 * HEAD_DIM)
    nc = kvc.shape[1]
    kvc = jnp.pad(kvc, ((0, 0), (0, LANES - nc), (0, 0)))
    kvc = jnp.concatenate([kvc[:, 0::2], kvc[:, 1::2]], axis=1)
    yb = _nsa_prompt(zb, kvc, sl_nsa)
    yc = _dsa_prompt(zc, sl_dsa)
    kv_d = _kv_rows(zc[..., gw:gw + 2 * HEAD_DIM])
    o = gw + 2 * HEAD_DIM + IDX_HEADS * IDX_DIM
    ki = zc[..., o:o + IDX_DIM]
    yd, conv_buf = _conv_mixer(zd, jnp.zeros((b, CONV_K - 1, gw), F32), lp['conv_w'])
    return (ya, yb, yc, yd), (s_rwkv, shift, kv_c, kv_s, kv_w[:, -min(NSA_WINDOW, t):], kv_d, ki, conv_buf)


def _mixers_sample(za, zb, zc, zd, lp, sl_nsa, sl_dsa, s_rwkv, s_shift, c_nsa_cmp, c_nsa_slc, s_win,
                   c_dsa_kv, c_dsa_idx, s_conv, page_table):
    b, t, _ = za.shape
    gw = GROUP_WIDTH
    n_past = page_table.shape[1] * PAGE_SIZE
    n_buf = s_win.shape[1]
    qpos = n_past + jnp.arange(t)
    shift_prev = jnp.pad(s_shift, ((0, 0), (0, A_PAD - A_COLS)))
    ya, s_rwkv_new, shift = _rwkv_mixer(za, shift_prev, s_rwkv, lp)
    q = zb[..., :gw].reshape(b, t, N_HEADS, HEAD_DIM)
    kv_c = _kv_rows(zb[..., gw:gw + 2 * HEAD_DIM])
    kv_s = _kv_rows(zb[..., gw + 2 * HEAD_DIM:gw + 4 * HEAD_DIM])
    kv_w = _kv_rows(zb[..., gw + 4 * HEAD_DIM:gw + 6 * HEAD_DIM])
    o = gw + 6 * HEAD_DIM
    gates = jax.nn.sigmoid(zb[..., o:o + 3 * N_HEADS]).reshape(b, t, N_HEADS, 3)
    past_c = c_nsa_cmp[page_table].reshape(b, n_past, 2, HEAD_DIM)
    kvc = _nsa_compress(jnp.concatenate([past_c, kv_c], axis=1), lp)
    kvw = jnp.concatenate([s_win, kv_w], axis=1)
    wpos = n_past - n_buf + jnp.arange(n_buf + t)
    yb = _nsa_attend_sample(q, gates, qpos, kvc, _paged_fetch(c_nsa_slc, page_table, kv_s), kvw, wpos,
                            sl_nsa, n_past + t)
    qc = zc[..., :gw].reshape(b, t, N_HEADS, HEAD_DIM)
    kv_d = _kv_rows(zc[..., gw:gw + 2 * HEAD_DIM])
    o = gw + 2 * HEAD_DIM
    qi = zc[..., o:o + IDX_HEADS * IDX_DIM].reshape(b, t, IDX_HEADS, IDX_DIM)
    o += IDX_HEADS * IDX_DIM
    ki = zc[..., o:o + IDX_DIM]
    wi = zc[..., o + IDX_DIM:o + IDX_DIM + IDX_HEADS]
    past_i = c_dsa_idx[page_table].reshape(b, n_past, IDX_DIM)
    ki_all = jnp.concatenate([past_i, ki], axis=1)
    yc = _dsa_attend_sample(qc, qi, wi, qpos, ki_all, _paged_fetch(c_dsa_kv, page_table, kv_d), sl_dsa, n_past + t)
    yd, conv_buf = _conv_mixer(zd, s_conv, lp['conv_w'])
    flat = lambda u: u.reshape(b, t, gw)
    return (ya, flat(yb), flat(yc), yd), (s_rwkv_new, shift, kv_c, kv_s, kvw[:, -n_buf:], kv_d, ki, conv_buf)


def _layer(x, mod, lp, pw, mixer_fn, alpha):
    b, t, d = x.shape
    sh1, sc1, g1, sh2, sc2, g2 = mod
    zs = _inproj(x, sc1, sh1, pw['w_in'])
    ys, states = mixer_fn(*zs)
    x1, h2, logits = _outproj(ys, x, g1, sc2, sh2, pw['w_out'], pw['ln1_g'], pw['ln1_b'],
                              pw['wr'], pw['br'], alpha)
    n_exp = lp['moe_wr'].shape[1]
    blk = _moe_block_rows(b * t * TOP_K, n_exp)
    y = _moe(h2.reshape(b * t, d), logits.reshape(b * t, LANES)[:, :n_exp], pw['moe'], blk)
    x2 = _resln(x1, y.reshape(b, t, d), g2, pw['ln2_g'], pw['ln2_b'], alpha)
    return x2, states


def _prep_weights(lp):
    d = lp['w_in'].shape[0]
    cols, o = [], 0
    for c, p in ((A_COLS, A_PAD), (B_COLS, B_PAD), (C_COLS, C_PAD), (D_COLS, D_PAD)):
        cols.append(jnp.pad(lp['w_in'][:, o:o + c], ((0, 0), (0, p - c))))
        o += c
    n_exp = lp['moe_wr'].shape[1]
    row = lambda u: u.reshape(1, -1)
    return {
        'w_in': jnp.concatenate(cols, axis=1).astype(BF16),
        'w_out': lp['w_out'].reshape(4, GROUP_WIDTH, d).astype(BF16),
        'ln1_g': row(lp['ln1_g']), 'ln1_b': row(lp['ln1_b']),
        'ln2_g': row(lp['ln2_g']), 'ln2_b': row(lp['ln2_b']),
        'wr': jnp.pad(lp['moe_wr'], ((0, 0), (0, LANES - n_exp))),
        'br': jnp.pad(lp['moe_br'], (0, LANES - n_exp)).reshape(1, LANES),
        'moe': (lp['moe_wgu'].astype(BF16), lp['moe_bgu'], lp['moe_wd'].astype(BF16), lp['moe_bd']),
    }


def kernel(x_prompt, x_sample, state_rwkv, state_rwkv_shift, cache_nsa_cmp, cache_nsa_slc, state_nsa_win, cache_dsa_kv, cache_dsa_idx, state_conv, page_table, c_prompt, c_sample, w_ada, b_ada, w_in, w_out, ln1_g, ln1_b, ln2_g, ln2_b, rwkv_mu, rwkv_w0, rwkv_w2, rwkv_a0, rwkv_a2, rwkv_g2, rwkv_kk, rwkv_ka, rwkv_rk, rwkv_gn_g, rwkv_gn_b, nsa_pe, nsa_w1, nsa_w2, conv_w, moe_wr, moe_br, moe_wgu, moe_bgu, moe_wd, moe_bd):
    depth = w_in.shape[0]
    alpha = (2 * depth) ** 0.25
    sl_nsa, sl_dsa = _alibi_slopes()
    bp = x_prompt.shape[0]
    hp, hs = x_prompt, x_sample
    c_all = jnp.concatenate([c_prompt, c_sample], axis=0)
    states_p, states_s = [], []
    for l in range(depth):
        lp = {
            'w_in': w_in[l], 'w_out': w_out[l], 'ln1_g': ln1_g[l], 'ln1_b': ln1_b[l],
            'ln2_g': ln2_g[l], 'ln2_b': ln2_b[l],
            'rwkv_mu': jnp.pad(rwkv_mu[l], (0, A_PAD - A_COLS)), 'rwkv_w0': rwkv_w0[l], 'rwkv_w2': rwkv_w2[l],
            'rwkv_a0': rwkv_a0[l], 'rwkv_a2': rwkv_a2[l], 'rwkv_g2': rwkv_g2[l],
            'rwkv_kk': rwkv_kk[l], 'rwkv_ka': rwkv_ka[l], 'rwkv_rk': rwkv_rk[l],
            'rwkv_gn_g': rwkv_gn_g[l], 'rwkv_gn_b': rwkv_gn_b[l],
            'nsa_pe': nsa_pe[l], 'nsa_w1': nsa_w1[l], 'nsa_w2': nsa_w2[l], 'conv_w': conv_w[l],
            'moe_wr': moe_wr[l], 'moe_br': moe_br[l], 'moe_wgu': moe_wgu[l], 'moe_bgu': moe_bgu[l],
            'moe_wd': moe_wd[l], 'moe_bd': moe_bd[l],
        }
        pw = _prep_weights(lp)
        mod_all = jax.nn.silu(c_all) @ w_ada[l] + b_ada[l]
        mod_p = [u[:, None, :] for u in jnp.split(mod_all[:bp], 6, axis=-1)]
        mod_s = [u[:, None, :] for u in jnp.split(mod_all[bp:], 6, axis=-1)]
        mix_p = functools.partial(_mixers_prompt, lp=lp, sl_nsa=sl_nsa, sl_dsa=sl_dsa)
        mix_s = functools.partial(
            _mixers_sample, lp=lp, sl_nsa=sl_nsa, sl_dsa=sl_dsa, s_rwkv=state_rwkv[l],
            s_shift=state_rwkv_shift[l], c_nsa_cmp=cache_nsa_cmp[l], c_nsa_slc=cache_nsa_slc[l],
            s_win=state_nsa_win[l], c_dsa_kv=cache_dsa_kv[l], c_dsa_idx=cache_dsa_idx[l],
            s_conv=state_conv[l], page_table=page_table)
        hp, st_p = _layer(hp, mod_p, lp, pw, mix_p, alpha)
        hs, st_s = _layer(hs, mod_s, lp, pw, mix_s, alpha)
        states_p.append(st_p)
        states_s.append(st_s)
    outs_p = [jnp.stack(v) for v in zip(*states_p)]
    outs_s = [jnp.stack(v) for v in zip(*states_s)]
    return (hp, hs, *outs_p, *outs_s)
```

```python
import functools

import jax
import jax.numpy as jnp
import numpy as np
from jax import lax
from jax.experimental import pallas as pl
from jax.experimental.pallas import tpu as pltpu

F32 = jnp.float32
BF16 = jnp.bfloat16

HEAD_DIM = 64
N_HEADS = 4
GROUP_WIDTH = N_HEADS * HEAD_DIM
RWKV_DECAY_SCALE = 0.606531
RWKV_GN_EPS = 64e-5
NSA_CMP_BLOCK = 32
NSA_SEL_BLOCK = 64
NSA_TOPN = 16
NSA_LOCAL_BLOCKS = 2
NSA_WINDOW = 512
DSA_TOPK = 256
IDX_HEADS = 4
IDX_DIM = 32
CONV_K = 3
TOP_K = 4
SWIGLU_LIMIT = 7.0
SWIGLU_ALPHA = 1.702
NEG_INF = -1e30
FORCE_SCORE = 1e4
Q_BLOCK = 128
PAGE_SIZE = 128

LANES = 128
A_COLS = 3 * GROUP_WIDTH + 32 + 32 + 64
B_COLS = GROUP_WIDTH + 6 * HEAD_DIM + 3 * N_HEADS
C_COLS = GROUP_WIDTH + 2 * HEAD_DIM + IDX_HEADS * IDX_DIM + IDX_DIM + IDX_HEADS
D_COLS = 3 * GROUP_WIDTH


def _pad_to(n, m):
    return -(-n // m) * m


A_PAD, B_PAD, C_PAD, D_PAD = (_pad_to(c, LANES) for c in (A_COLS, B_COLS, C_COLS, D_COLS))
VMEM_LIMIT = 56 * 1024 * 1024

_NT = (((1,), (1,)), ((), ()))


def _alibi_slopes():
    n = 2 * N_HEADS
    s = [2.0 ** (-8.0 * i / n) for i in range(1, n + 1)]
    return s[0::2], s[1::2]


def _ln_rows(x, eps=1e-5):
    mu = jnp.mean(x, axis=-1, keepdims=True)
    xc = x - mu
    var = jnp.mean(xc * xc, axis=-1, keepdims=True)
    return xc * lax.rsqrt(var + eps)


def _cparams(sem):
    return pltpu.CompilerParams(dimension_semantics=sem, vmem_limit_bytes=VMEM_LIMIT)


def _inproj_kernel(x_ref, sc_ref, sh_ref, w_ref, za_ref, zb_ref, zc_ref, zd_ref):
    h = _ln_rows(x_ref[0]) * (1.0 + sc_ref[0]) + sh_ref[0]
    z = jnp.dot(h.astype(BF16), w_ref[...], preferred_element_type=F32)
    o = 0
    for ref, w in ((za_ref, A_PAD), (zb_ref, B_PAD), (zc_ref, C_PAD), (zd_ref, D_PAD)):
        ref[0] = z[:, o:o + w]
        o += w


def _inproj(x, sc, sh, w_pad):
    b, t, d = x.shape
    tm = min(t, 512)
    mod_rows = sc.shape[1]
    mod_blk = (1, tm, d) if mod_rows == t and t > 1 else (1, 1, d)
    mod_map = (lambda i, j: (i, j, 0)) if mod_rows == t and t > 1 else (lambda i, j: (i, 0, 0))
    widths = (A_PAD, B_PAD, C_PAD, D_PAD)
    return pl.pallas_call(
        _inproj_kernel,
        grid=(b, t // tm),
        in_specs=[pl.BlockSpec((1, tm, d), lambda i, j: (i, j, 0)),
                  pl.BlockSpec(mod_blk, mod_map),
                  pl.BlockSpec(mod_blk, mod_map),
                  pl.BlockSpec(w_pad.shape, lambda i, j: (0, 0))],
        out_specs=[pl.BlockSpec((1, tm, w), lambda i, j: (i, j, 0)) for w in widths],
        out_shape=[jax.ShapeDtypeStruct((b, t, w), F32) for w in widths],
        compiler_params=_cparams(("parallel", "parallel")),
        name="inproj",
    )(x, sc, sh, w_pad)


def _rwkv_scan_kernel(r_ref, w_ref, k_ref, v_ref, kk_ref, kka_ref, s0_ref, y_ref, sfin_ref,
                      s_scr, y_scr, *, bg, tc, unroll):
    c = pl.program_id(1)
    rows = bg * HEAD_DIM

    @pl.when(c == 0)
    def _():
        s_scr[...] = s0_ref[...].reshape(rows, GROUP_WIDTH)

    li = lax.broadcasted_iota(jnp.int32, (2 * GROUP_WIDTH, GROUP_WIDTH), 0)
    lj = lax.broadcasted_iota(jnp.int32, (2 * GROUP_WIDTH, GROUP_WIDTH), 1)
    seg_ones2 = jnp.where((li % GROUP_WIDTH) // HEAD_DIM == lj // HEAD_DIM, 1.0, 0.0).astype(BF16)
    seg_ones1 = seg_ones2[:GROUP_WIDTH]
    lane = lax.broadcasted_iota(jnp.int32, (rows, GROUP_WIDTH), 1)
    row = lax.broadcasted_iota(jnp.int32, (rows, GROUP_WIDTH), 0)
    diag = jnp.where(row % HEAD_DIM == lane % HEAD_DIM, 1.0, 0.0)

    def seg_sum(x):
        hi = x.astype(BF16)
        lo = (x - hi.astype(F32)).astype(BF16)
        return jnp.dot(jnp.concatenate([hi, lo], axis=1), seg_ones2, preferred_element_type=F32)

    def bcast_rows(ref, t):
        return jnp.concatenate(
            [jnp.broadcast_to(ref[b, pl.ds(t, 1), :], (HEAD_DIM, GROUP_WIDTH)) for b in range(bg)], axis=0)

    def step(t):
        s = s_scr[...]
        kk = bcast_rows(kk_ref, t)
        sa = seg_sum(s * kk)
        vcol = seg_sum(bcast_rows(v_ref, t) * diag)
        s = s * bcast_rows(w_ref, t) - sa * bcast_rows(kka_ref, t) + vcol * bcast_rows(k_ref, t)
        s_scr[...] = s
        y = jnp.dot((s * bcast_rows(r_ref, t)).astype(BF16), seg_ones1, preferred_element_type=F32)
        y_scr[...] = jnp.where(lane % HEAD_DIM == t, y, y_scr[...])

    def outer(i, carry):
        for u in range(unroll):
            step(i * unroll + u)
        return carry

    lax.fori_loop(0, tc // unroll, outer, 0)
    y_ref[...] = y_scr[...].reshape(bg, 1, HEAD_DIM, GROUP_WIDTH)

    @pl.when(c == pl.num_programs(1) - 1)
    def _():
        sfin_ref[...] = s_scr[...].reshape(bg, HEAD_DIM, GROUP_WIDTH)


def _rwkv_scan(r, w, k, v, kk, kka, s0):
    b, t, gw = r.shape
    bg = 8 if b % 8 == 0 else b
    tc = min(t, HEAD_DIM)
    unroll = 4 if tc % 4 == 0 else 1
    nc = t // tc
    s0l = s0.transpose(0, 2, 1, 3).reshape(b, HEAD_DIM, gw)
    seq = pl.BlockSpec((bg, tc, gw), lambda i, c: (i, c, 0))
    st = pl.BlockSpec((bg, HEAD_DIM, gw), lambda i, c: (i, 0, 0))
    y_raw, s_fin = pl.pallas_call(
        functools.partial(_rwkv_scan_kernel, bg=bg, tc=tc, unroll=unroll),
        grid=(b // bg, nc),
        in_specs=[seq] * 6 + [st],
        out_specs=[pl.BlockSpec((bg, 1, HEAD_DIM, gw), lambda i, c: (i, c, 0, 0)), st],
        out_shape=[jax.ShapeDtypeStruct((b, nc, HEAD_DIM, gw), F32),
                   jax.ShapeDtypeStruct((b, HEAD_DIM, gw), F32)],
        scratch_shapes=[pltpu.VMEM((bg * HEAD_DIM, gw), F32), pltpu.VMEM((bg * HEAD_DIM, gw), F32)],
        compiler_params=_cparams(("parallel", "arbitrary")),
        name="rwkv_scan",
    )(r, w, k, v, kk, kka, s0l)
    y = y_raw.reshape(b, nc, HEAD_DIM, N_HEADS, HEAD_DIM)[..., :tc]
    y = y.transpose(0, 1, 4, 3, 2).reshape(b, t, N_HEADS, HEAD_DIM)
    s_fin = s_fin.reshape(b, HEAD_DIM, N_HEADS, HEAD_DIM).transpose(0, 2, 1, 3)
    return y, s_fin


def _rwkv_mixer(za, shift_prev, s0, lp):
    b, t, _ = za.shape
    z_prev = jnp.concatenate([shift_prev[:, None], za[:, :-1]], axis=1)
    zs = za + lp['rwkv_mu'] * (z_prev - za)
    gw = GROUP_WIDTH
    r, k, v = zs[..., :gw], zs[..., gw:2 * gw], zs[..., 2 * gw:3 * gw]
    o = 3 * gw
    w_lo, a_lo, g_lo = zs[..., o:o + 32], zs[..., o + 32:o + 64], zs[..., o + 64:o + 128]
    decay = jnp.exp(-RWKV_DECAY_SCALE * jax.nn.sigmoid(lp['rwkv_w0'] + jnp.tanh(w_lo) @ lp['rwkv_w2']))
    a = jax.nn.sigmoid(lp['rwkv_a0'] + a_lo @ lp['rwkv_a2'])
    g = jax.nn.sigmoid(g_lo) @ lp['rwkv_g2']
    heads = lambda u: u.reshape(b, t, N_HEADS, HEAD_DIM)
    kk = heads(k) * lp['rwkv_kk']
    kk = kk / jnp.maximum(jnp.sqrt(jnp.sum(kk * kk, axis=-1, keepdims=True)), 1e-12)
    kk = kk.reshape(b, t, gw)
    k = k * (1.0 + (a - 1.0) * lp['rwkv_ka'].reshape(gw))
    y, s_fin = _rwkv_scan(r, decay, k, v, kk, kk * a, s0)
    mu = jnp.mean(y, axis=-1, keepdims=True)
    var = jnp.mean(jnp.square(y - mu), axis=-1, keepdims=True)
    yn = ((y - mu) * lax.rsqrt(var + RWKV_GN_EPS) * lp['rwkv_gn_g'].reshape(N_HEADS, HEAD_DIM)
          + lp['rwkv_gn_b'].reshape(N_HEADS, HEAD_DIM))
    bonus = jnp.sum(heads(r) * heads(k) * lp['rwkv_rk'], axis=-1, keepdims=True) * heads(v)
    out = (yn + bonus).reshape(b, t, gw) * g
    return out, s_fin, za[:, -1, :A_COLS]


KEY_TILE = 512
POS_SPLIT = 64
M_FLOOR = -1e29


def _aug_keys(kv, pos):
    lane = lax.broadcasted_iota(jnp.int32, kv.shape, 1)
    hi = (pos // POS_SPLIT).astype(F32)
    lo = (pos % POS_SPLIT).astype(F32)
    extra = jnp.where(lane == 0, hi, jnp.where(lane == 1, lo, 0.0))
    return jnp.concatenate([kv.astype(BF16), extra.astype(BF16)], axis=1)


def _aug_queries(q2, slopes, lane):
    rows = []
    for h in range(N_HEADS):
        pair = q2[:, (h // 2) * LANES:(h // 2 + 1) * LANES]
        if h % 2:
            pair = pltpu.roll(pair, HEAD_DIM, 1)
        extra = jnp.where(lane == 0, POS_SPLIT * slopes[h], jnp.where(lane == 1, slopes[h], 0.0))
        rows.append(jnp.concatenate([jnp.where(lane < HEAD_DIM, pair, 0.0), extra], axis=1))
    return jnp.concatenate(rows, axis=0).astype(BF16)


def _tile_heads(x):
    return jnp.concatenate([x] * N_HEADS, axis=0)


def _softmax_block(qa, ka, madd):
    s = lax.dot_general(qa, ka, _NT, preferred_element_type=F32) + _tile_heads(madd)
    m = jnp.maximum(jnp.max(s, axis=1, keepdims=True), M_FLOOR)
    e = jnp.exp(s - m)
    inv = 1.0 / jnp.maximum(jnp.sum(e, axis=1, keepdims=True), 1e-30)
    o = jnp.dot(e.astype(BF16), ka, preferred_element_type=F32) * inv
    return e, inv, o


def _flash_step(qa, ka, madd, m, l, acc):
    s = lax.dot_general(qa, ka, _NT, preferred_element_type=F32) + _tile_heads(madd)
    m_new = jnp.maximum(m, jnp.max(s, axis=1, keepdims=True))
    alpha = jnp.exp(m - m_new)
    p = jnp.exp(s - m_new)
    l = alpha * l + jnp.sum(p, axis=1, keepdims=True)
    acc = alpha * acc + jnp.dot(p.astype(BF16), ka, preferred_element_type=F32)
    return m_new, l, acc


def _flash_init():
    rows = N_HEADS * Q_BLOCK
    return (jnp.full((rows, 1), M_FLOOR, F32), jnp.zeros((rows, 1), F32), jnp.zeros((rows, 2 * LANES), F32))


def _heads_out(per_head, lane):
    pair = lambda a, b: jnp.where(lane < HEAD_DIM, pltpu.roll(a, HEAD_DIM, 1), b)
    return jnp.concatenate([pair(per_head[0], per_head[1]), pair(per_head[2], per_head[3])], axis=1)


def _head_rows(o, h):
    return o[h * Q_BLOCK:(h + 1) * Q_BLOCK, :LANES]


def _nsa_kernel(q_ref, ksv_ref, kwv_ref, gl_ref, kvc_ref, exp_ref, o_ref, ks_scr, kw_scr,
                *, t, n_win, n_top, slopes):
    qb = Q_BLOCK
    j = pl.program_id(1)
    q0 = j * qb

    @pl.when(j == 0)
    def _():
        for c in range(t // KEY_TILE):
            rows = slice(c * KEY_TILE, (c + 1) * KEY_TILE)
            pos = c * KEY_TILE + lax.broadcasted_iota(jnp.int32, (KEY_TILE, 1), 0)
            ks_scr[rows, :] = _aug_keys(ksv_ref[0, rows, :], pos)
            kw_scr[rows, :] = _aug_keys(kwv_ref[0, rows, :], pos)

    lane = lax.broadcasted_iota(jnp.int32, (qb, LANES), 1)
    qpos = q0 + lax.broadcasted_iota(jnp.int32, (qb, 1), 0)
    qa = _aug_queries(q_ref[0] * (HEAD_DIM ** -0.5), slopes, lane)
    gates = jax.nn.sigmoid(gl_ref[0])

    perm_end = lambda i: (2 * (i % HEAD_DIM) + i // HEAD_DIM) * NSA_CMP_BLOCK + (NSA_CMP_BLOCK - 1)
    kca = _aug_keys(kvc_ref[0], perm_end(lax.broadcasted_iota(jnp.int32, (LANES, 1), 0)))
    cend = perm_end(lax.broadcasted_iota(jnp.int32, (1, LANES), 1))
    e, inv, o_c = _softmax_block(qa, kca, jnp.where(cend <= qpos, 0.0, NEG_INF))
    p_c = e * inv
    imp = _head_rows(p_c, 0)
    for h in range(1, N_HEADS):
        imp = imp + _head_rows(p_c, h)
    imp = imp + pltpu.roll(imp, HEAD_DIM, 1)

    cur = qpos // NSA_SEL_BLOCK
    forced = (lane == 0) | (lane > cur - NSA_LOCAL_BLOCKS)
    score = jnp.where(lane > cur, NEG_INF, jnp.where(forced, FORCE_SCORE, imp))
    score = jnp.where(lane < HEAD_DIM, score, -3e38)
    rank = jnp.zeros((qb, LANES), F32)
    for i in range(NSA_SEL_BLOCK):
        col = score[:, i:i + 1]
        beats = (col > score) | ((col == score) & (lane > i))
        rank = rank + jnp.where(beats, 1.0, 0.0)
    sel = jnp.where((rank < n_top) & (lane < HEAD_DIM), 1.0, 0.0).astype(BF16)

    def sel_tile(kt, carry):
        k0 = pl.multiple_of(kt * KEY_TILE, KEY_TILE)
        kpos = k0 + lax.broadcasted_iota(jnp.int32, (1, KEY_TILE), 1)
        member = jnp.dot(sel, exp_ref[kt], preferred_element_type=F32) > 0.5
        madd = jnp.where(member & (kpos <= qpos), 0.0, NEG_INF)
        return _flash_step(qa, ks_scr[pl.ds(k0, KEY_TILE), :], madd, *carry)

    n_tiles = (q0 + qb - 1) // KEY_TILE + 1
    _, l_s, acc_s = lax.fori_loop(0, n_tiles, sel_tile, _flash_init())
    o_s = acc_s * (1.0 / jnp.maximum(l_s, 1e-30))

    kst = pl.multiple_of(jnp.maximum(q0 + qb - n_win, 0), qb)
    dw = qpos - (kst + lax.broadcasted_iota(jnp.int32, (1, n_win), 1))
    wadd = jnp.where((dw >= 0) & (dw <= NSA_WINDOW), 0.0, NEG_INF)
    _, _, o_w = _softmax_block(qa, kw_scr[pl.ds(kst, n_win), :], wadd)

    outs = []
    for h in range(N_HEADS):
        g = [gates[:, 3 * h + i:3 * h + i + 1] for i in range(3)]
        outs.append(g[0] * _head_rows(o_c, h) + g[1] * _head_rows(o_s, h) + g[2] * _head_rows(o_w, h))
    o_ref[0] = _heads_out(outs, lane)


def _nsa_prompt(zb, kvc_perm, slopes):
    b, t, _ = zb.shape
    assert t % KEY_TILE == 0 and t // NSA_CMP_BLOCK <= LANES and t <= 256 * POS_SPLIT
    n_win = min(NSA_WINDOW + Q_BLOCK, t)
    n_sel = -(-t // NSA_SEL_BLOCK)
    n_kt = t // KEY_TILE
    key_blk = (np.arange(t) // NSA_SEL_BLOCK).reshape(n_kt, 1, KEY_TILE)
    expand = jnp.asarray((np.arange(LANES)[None, :, None] == key_blk).astype(np.float32), BF16)
    full = lambda c: pl.BlockSpec((1, t, LANES), lambda i, j, c=c: (i, 0, c))
    return pl.pallas_call(
        functools.partial(_nsa_kernel, t=t, n_win=n_win, n_top=min(NSA_TOPN, n_sel), slopes=slopes),
        grid=(b, t // Q_BLOCK),
        in_specs=[pl.BlockSpec((1, Q_BLOCK, 2 * LANES), lambda i, j: (i, j, 0)),
                  full(3), full(4),
                  pl.BlockSpec((1, Q_BLOCK, LANES), lambda i, j: (i, j, 5)),
                  pl.BlockSpec((1, LANES, LANES), lambda i, j: (i, 0, 0)),
                  pl.BlockSpec((n_kt, LANES, KEY_TILE), lambda i, j: (0, 0, 0))],
        out_specs=pl.BlockSpec((1, Q_BLOCK, GROUP_WIDTH), lambda i, j: (i, j, 0)),
        out_shape=jax.ShapeDtypeStruct((b, t, GROUP_WIDTH), F32),
        scratch_shapes=[pltpu.VMEM((t, 2 * LANES), BF16), pltpu.VMEM((t, 2 * LANES), BF16)],
        compiler_params=_cparams(("parallel", "arbitrary")),
        name="nsa_prompt",
    )(zb, zb, zb, zb, kvc_perm, expand)


def _nsa_compress(rows, lp):
    b, l = rows.shape[:2]
    nc = l // NSA_CMP_BLOCK
    blk = rows[:, :nc * NSA_CMP_BLOCK].reshape(b, nc, NSA_CMP_BLOCK, 2, HEAD_DIM) + lp['nsa_pe']
    flat = jnp.swapaxes(blk, 2, 3).reshape(b, nc, 2, NSA_CMP_BLOCK * HEAD_DIM)
    hid = jax.nn.gelu(jnp.einsum('bnci,cih->bnch', flat, lp['nsa_w1']))
    return jnp.einsum('bnch,chd->bncd', hid, lp['nsa_w2'])


def _dsa_kernel(q_ref, kv_ref, qi_ref, kiw_ref, wiq_ref, o_ref, ka_scr, kip_scr, skey_scr, *, t, k_sel, slopes):
    qb = Q_BLOCK
    j = pl.program_id(1)
    q0 = j * qb
    n_sub = KEY_TILE // LANES

    @pl.when(j == 0)
    def _():
        for c in range(t // KEY_TILE):
            rows = slice(c * KEY_TILE, (c + 1) * KEY_TILE)
            pos = c * KEY_TILE + lax.broadcasted_iota(jnp.int32, (KEY_TILE, 1), 0)
            ka_scr[rows, :] = _aug_keys(kv_ref[0, rows, :], pos)
            lk = lax.broadcasted_iota(jnp.int32, (KEY_TILE, LANES), 1)
            ki = jnp.where(lk < IDX_DIM, kiw_ref[0, rows, :], 0.0)
            hi = ki.astype(BF16).astype(F32)
            packed = hi + pltpu.roll(ki - hi, IDX_DIM, 1) + pltpu.roll(hi, 2 * IDX_DIM, 1)
            kip_scr[rows, :] = packed.astype(BF16)

    lane = lax.broadcasted_iota(jnp.int32, (qb, LANES), 1)
    qpos = q0 + lax.broadcasted_iota(jnp.int32, (qb, 1), 0)
    n_tiles = (q0 + qb - 1) // KEY_TILE + 1
    tile_pos = lambda kt: kt * KEY_TILE + lax.broadcasted_iota(jnp.int32, (1, KEY_TILE), 1)

    qi = qi_ref[0]
    wq = wiq_ref[0]
    packed_q = []
    for h in range(IDX_HEADS):
        qh = qi if h == 0 else pltpu.roll(qi, LANES - IDX_DIM * h, 1)
        qh = jnp.where(lane < IDX_DIM, qh, 0.0)
        hi = qh.astype(BF16).astype(F32)
        packed_q.append(hi + pltpu.roll(hi, IDX_DIM, 1) + pltpu.roll(qh - hi, 2 * IDX_DIM, 1))
    packed_q = jnp.concatenate(packed_q, axis=0).astype(BF16)
    w_cols = [wq[:, IDX_DIM + h:IDX_DIM + h + 1] for h in range(IDX_HEADS)]

    def idx_tile(kt, carry):
        k0 = pl.multiple_of(kt * KEY_TILE, KEY_TILE)
        rel = jnp.maximum(lax.dot_general(packed_q, kip_scr[pl.ds(k0, KEY_TILE), :], _NT,
                                          preferred_element_type=F32), 0.0)
        score = w_cols[0] * rel[:qb]
        for h in range(1, IDX_HEADS):
            score = score + w_cols[h] * rel[h * qb:(h + 1) * qb]
        score = jnp.where(score == 0.0, 0.0, score)
        score = jnp.where(tile_pos(kt) <= qpos, score, NEG_INF)
        bits = pltpu.bitcast(score, jnp.int32)
        skey_scr[kt] = bits ^ ((bits >> 31) & 0x7FFFFFFF)
        return carry

    lax.fori_loop(0, n_tiles, idx_tile, 0)

    kf = float(k_sel)

    def count(c, strict):
        cb = jnp.broadcast_to(c, (qb, LANES))

        def body(kt, acc):
            x = skey_scr[kt]
            for u in range(n_sub):
                xu = x[:, u * LANES:(u + 1) * LANES]
                acc = acc + jnp.where((xu > cb) if strict else (xu >= cb), 1.0, 0.0)
            return acc

        acc = lax.fori_loop(0, n_tiles, body, jnp.zeros((qb, LANES), F32))
        return jnp.sum(acc, axis=1, keepdims=True)

    int_min = jnp.int32(-2 ** 31)
    tau = jnp.where(count(jnp.zeros((qb, 1), jnp.int32), False) >= kf, jnp.int32(0), int_min)

    def bit_step(i, tau):
        cand = tau + lax.shift_left(jnp.int32(1), 30 - i)
        return jnp.where(count(cand, False) >= kf, cand, tau)

    tau = lax.fori_loop(0, 31, bit_step, tau)
    need = kf - count(tau, True)
    taub = jnp.broadcast_to(tau, (qb, LANES))
    ui = lax.broadcasted_iota(jnp.int32, (LANES, LANES), 0)
    uj = lax.broadcasted_iota(jnp.int32, (LANES, LANES), 1)
    upper = jnp.where(ui <= uj, 1.0, 0.0).astype(BF16)
    qa = _aug_queries(q_ref[0] * (HEAD_DIM ** -0.5), slopes, lane)

    def att_tile(kt, carry):
        ties, m, l, acc = carry
        k0 = pl.multiple_of(kt * KEY_TILE, KEY_TILE)
        x = skey_scr[kt]
        parts = []
        for u in range(n_sub):
            xu = x[:, u * LANES:(u + 1) * LANES]
            eq = xu == taub
            eqf = jnp.where(eq, 1.0, 0.0)
            before = jnp.dot(eqf.astype(BF16), upper, preferred_element_type=F32) + ties
            parts.append(jnp.where((xu > taub) | (eq & (before <= need)), 0.0, NEG_INF))
            ties = ties + jnp.sum(eqf, axis=1, keepdims=True)
        madd = jnp.where(tile_pos(kt) <= qpos, jnp.concatenate(parts, axis=1), NEG_INF)
        return (ties,) + _flash_step(qa, ka_scr[pl.ds(k0, KEY_TILE), :], madd, m, l, acc)

    _, _, l, acc = lax.fori_loop(0, n_tiles, att_tile, (jnp.zeros((qb, 1), F32),) + _flash_init())
    o = acc * (1.0 / jnp.maximum(l, 1e-30))
    o_ref[0] = _heads_out([_head_rows(o, h) for h in range(N_HEADS)], lane)


def _dsa_prompt(zc, slopes):
    b, t, _ = zc.shape
    assert t % KEY_TILE == 0 and t <= 256 * POS_SPLIT
    k_sel = min(DSA_TOPK, t // 4)
    assert k_sel <= KEY_TILE
    full = lambda c: pl.BlockSpec((1, t, LANES), lambda i, j, c=c: (i, 0, c))
    qblk = lambda c: pl.BlockSpec((1, Q_BLOCK, LANES), lambda i, j, c=c: (i, j, c))
    return pl.pallas_call(
        functools.partial(_dsa_kernel, t=t, k_sel=k_sel, slopes=slopes),
        grid=(b, t // Q_BLOCK),
        in_specs=[pl.BlockSpec((1, Q_BLOCK, 2 * LANES), lambda i, j: (i, j, 0)),
                  full(2), qblk(3), full(4), qblk(4)],
        out_specs=pl.BlockSpec((1, Q_BLOCK, GROUP_WIDTH), lambda i, j: (i, j, 0)),
        out_shape=jax.ShapeDtypeStruct((b, t, GROUP_WIDTH), F32),
        scratch_shapes=[pltpu.VMEM((t, 2 * LANES), BF16), pltpu.VMEM((t, LANES), BF16),
                        pltpu.VMEM((t // KEY_TILE, Q_BLOCK, KEY_TILE), jnp.int32)],
        compiler_params=_cparams(("parallel", "arbitrary")),
        name="dsa_prompt",
    )(zc, zc, zc, zc, zc)


def _outproj_kernel(ya_ref, yb_ref, yc_ref, yd_ref, x_ref, g1_ref, sc2_ref, sh2_ref, wo_ref, lg_ref, lb_ref,
                    wr_ref, br_ref, x1_ref, h2_ref, logit_ref, *, alpha):
    mix = jnp.zeros(x_ref.shape[1:], F32)
    for i, ref in enumerate((ya_ref, yb_ref, yc_ref, yd_ref)):
        mix = mix + jnp.dot(ref[0].astype(BF16), wo_ref[i], preferred_element_type=F32)
    x1 = _ln_rows(alpha * x_ref[0] + g1_ref[0] * mix) * lg_ref[...] + lb_ref[...]
    x1_ref[0] = x1
    h2 = _ln_rows(x1) * (1.0 + sc2_ref[0]) + sh2_ref[0]
    h2_ref[0] = h2.astype(BF16)
    logit_ref[0] = jnp.dot(h2, wr_ref[...], preferred_element_type=F32,
                           precision=lax.Precision.HIGHEST) + br_ref[...]


def _outproj(ys, x, g1, sc2, sh2, wo, ln_g, ln_b, wr_pad, br_pad, alpha):
    b, t, d = x.shape
    tm = min(t, 512)
    per_row = g1.shape[1] == t and t > 1
    mod_blk = (1, tm, d) if per_row else (1, 1, d)
    mod_map = (lambda i, j: (i, j, 0)) if per_row else (lambda i, j: (i, 0, 0))
    mod = pl.BlockSpec(mod_blk, mod_map)
    row = lambda w: pl.BlockSpec((1, tm, w), lambda i, j: (i, j, 0))
    const = lambda a: pl.BlockSpec(a.shape, lambda i, j: (0,) * a.ndim)
    return pl.pallas_call(
        functools.partial(_outproj_kernel, alpha=alpha),
        grid=(b, t // tm),
        in_specs=[row(GROUP_WIDTH)] * 4 + [row(d), mod, mod, mod, const(wo), const(ln_g), const(ln_b),
                                           const(wr_pad), const(br_pad)],
        out_specs=[row(d), row(d), row(LANES)],
        out_shape=[jax.ShapeDtypeStruct((b, t, d), F32), jax.ShapeDtypeStruct((b, t, d), BF16),
                   jax.ShapeDtypeStruct((b, t, LANES), F32)],
        compiler_params=_cparams(("parallel", "parallel")),
        name="outproj",
    )(*ys, x, g1, sc2, sh2, wo, ln_g, ln_b, wr_pad, br_pad)


def _moe_kernel(be_ref, nu_ref, x_ref, wgu_ref, bgu_ref, wd_ref, bd_ref, o_ref):
    i = pl.program_id(0)
    ff = wd_ref.shape[1]

    @pl.when(i < nu_ref[0])
    def _():
        gu = jnp.dot(x_ref[...], wgu_ref[0], preferred_element_type=F32) + bgu_ref[0]
        gt = jnp.minimum(gu[:, :ff], SWIGLU_LIMIT)
        up = jnp.clip(gu[:, ff:], -SWIGLU_LIMIT, SWIGLU_LIMIT)
        act = (up + 1.0) * gt * jax.nn.sigmoid(SWIGLU_ALPHA * gt)
        o_ref[...] = jnp.dot(act.astype(BF16), wd_ref[0], preferred_element_type=F32) + bd_ref[0]

    @pl.when(i >= nu_ref[0])
    def _():
        o_ref[...] = jnp.zeros(o_ref.shape, F32)


def _moe_experts(xg, blk_exp, n_used, wgu, bgu, wd, bd, blk):
    cap, d = xg.shape
    e, _, ff2 = wgu.shape
    ff = ff2 // 2
    grid_spec = pltpu.PrefetchScalarGridSpec(
        num_scalar_prefetch=2,
        grid=(cap // blk,),
        in_specs=[pl.BlockSpec((blk, d), lambda i, be, nu: (i, 0)),
                  pl.BlockSpec((1, d, ff2), lambda i, be, nu: (be[i], 0, 0)),
                  pl.BlockSpec((1, 1, ff2), lambda i, be, nu: (be[i], 0, 0)),
                  pl.BlockSpec((1, ff, d), lambda i, be, nu: (be[i], 0, 0)),
                  pl.BlockSpec((1, 1, d), lambda i, be, nu: (be[i], 0, 0))],
        out_specs=pl.BlockSpec((blk, d), lambda i, be, nu: (i, 0)),
    )
    return pl.pallas_call(
        _moe_kernel,
        grid_spec=grid_spec,
        out_shape=jax.ShapeDtypeStruct((cap, d), F32),
        compiler_params=_cparams(("arbitrary",)),
        name="moe_experts",
    )(blk_exp, n_used, xg, wgu, bgu.reshape(e, 1, ff2), wd, bd.reshape(e, 1, d))


def _slot_tokens_kernel(pos_ref, out_ref, *, chunk, shift):
    c = pl.program_id(0)

    @pl.when(c == 0)
    def _():
        def zero(i, carry):
            out_ref[i] = jnp.int32(0)
            return carry
        lax.fori_loop(0, out_ref.shape[0], zero, 0, unroll=8)

    base = c * chunk

    def body(i, carry):
        out_ref[pos_ref[i]] = lax.shift_right_logical(base + i, shift)
        return carry

    lax.fori_loop(0, chunk, body, 0, unroll=8)


def _slot_tokens(pos, cap):
    m = pos.shape[0]
    assert TOP_K & (TOP_K - 1) == 0
    chunk = min(m, 8192)
    assert m % chunk == 0
    return pl.pallas_call(
        functools.partial(_slot_tokens_kernel, chunk=chunk, shift=TOP_K.bit_length() - 1),
        grid=(m // chunk,),
        in_specs=[pl.BlockSpec((chunk,), lambda c: (c,), memory_space=pltpu.SMEM)],
        out_specs=pl.BlockSpec((cap,), lambda c: (0,), memory_space=pltpu.SMEM),
        out_shape=jax.ShapeDtypeStruct((cap,), jnp.int32),
        compiler_params=pltpu.CompilerParams(dimension_semantics=("arbitrary",)),
        name="slot_tokens",
    )(pos)


def _moe(h2, logits, mw, blk):
    n, d = h2.shape
    n_exp = logits.shape[1]
    top_v, top_e = lax.top_k(logits, TOP_K)
    gate = jax.nn.softmax(top_v, axis=-1)
    m = n * TOP_K
    cap = -(-(m + n_exp * (blk - 1)) // blk) * blk
    e_flat = top_e.reshape(m)
    onehot = (e_flat[:, None] == jnp.arange(n_exp)[None, :]).astype(jnp.int32)
    csum = jnp.cumsum(onehot, axis=0)
    counts = csum[-1]
    rank = jnp.take_along_axis(csum, e_flat[:, None], axis=1)[:, 0] - 1
    padded = (counts + blk - 1) // blk * blk
    pend = jnp.cumsum(padded)
    pstart = pend - padded
    pos = (pstart[e_flat] + rank).astype(jnp.int32)
    nblk = cap // blk
    blk_exp = jnp.minimum(jnp.searchsorted(pend, jnp.arange(nblk) * blk, side='right'), n_exp - 1).astype(jnp.int32)
    row_tok = _slot_tokens(pos, cap)
    n_used = (pend[-1] // blk).astype(jnp.int32).reshape(1)
    out = _moe_experts(jnp.take(h2, row_tok, axis=0), blk_exp, n_used, *mw, blk)
    picked = jnp.take(out, pos, axis=0).reshape(n, TOP_K, d)
    return jnp.sum(gate[..., None] * picked, axis=1)


def _moe_block_rows(n_assign, n_exp):
    per_expert = max(1, n_assign // n_exp)
    return min(256, max(16, 1 << (per_expert.bit_length() - 1)))


def _resln_kernel(x_ref, y_ref, g_ref, lg_ref, lb_ref, o_ref, *, alpha):
    o_ref[0] = _ln_rows(alpha * x_ref[0] + g_ref[0] * y_ref[0]) * lg_ref[...] + lb_ref[...]


def _resln(x, y, g, ln_g, ln_b, alpha):
    b, t, d = x.shape
    tm = min(t, 512)
    per_row = g.shape[1] == t and t > 1
    mod = pl.BlockSpec((1, tm, d) if per_row else (1, 1, d),
                       (lambda i, j: (i, j, 0)) if per_row else (lambda i, j: (i, 0, 0)))
    row = pl.BlockSpec((1, tm, d), lambda i, j: (i, j, 0))
    const = pl.BlockSpec((1, d), lambda i, j: (0, 0))
    return pl.pallas_call(
        functools.partial(_resln_kernel, alpha=alpha),
        grid=(b, t // tm),
        in_specs=[row, row, mod, const, const],
        out_specs=row,
        out_shape=jax.ShapeDtypeStruct((b, t, d), F32),
        compiler_params=_cparams(("parallel", "parallel")),
        name="resln",
    )(x, y, g, ln_g, ln_b)


def _conv_mixer(zd, buf, conv_w):
    t = zd.shape[1]
    ch = GROUP_WIDTH
    b_gate, c_gate, xin = zd[..., :ch], zd[..., ch:2 * ch], zd[..., 2 * ch:3 * ch]
    ue = jnp.concatenate([buf, c_gate * xin], axis=1)
    y = sum(ue[:, j:j + t] * conv_w[j] for j in range(CONV_K))
    return b_gate * y, ue[:, t:]


def _masked_softmax(s, valid):
    p = jax.nn.softmax(jnp.where(valid, s, NEG_INF), axis=-1)
    return jnp.where(valid, p, 0.0)


def _paged_fetch(pool, page_table, new_rows):
    n_past = page_table.shape[1] * PAGE_SIZE
    n_new = new_rows.shape[1]

    def fetch(pos):
        b = pos.shape[0]
        flat = pos.reshape(b, -1)
        bi = jnp.arange(b)[:, None]
        pc = jnp.clip(flat, 0, n_past - 1)
        past = pool[page_table[bi, pc // PAGE_SIZE], pc % PAGE_SIZE]
        new = new_rows[bi, jnp.clip(flat - n_past, 0, n_new - 1)]
        is_past = (flat < n_past).reshape(flat.shape + (1,) * (past.ndim - 2))
        return jnp.where(is_past, past, new).reshape(pos.shape + past.shape[2:])
    return fetch


def _nsa_attend_sample(q, gates, qpos, kvc, fetch_slc, kvw, wpos, slopes, n_keys):
    b, tq = q.shape[:2]
    q = q * HEAD_DIM ** -0.5
    sl = jnp.asarray(slopes, F32)[:, None, None]
    nc = kvc.shape[1]
    cend = jnp.arange(nc) * NSA_CMP_BLOCK + (NSA_CMP_BLOCK - 1)
    s_c = jnp.einsum('bqhd,bnd->bhqn', q, kvc[:, :, 0]) - sl * (qpos[:, None] - cend[None, :]).astype(F32)
    p_c = _masked_softmax(s_c, cend[None, :] <= qpos[:, None])
    o_c = jnp.einsum('bhqn,bnd->bqhd', p_c, kvc[:, :, 1])
    ratio = NSA_SEL_BLOCK // NSA_CMP_BLOCK
    n_sel = -(-n_keys // NSA_SEL_BLOCK)
    imp = jnp.pad(p_c.sum(1), ((0, 0), (0, 0), (0, n_sel * ratio - nc))).reshape(b, tq, n_sel, ratio).sum(-1)
    blk = jnp.arange(n_sel)[None, :]
    cur = (qpos // NSA_SEL_BLOCK)[:, None]
    forced = (blk == 0) | (blk > cur - NSA_LOCAL_BLOCKS)
    score = jnp.where(blk > cur, NEG_INF, jnp.where(forced, FORCE_SCORE, imp))
    n_top = min(NSA_TOPN, n_sel)
    _, sel = lax.top_k(score, n_top)
    kpos = (sel[..., None] * NSA_SEL_BLOCK + jnp.arange(NSA_SEL_BLOCK)).reshape(b, tq, n_top * NSA_SEL_BLOCK)
    kv_s = fetch_slc(kpos)
    s_s = (jnp.einsum('bqhd,bqnd->bhqn', q, kv_s[..., 0, :])
           - sl * (qpos[None, None, :, None] - kpos[:, None]).astype(F32))
    p_s = _masked_softmax(s_s, (kpos <= qpos[None, :, None])[:, None])
    o_s = jnp.einsum('bhqn,bqnd->bqhd', p_s, kv_s[..., 1, :])
    dw = qpos[:, None] - wpos[None, :]
    s_w = jnp.einsum('bqhd,bnd->bhqn', q, kvw[:, :, 0]) - sl * dw.astype(F32)
    p_w = _masked_softmax(s_w, (dw >= 0) & (dw <= NSA_WINDOW) & (wpos[None, :] >= 0))
    o_w = jnp.einsum('bhqn,bnd->bqhd', p_w, kvw[:, :, 1])
    return gates[..., 0:1] * o_c + gates[..., 1:2] * o_s + gates[..., 2:3] * o_w


def _dsa_attend_sample(q, qi, wi, qpos, kidx, fetch_kv, slopes, n_keys):
    l = kidx.shape[1]
    sl = jnp.asarray(slopes, F32)[:, None, None]
    rel = jax.nn.relu(jnp.einsum('bqhi,bsi->bqhs', qi, kidx, precision=lax.Precision.HIGHEST))
    score = jnp.sum(wi[..., None] * rel, axis=2)
    score = jnp.where(jnp.arange(l)[None, None, :] <= qpos[None, :, None], score, NEG_INF)
    k_sel = min(DSA_TOPK, n_keys // 4)
    _, sel = lax.top_k(score, k_sel)
    kv = fetch_kv(sel)
    s = (jnp.einsum('bqhd,bqnd->bhqn', q * HEAD_DIM ** -0.5, kv[..., 0, :])
         - sl * (qpos[None, None, :, None] - sel[:, None]).astype(F32))
    p = _masked_softmax(s, (sel <= qpos[None, :, None])[:, None])
    return jnp.einsum('bhqn,bqnd->bqhd', p, kv[..., 1, :])


def _kv_rows(u):
    return u.reshape(u.shape[0], u.shape[1], 2, HEAD_DIM)


def _mixers_prompt(za, zb, zc, zd, lp, sl_nsa, sl_dsa):
    b, t, _ = za.shape
    gw = GROUP_WIDTH
    ya, s_rwkv, shift = _rwkv_mixer(za, jnp.zeros((b, A_PAD), F32),
                                    jnp.zeros((b, N_HEADS, HEAD_DIM, HEAD_DIM), F32), lp)
    kv_c = _kv_rows(zb[..., gw:gw + 2 * HEAD_DIM])
    kv_s = _kv_rows(zb[..., gw + 2 * HEAD_DIM:gw + 4 * HEAD_DIM])
    kv_w = _kv_rows(zb[..., gw + 4 * HEAD_DIM:gw + 6 * HEAD_DIM])
    kvc = _nsa_compress(kv_c, lp).reshape(b, -1, 2 * HEAD_DIM)
    nc = kvc.shape[1]
    kvc = jnp.pad(kvc, ((0, 0), (0, LANES - nc), (0, 0)))
    kvc = jnp.concatenate([kvc[:, 0::2], kvc[:, 1::2]], axis=1)
    yb = _nsa_prompt(zb, kvc, sl_nsa)
    yc = _dsa_prompt(zc, sl_dsa)
    kv_d = _kv_rows(zc[..., gw:gw + 2 * HEAD_DIM])
    o = gw + 2 * HEAD_DIM + IDX_HEADS * IDX_DIM
    ki = zc[..., o:o + IDX_DIM]
    yd, conv_buf = _conv_mixer(zd, jnp.zeros((b, CONV_K - 1, gw), F32), lp['conv_w'])
    return (ya, yb, yc, yd), (s_rwkv, shift, kv_c, kv_s, kv_w[:, -min(NSA_WINDOW, t):], kv_d, ki, conv_buf)


def _mixers_sample(za, zb, zc, zd, lp, sl_nsa, sl_dsa, s_rwkv, s_shift, c_nsa_cmp, c_nsa_slc, s_win,
                   c_dsa_kv, c_dsa_idx, s_conv, page_table):
    b, t, _ = za.shape
    gw = GROUP_WIDTH
    n_past = page_table.shape[1] * PAGE_SIZE
    n_buf = s_win.shape[1]
    qpos = n_past + jnp.arange(t)
    shift_prev = jnp.pad(s_shift, ((0, 0), (0, A_PAD - A_COLS)))
    ya, s_rwkv_new, shift = _rwkv_mixer(za, shift_prev, s_rwkv, lp)
    q = zb[..., :gw].reshape(b, t, N_HEADS, HEAD_DIM)
    kv_c = _kv_rows(zb[..., gw:gw + 2 * HEAD_DIM])
    kv_s = _kv_rows(zb[..., gw + 2 * HEAD_DIM:gw + 4 * HEAD_DIM])
    kv_w = _kv_rows(zb[..., gw + 4 * HEAD_DIM:gw + 6 * HEAD_DIM])
    o = gw + 6 * HEAD_DIM
    gates = jax.nn.sigmoid(zb[..., o:o + 3 * N_HEADS]).reshape(b, t, N_HEADS, 3)
    past_c = c_nsa_cmp[page_table].reshape(b, n_past, 2, HEAD_DIM)
    kvc = _nsa_compress(jnp.concatenate([past_c, kv_c], axis=1), lp)
    kvw = jnp.concatenate([s_win, kv_w], axis=1)
    wpos = n_past - n_buf + jnp.arange(n_buf + t)
    yb = _nsa_attend_sample(q, gates, qpos, kvc, _paged_fetch(c_nsa_slc, page_table, kv_s), kvw, wpos,
                            sl_nsa, n_past + t)
    qc = zc[..., :gw].reshape(b, t, N_HEADS, HEAD_DIM)
    kv_d = _kv_rows(zc[..., gw:gw + 2 * HEAD_DIM])
    o = gw + 2 * HEAD_DIM
    qi = zc[..., o:o + IDX_HEADS * IDX_DIM].reshape(b, t, IDX_HEADS, IDX_DIM)
    o += IDX_HEADS * IDX_DIM
    ki = zc[..., o:o + IDX_DIM]
    wi = zc[..., o + IDX_DIM:o + IDX_DIM + IDX_HEADS]
    past_i = c_dsa_idx[page_table].reshape(b, n_past, IDX_DIM)
    ki_all = jnp.concatenate([past_i, ki], axis=1)
    yc = _dsa_attend_sample(qc, qi, wi, qpos, ki_all, _paged_fetch(c_dsa_kv, page_table, kv_d), sl_dsa, n_past + t)
    yd, conv_buf = _conv_mixer(zd, s_conv, lp['conv_w'])
    flat = lambda u: u.reshape(b, t, gw)
    return (ya, flat(yb), flat(yc), yd), (s_rwkv_new, shift, kv_c, kv_s, kvw[:, -n_buf:], kv_d, ki, conv_buf)


def _layer(x, mod, lp, pw, mixer_fn, alpha):
    b, t, d = x.shape
    sh1, sc1, g1, sh2, sc2, g2 = mod
    zs = _inproj(x, sc1, sh1, pw['w_in'])
    ys, states = mixer_fn(*zs)
    x1, h2, logits = _outproj(ys, x, g1, sc2, sh2, pw['w_out'], pw['ln1_g'], pw['ln1_b'],
                              pw['wr'], pw['br'], alpha)
    n_exp = lp['moe_wr'].shape[1]
    blk = _moe_block_rows(b * t * TOP_K, n_exp)
    y = _moe(h2.reshape(b * t, d), logits.reshape(b * t, LANES)[:, :n_exp], pw['moe'], blk)
    x2 = _resln(x1, y.reshape(b, t, d), g2, pw['ln2_g'], pw['ln2_b'], alpha)
    return x2, states


def _prep_weights(lp):
    d = lp['w_in'].shape[0]
    cols, o = [], 0
    for c, p in ((A_COLS, A_PAD), (B_COLS, B_PAD), (C_COLS, C_PAD), (D_COLS, D_PAD)):
        cols.append(jnp.pad(lp['w_in'][:, o:o + c], ((0, 0), (0, p - c))))
        o += c
    n_exp = lp['moe_wr'].shape[1]
    row = lambda u: u.reshape(1, -1)
    return {
        'w_in': jnp.concatenate(cols, axis=1).astype(BF16),
        'w_out': lp['w_out'].reshape(4, GROUP_WIDTH, d).astype(BF16),
        'ln1_g': row(lp['ln1_g']), 'ln1_b': row(lp['ln1_b']),
        'ln2_g': row(lp['ln2_g']), 'ln2_b': row(lp['ln2_b']),
        'wr': jnp.pad(lp['moe_wr'], ((0, 0), (0, LANES - n_exp))),
        'br': jnp.pad(lp['moe_br'], (0, LANES - n_exp)).reshape(1, LANES),
        'moe': (lp['moe_wgu'].astype(BF16), lp['moe_bgu'], lp['moe_wd'].astype(BF16), lp['moe_bd']),
    }


def kernel(x_prompt, x_sample, state_rwkv, state_rwkv_shift, cache_nsa_cmp, cache_nsa_slc, state_nsa_win, cache_dsa_kv, cache_dsa_idx, state_conv, page_table, c_prompt, c_sample, w_ada, b_ada, w_in, w_out, ln1_g, ln1_b, ln2_g, ln2_b, rwkv_mu, rwkv_w0, rwkv_w2, rwkv_a0, rwkv_a2, rwkv_g2, rwkv_kk, rwkv_ka, rwkv_rk, rwkv_gn_g, rwkv_gn_b, nsa_pe, nsa_w1, nsa_w2, conv_w, moe_wr, moe_br, moe_wgu, moe_bgu, moe_wd, moe_bd):
    depth = w_in.shape[0]
    alpha = (2 * depth) ** 0.25
    sl_nsa, sl_dsa = _alibi_slopes()
    bp = x_prompt.shape[0]
    hp, hs = x_prompt, x_sample
    c_all = jnp.concatenate([c_prompt, c_sample], axis=0)
    states_p, states_s = [], []
    for l in range(depth):
        lp = {
            'w_in': w_in[l], 'w_out': w_out[l], 'ln1_g': ln1_g[l], 'ln1_b': ln1_b[l],
            'ln2_g': ln2_g[l], 'ln2_b': ln2_b[l],
            'rwkv_mu': jnp.pad(rwkv_mu[l], (0, A_PAD - A_COLS)), 'rwkv_w0': rwkv_w0[l], 'rwkv_w2': rwkv_w2[l],
            'rwkv_a0': rwkv_a0[l], 'rwkv_a2': rwkv_a2[l], 'rwkv_g2': rwkv_g2[l],
            'rwkv_kk': rwkv_kk[l], 'rwkv_ka': rwkv_ka[l], 'rwkv_rk': rwkv_rk[l],
            'rwkv_gn_g': rwkv_gn_g[l], 'rwkv_gn_b': rwkv_gn_b[l],
            'nsa_pe': nsa_pe[l], 'nsa_w1': nsa_w1[l], 'nsa_w2': nsa_w2[l], 'conv_w': conv_w[l],
            'moe_wr': moe_wr[l], 'moe_br': moe_br[l], 'moe_wgu': moe_wgu[l], 'moe_bgu': moe_bgu[l],
            'moe_wd': moe_wd[l], 'moe_bd': moe_bd[l],
        }
        pw = _prep_weights(lp)
        mod_all = jax.nn.silu(c_all) @ w_ada[l] + b_ada[l]
        mod_p = [u[:, None, :] for u in jnp.split(mod_all[:bp], 6, axis=-1)]
        mod_s = [u[:, None, :] for u in jnp.split(mod_all[bp:], 6, axis=-1)]
        mix_p = functools.partial(_mixers_prompt, lp=lp, sl_nsa=sl_nsa, sl_dsa=sl_dsa)
        mix_s = functools.partial(
            _mixers_sample, lp=lp, sl_nsa=sl_nsa, sl_dsa=sl_dsa, s_rwkv=state_rwkv[l],
            s_shift=state_rwkv_shift[l], c_nsa_cmp=cache_nsa_cmp[l], c_nsa_slc=cache_nsa_slc[l],
            s_win=state_nsa_win[l], c_dsa_kv=cache_dsa_kv[l], c_dsa_idx=cache_dsa_idx[l],
            s_conv=state_conv[l], page_table=page_table)
        hp, st_p = _layer(hp, mod_p, lp, pw, mix_p, alpha)
        hs, st_s = _layer(hs, mod_s, lp, pw, mix_s, alpha)
        states_p.append(st_p)
        states_s.append(st_s)
    outs_p = [jnp.stack(v) for v in zip(*states_p)]
    outs_s = [jnp.stack(v) for v in zip(*states_s)]
    return (hp, hs, *outs_p, *outs_s)
```

```python
import functools

import jax
import jax.numpy as jnp
import numpy as np
from jax import lax
from jax.experimental import pallas as pl
from jax.experimental.pallas import tpu as pltpu

F32 = jnp.float32
BF16 = jnp.bfloat16

HEAD_DIM = 64
N_HEADS = 4
GROUP_WIDTH = N_HEADS * HEAD_DIM
RWKV_DECAY_SCALE = 0.606531
RWKV_GN_EPS = 64e-5
NSA_CMP_BLOCK = 32
NSA_SEL_BLOCK = 64
NSA_TOPN = 16
NSA_LOCAL_BLOCKS = 2
NSA_WINDOW = 512
DSA_TOPK = 256
IDX_HEADS = 4
IDX_DIM = 32
CONV_K = 3
TOP_K = 4
SWIGLU_LIMIT = 7.0
SWIGLU_ALPHA = 1.702
NEG_INF = -1e30
FORCE_SCORE = 1e4
Q_BLOCK = 128
PAGE_SIZE = 128

LANES = 128
A_COLS = 3 * GROUP_WIDTH + 32 + 32 + 64
B_COLS = GROUP_WIDTH + 6 * HEAD_DIM + 3 * N_HEADS
C_COLS = GROUP_WIDTH + 2 * HEAD_DIM + IDX_HEADS * IDX_DIM + IDX_DIM + IDX_HEADS
D_COLS = 3 * GROUP_WIDTH


def _pad_to(n, m):
    return -(-n // m) * m


A_PAD, B_PAD, C_PAD, D_PAD = (_pad_to(c, LANES) for c in (A_COLS, B_COLS, C_COLS, D_COLS))
VMEM_LIMIT = 56 * 1024 * 1024

_NT = (((1,), (1,)), ((), ()))


def _alibi_slopes():
    n = 2 * N_HEADS
    s = [2.0 ** (-8.0 * i / n) for i in range(1, n + 1)]
    return s[0::2], s[1::2]


def _ln_rows(x, eps=1e-5):
    mu = jnp.mean(x, axis=-1, keepdims=True)
    xc = x - mu
    var = jnp.mean(xc * xc, axis=-1, keepdims=True)
    return xc * lax.rsqrt(var + eps)


def _cparams(sem):
    return pltpu.CompilerParams(dimension_semantics=sem, vmem_limit_bytes=VMEM_LIMIT)


def _inproj_kernel(x_ref, sc_ref, sh_ref, w_ref, za_ref, zb_ref, zc_ref, zd_ref):
    h = _ln_rows(x_ref[0]) * (1.0 + sc_ref[0]) + sh_ref[0]
    z = jnp.dot(h.astype(BF16), w_ref[...], preferred_element_type=F32)
    o = 0
    for ref, w in ((za_ref, A_PAD), (zb_ref, B_PAD), (zc_ref, C_PAD), (zd_ref, D_PAD)):
        ref[0] = z[:, o:o + w]
        o += w


def _inproj(x, sc, sh, w_pad):
    b, t, d = x.shape
    tm = min(t, 512)
    mod_rows = sc.shape[1]
    mod_blk = (1, tm, d) if mod_rows == t and t > 1 else (1, 1, d)
    mod_map = (lambda i, j: (i, j, 0)) if mod_rows == t and t > 1 else (lambda i, j: (i, 0, 0))
    widths = (A_PAD, B_PAD, C_PAD, D_PAD)
    return pl.pallas_call(
        _inproj_kernel,
        grid=(b, t // tm),
        in_specs=[pl.BlockSpec((1, tm, d), lambda i, j: (i, j, 0)),
                  pl.BlockSpec(mod_blk, mod_map),
                  pl.BlockSpec(mod_blk, mod_map),
                  pl.BlockSpec(w_pad.shape, lambda i, j: (0, 0))],
        out_specs=[pl.BlockSpec((1, tm, w), lambda i, j: (i, j, 0)) for w in widths],
        out_shape=[jax.ShapeDtypeStruct((b, t, w), F32) for w in widths],
        compiler_params=_cparams(("parallel", "parallel")),
        name="inproj",
    )(x, sc, sh, w_pad)


def _rwkv_scan_kernel(r_ref, w_ref, k_ref, v_ref, kk_ref, kka_ref, s0_ref, y_ref, sfin_ref,
                      s_scr, y_scr, *, bg, tc, unroll):
    c = pl.program_id(1)
    rows = bg * HEAD_DIM

    @pl.when(c == 0)
    def _():
        s_scr[...] = s0_ref[...].reshape(rows, GROUP_WIDTH)

    li = lax.broadcasted_iota(jnp.int32, (2 * GROUP_WIDTH, GROUP_WIDTH), 0)
    lj = lax.broadcasted_iota(jnp.int32, (2 * GROUP_WIDTH, GROUP_WIDTH), 1)
    seg_ones2 = jnp.where((li % GROUP_WIDTH) // HEAD_DIM == lj // HEAD_DIM, 1.0, 0.0).astype(BF16)
    seg_ones1 = seg_ones2[:GROUP_WIDTH]
    lane = lax.broadcasted_iota(jnp.int32, (rows, GROUP_WIDTH), 1)
    row = lax.broadcasted_iota(jnp.int32, (rows, GROUP_WIDTH), 0)
    diag = jnp.where(row % HEAD_DIM == lane % HEAD_DIM, 1.0, 0.0)

    def seg_sum(x):
        hi = x.astype(BF16)
        lo = (x - hi.astype(F32)).astype(BF16)
        return jnp.dot(jnp.concatenate([hi, lo], axis=1), seg_ones2, preferred_element_type=F32)

    def bcast_rows(ref, t):
        return jnp.concatenate(
            [jnp.broadcast_to(ref[b, pl.ds(t, 1), :], (HEAD_DIM, GROUP_WIDTH)) for b in range(bg)], axis=0)

    def step(t):
        s = s_scr[...]
        kk = bcast_rows(kk_ref, t)
        sa = seg_sum(s * kk)
        vcol = seg_sum(bcast_rows(v_ref, t) * diag)
        s = s * bcast_rows(w_ref, t) - sa * bcast_rows(kka_ref, t) + vcol * bcast_rows(k_ref, t)
        s_scr[...] = s
        y = jnp.dot((s * bcast_rows(r_ref, t)).astype(BF16), seg_ones1, preferred_element_type=F32)
        y_scr[...] = jnp.where(lane % HEAD_DIM == t, y, y_scr[...])

    def outer(i, carry):
        for u in range(unroll):
            step(i * unroll + u)
        return carry

    lax.fori_loop(0, tc // unroll, outer, 0)
    y_ref[...] = y_scr[...].reshape(bg, 1, HEAD_DIM, GROUP_WIDTH)

    @pl.when(c == pl.num_programs(1) - 1)
    def _():
        sfin_ref[...] = s_scr[...].reshape(bg, HEAD_DIM, GROUP_WIDTH)


def _rwkv_scan(r, w, k, v, kk, kka, s0):
    b, t, gw = r.shape
    bg = 8 if b % 8 == 0 else b
    tc = min(t, HEAD_DIM)
    unroll = 4 if tc % 4 == 0 else 1
    nc = t // tc
    s0l = s0.transpose(0, 2, 1, 3).reshape(b, HEAD_DIM, gw)
    seq = pl.BlockSpec((bg, tc, gw), lambda i, c: (i, c, 0))
    st = pl.BlockSpec((bg, HEAD_DIM, gw), lambda i, c: (i, 0, 0))
    y_raw, s_fin = pl.pallas_call(
        functools.partial(_rwkv_scan_kernel, bg=bg, tc=tc, unroll=unroll),
        grid=(b // bg, nc),
        in_specs=[seq] * 6 + [st],
        out_specs=[pl.BlockSpec((bg, 1, HEAD_DIM, gw), lambda i, c: (i, c, 0, 0)), st],
        out_shape=[jax.ShapeDtypeStruct((b, nc, HEAD_DIM, gw), F32),
                   jax.ShapeDtypeStruct((b, HEAD_DIM, gw), F32)],
        scratch_shapes=[pltpu.VMEM((bg * HEAD_DIM, gw), F32), pltpu.VMEM((bg * HEAD_DIM, gw), F32)],
        compiler_params=_cparams(("parallel", "arbitrary")),
        name="rwkv_scan",
    )(r, w, k, v, kk, kka, s0l)
    y = y_raw.reshape(b, nc, HEAD_DIM, N_HEADS, HEAD_DIM)[..., :tc]
    y = y.transpose(0, 1, 4, 3, 2).reshape(b, t, N_HEADS, HEAD_DIM)
    s_fin = s_fin.reshape(b, HEAD_DIM, N_HEADS, HEAD_DIM).transpose(0, 2, 1, 3)
    return y, s_fin


def _rwkv_mixer(za, shift_prev, s0, lp):
    b, t, _ = za.shape
    z_prev = jnp.concatenate([shift_prev[:, None], za[:, :-1]], axis=1)
    zs = za + lp['rwkv_mu'] * (z_prev - za)
    gw = GROUP_WIDTH
    r, k, v = zs[..., :gw], zs[..., gw:2 * gw], zs[..., 2 * gw:3 * gw]
    o = 3 * gw
    w_lo, a_lo, g_lo = zs[..., o:o + 32], zs[..., o + 32:o + 64], zs[..., o + 64:o + 128]
    decay = jnp.exp(-RWKV_DECAY_SCALE * jax.nn.sigmoid(lp['rwkv_w0'] + jnp.tanh(w_lo) @ lp['rwkv_w2']))
    a = jax.nn.sigmoid(lp['rwkv_a0'] + a_lo @ lp['rwkv_a2'])
    g = jax.nn.sigmoid(g_lo) @ lp['rwkv_g2']
    heads = lambda u: u.reshape(b, t, N_HEADS, HEAD_DIM)
    kk = heads(k) * lp['rwkv_kk']
    kk = kk / jnp.maximum(jnp.sqrt(jnp.sum(kk * kk, axis=-1, keepdims=True)), 1e-12)
    kk = kk.reshape(b, t, gw)
    k = k * (1.0 + (a - 1.0) * lp['rwkv_ka'].reshape(gw))
    y, s_fin = _rwkv_scan(r, decay, k, v, kk, kk * a, s0)
    mu = jnp.mean(y, axis=-1, keepdims=True)
    var = jnp.mean(jnp.square(y - mu), axis=-1, keepdims=True)
    yn = ((y - mu) * lax.rsqrt(var + RWKV_GN_EPS) * lp['rwkv_gn_g'].reshape(N_HEADS, HEAD_DIM)
          + lp['rwkv_gn_b'].reshape(N_HEADS, HEAD_DIM))
    bonus = jnp.sum(heads(r) * heads(k) * lp['rwkv_rk'], axis=-1, keepdims=True) * heads(v)
    out = (yn + bonus).reshape(b, t, gw) * g
    return out, s_fin, za[:, -1, :A_COLS]


KEY_TILE = 512
POS_SPLIT = 128
M_FLOOR = -1e29


def _aug_keys(kv, pos):
    lane = lax.broadcasted_iota(jnp.int32, kv.shape, 1)
    hi = (pos // POS_SPLIT).astype(F32)
    lo = (pos % POS_SPLIT).astype(F32)
    extra = jnp.where(lane == 0, hi, jnp.where(lane == 1, lo, 0.0))
    return jnp.concatenate([kv.astype(BF16), extra.astype(BF16)], axis=1)


def _aug_queries(q2, slopes, lane):
    rows = []
    for h in range(N_HEADS):
        pair = q2[:, (h // 2) * LANES:(h // 2 + 1) * LANES]
        if h % 2:
            pair = pltpu.roll(pair, HEAD_DIM, 1)
        extra = jnp.where(lane == 0, POS_SPLIT * slopes[h], jnp.where(lane == 1, slopes[h], 0.0))
        rows.append(jnp.concatenate([jnp.where(lane < HEAD_DIM, pair, 0.0), extra], axis=1))
    return jnp.concatenate(rows, axis=0).astype(BF16)


def _tile_heads(x):
    return jnp.concatenate([x] * N_HEADS, axis=0)


def _softmax_block(qa, ka, madd):
    s = lax.dot_general(qa, ka, _NT, preferred_element_type=F32) + _tile_heads(madd)
    m = jnp.maximum(jnp.max(s, axis=1, keepdims=True), M_FLOOR)
    e = jnp.exp(s - m)
    inv = 1.0 / jnp.maximum(jnp.sum(e, axis=1, keepdims=True), 1e-30)
    o = jnp.dot(e.astype(BF16), ka, preferred_element_type=F32) * inv
    return e, inv, o


def _flash_step(qa, ka, madd, m, l, acc):
    s = lax.dot_general(qa, ka, _NT, preferred_element_type=F32) + _tile_heads(madd)
    m_new = jnp.maximum(m, jnp.max(s, axis=1, keepdims=True))
    alpha = jnp.exp(m - m_new)
    p = jnp.exp(s - m_new)
    l = alpha * l + jnp.sum(p, axis=1, keepdims=True)
    acc = alpha * acc + jnp.dot(p.astype(BF16), ka, preferred_element_type=F32)
    return m_new, l, acc


def _flash_init(qb=Q_BLOCK):
    rows = N_HEADS * qb
    return (jnp.full((rows, 1), M_FLOOR, F32), jnp.zeros((rows, 1), F32), jnp.zeros((rows, 2 * LANES), F32))


def _heads_out(per_head, lane):
    pair = lambda a, b: jnp.where(lane < HEAD_DIM, pltpu.roll(a, HEAD_DIM, 1), b)
    return jnp.concatenate([pair(per_head[0], per_head[1]), pair(per_head[2], per_head[3])], axis=1)


def _head_rows(o, h, qb=Q_BLOCK):
    return o[h * qb:(h + 1) * qb, :LANES]


def _nsa_kernel(q_ref, ksv_ref, kwv_ref, gl_ref, kvc_ref, exp_ref, o_ref, ks_scr, kw_scr,
                *, t, n_win, n_top, slopes):
    qb = Q_BLOCK
    j = pl.program_id(1)
    q0 = j * qb

    @pl.when(j == 0)
    def _():
        for c in range(t // KEY_TILE):
            rows = slice(c * KEY_TILE, (c + 1) * KEY_TILE)
            pos = c * KEY_TILE + lax.broadcasted_iota(jnp.int32, (KEY_TILE, 1), 0)
            ks_scr[rows, :] = _aug_keys(ksv_ref[0, rows, :], pos)
            kw_scr[rows, :] = _aug_keys(kwv_ref[0, rows, :], pos)

    lane = lax.broadcasted_iota(jnp.int32, (qb, LANES), 1)
    qpos = q0 + lax.broadcasted_iota(jnp.int32, (qb, 1), 0)
    qa = _aug_queries(q_ref[0] * (HEAD_DIM ** -0.5), slopes, lane)
    gates = jax.nn.sigmoid(gl_ref[0])

    perm_end = lambda i: (2 * (i % HEAD_DIM) + i // HEAD_DIM) * NSA_CMP_BLOCK + (NSA_CMP_BLOCK - 1)
    kca = _aug_keys(kvc_ref[0], perm_end(lax.broadcasted_iota(jnp.int32, (LANES, 1), 0)))
    cend = perm_end(lax.broadcasted_iota(jnp.int32, (1, LANES), 1))
    e, inv, o_c = _softmax_block(qa, kca, jnp.where(cend <= qpos, 0.0, NEG_INF))
    p_c = e * inv
    imp = _head_rows(p_c, 0)
    for h in range(1, N_HEADS):
        imp = imp + _head_rows(p_c, h)
    imp = imp + pltpu.roll(imp, HEAD_DIM, 1)

    cur = qpos // NSA_SEL_BLOCK
    forced = (lane == 0) | (lane > cur - NSA_LOCAL_BLOCKS)
    score = jnp.where(lane > cur, NEG_INF, jnp.where(forced, FORCE_SCORE, imp))
    score = jnp.where(lane < HEAD_DIM, score, -3e38)
    rank = jnp.zeros((qb, LANES), F32)
    for i in range(NSA_SEL_BLOCK):
        col = score[:, i:i + 1]
        beats = (col > score) | ((col == score) & (lane > i))
        rank = rank + jnp.where(beats, 1.0, 0.0)
    sel = jnp.where((rank < n_top) & (lane < HEAD_DIM), 1.0, 0.0).astype(BF16)

    def sel_tile(kt, carry):
        k0 = pl.multiple_of(kt * KEY_TILE, KEY_TILE)
        kpos = k0 + lax.broadcasted_iota(jnp.int32, (1, KEY_TILE), 1)
        member = jnp.dot(sel, exp_ref[kt], preferred_element_type=F32) > 0.5
        madd = jnp.where(member & (kpos <= qpos), 0.0, NEG_INF)
        return _flash_step(qa, ks_scr[pl.ds(k0, KEY_TILE), :], madd, *carry)

    n_tiles = (q0 + qb - 1) // KEY_TILE + 1
    _, l_s, acc_s = lax.fori_loop(0, n_tiles, sel_tile, _flash_init())
    o_s = acc_s * (1.0 / jnp.maximum(l_s, 1e-30))

    kst = pl.multiple_of(jnp.maximum(q0 + qb - n_win, 0), qb)
    dw = qpos - (kst + lax.broadcasted_iota(jnp.int32, (1, n_win), 1))
    wadd = jnp.where((dw >= 0) & (dw <= NSA_WINDOW), 0.0, NEG_INF)
    _, _, o_w = _softmax_block(qa, kw_scr[pl.ds(kst, n_win), :], wadd)

    outs = []
    for h in range(N_HEADS):
        g = [gates[:, 3 * h + i:3 * h + i + 1] for i in range(3)]
        outs.append(g[0] * _head_rows(o_c, h) + g[1] * _head_rows(o_s, h) + g[2] * _head_rows(o_w, h))
    o_ref[0] = _heads_out(outs, lane)


def _nsa_prompt(zb, kvc_perm, slopes):
    b, t, _ = zb.shape
    assert t % KEY_TILE == 0 and t // NSA_CMP_BLOCK <= LANES and t <= 256 * POS_SPLIT
    n_win = min(NSA_WINDOW + Q_BLOCK, t)
    n_sel = -(-t // NSA_SEL_BLOCK)
    n_kt = t // KEY_TILE
    key_blk = (np.arange(t) // NSA_SEL_BLOCK).reshape(n_kt, 1, KEY_TILE)
    expand = jnp.asarray((np.arange(LANES)[None, :, None] == key_blk).astype(np.float32), BF16)
    full = lambda c: pl.BlockSpec((1, t, LANES), lambda i, j, c=c: (i, 0, c))
    return pl.pallas_call(
        functools.partial(_nsa_kernel, t=t, n_win=n_win, n_top=min(NSA_TOPN, n_sel), slopes=slopes),
        grid=(b, t // Q_BLOCK),
        in_specs=[pl.BlockSpec((1, Q_BLOCK, 2 * LANES), lambda i, j: (i, j, 0)),
                  full(3), full(4),
                  pl.BlockSpec((1, Q_BLOCK, LANES), lambda i, j: (i, j, 5)),
                  pl.BlockSpec((1, LANES, LANES), lambda i, j: (i, 0, 0)),
                  pl.BlockSpec((n_kt, LANES, KEY_TILE), lambda i, j: (0, 0, 0))],
        out_specs=pl.BlockSpec((1, Q_BLOCK, GROUP_WIDTH), lambda i, j: (i, j, 0)),
        out_shape=jax.ShapeDtypeStruct((b, t, GROUP_WIDTH), F32),
        scratch_shapes=[pltpu.VMEM((t, 2 * LANES), BF16), pltpu.VMEM((t, 2 * LANES), BF16)],
        compiler_params=_cparams(("parallel", "arbitrary")),
        name="nsa_prompt",
    )(zb, zb, zb, zb, kvc_perm, expand)


def _nsa_compress(rows, lp):
    b, l = rows.shape[:2]
    nc = l // NSA_CMP_BLOCK
    blk = rows[:, :nc * NSA_CMP_BLOCK].reshape(b, nc, NSA_CMP_BLOCK, 2, HEAD_DIM) + lp['nsa_pe']
    flat = jnp.swapaxes(blk, 2, 3).reshape(b, nc, 2, NSA_CMP_BLOCK * HEAD_DIM)
    hid = jax.nn.gelu(jnp.einsum('bnci,cih->bnch', flat, lp['nsa_w1']))
    return jnp.einsum('bnch,chd->bncd', hid, lp['nsa_w2'])


def _dsa_keyprep_kernel(kv_ref, ki_ref, ka_ref, kip_ref):
    pos = pl.program_id(1) * KEY_TILE + lax.broadcasted_iota(jnp.int32, (KEY_TILE, 1), 0)
    ka_ref[0] = _aug_keys(kv_ref[0], pos)
    ki = ki_ref[0]
    if ki.shape[1] == LANES:
        ki = jnp.where(lax.broadcasted_iota(jnp.int32, ki.shape, 1) < IDX_DIM, ki, 0.0)
    else:
        ki = jnp.concatenate([ki, jnp.zeros((KEY_TILE, LANES - IDX_DIM), F32)], axis=1)
    hi = ki.astype(BF16).astype(F32)
    kip_ref[0] = (hi + pltpu.roll(ki - hi, IDX_DIM, 1) + pltpu.roll(hi, 2 * IDX_DIM, 1)).astype(BF16)


def _dsa_keyprep(kv_src, kv_col, ki_src, ki_col):
    b, t, _ = kv_src.shape
    ki_w = min(ki_src.shape[2], LANES)
    return pl.pallas_call(
        _dsa_keyprep_kernel,
        grid=(b, t // KEY_TILE),
        in_specs=[pl.BlockSpec((1, KEY_TILE, LANES), lambda i, c: (i, c, kv_col)),
                  pl.BlockSpec((1, KEY_TILE, ki_w), lambda i, c: (i, c, ki_col))],
        out_specs=[pl.BlockSpec((1, KEY_TILE, 2 * LANES), lambda i, c: (i, c, 0)),
                   pl.BlockSpec((1, KEY_TILE, LANES), lambda i, c: (i, c, 0))],
        out_shape=[jax.ShapeDtypeStruct((b, t, 2 * LANES), BF16), jax.ShapeDtypeStruct((b, t, LANES), BF16)],
        compiler_params=_cparams(("parallel", "parallel")),
        name="dsa_keyprep",
    )(kv_src, ki_src)


def _dsa_kernel(q_ref, ka_ref, qi_ref, kip_ref, wiq_ref, o_ref, skey_scr, *, t, qb, q_base, k_sel, slopes):
    n_sub = KEY_TILE // LANES
    ka_scr, kip_scr = ka_ref.at[0], kip_ref.at[0]
    lane = lax.broadcasted_iota(jnp.int32, (qb, LANES), 1)
    if q_base is None:
        q0 = pl.program_id(1) * qb
        qpos = q0 + lax.broadcasted_iota(jnp.int32, (qb, 1), 0)
        n_tiles = (q0 + qb - 1) // KEY_TILE + 1
    else:
        qpos = jnp.full((qb, 1), q_base, jnp.int32)
        n_tiles = t // KEY_TILE
    tile_pos = lambda kt: kt * KEY_TILE + lax.broadcasted_iota(jnp.int32, (1, KEY_TILE), 1)

    qi = qi_ref[0]
    wq = wiq_ref[0]
    packed_q = []
    for h in range(IDX_HEADS):
        qh = qi if h == 0 else pltpu.roll(qi, LANES - IDX_DIM * h, 1)
        qh = jnp.where(lane < IDX_DIM, qh, 0.0)
        hi = qh.astype(BF16).astype(F32)
        packed_q.append(hi + pltpu.roll(hi, IDX_DIM, 1) + pltpu.roll(qh - hi, 2 * IDX_DIM, 1))
    packed_q = jnp.concatenate(packed_q, axis=0).astype(BF16)
    w_cols = [wq[:, IDX_DIM + h:IDX_DIM + h + 1] for h in range(IDX_HEADS)]

    def idx_tile(kt, carry):
        k0 = pl.multiple_of(kt * KEY_TILE, KEY_TILE)
        rel = jnp.maximum(lax.dot_general(packed_q, kip_scr[pl.ds(k0, KEY_TILE), :], _NT,
                                          preferred_element_type=F32), 0.0)
        score = w_cols[0] * rel[:qb]
        for h in range(1, IDX_HEADS):
            score = score + w_cols[h] * rel[h * qb:(h + 1) * qb]
        score = jnp.where(score == 0.0, 0.0, score)
        score = jnp.where(tile_pos(kt) <= qpos, score, NEG_INF)
        bits = pltpu.bitcast(score, jnp.int32)
        skey_scr[kt] = bits ^ ((bits >> 31) & 0x7FFFFFFF)
        return carry

    lax.fori_loop(0, n_tiles, idx_tile, 0)

    kf = float(k_sel)

    def count(c, strict):
        cb = jnp.broadcast_to(c, (qb, LANES))

        def body(kt, acc):
            x = skey_scr[kt]
            for u in range(n_sub):
                xu = x[:, u * LANES:(u + 1) * LANES]
                acc = acc + jnp.where((xu > cb) if strict else (xu >= cb), 1.0, 0.0)
            return acc

        acc = lax.fori_loop(0, n_tiles, body, jnp.zeros((qb, LANES), F32))
        return jnp.sum(acc, axis=1, keepdims=True)

    int_min = jnp.int32(-2 ** 31)
    tau = jnp.where(count(jnp.zeros((qb, 1), jnp.int32), False) >= kf, jnp.int32(0), int_min)

    def bit_step(i, tau):
        cand = tau + lax.shift_left(jnp.int32(1), 30 - i)
        return jnp.where(count(cand, False) >= kf, cand, tau)

    tau = lax.fori_loop(0, 31, bit_step, tau)
    need = kf - count(tau, True)
    taub = jnp.broadcast_to(tau, (qb, LANES))
    ui = lax.broadcasted_iota(jnp.int32, (LANES, LANES), 0)
    uj = lax.broadcasted_iota(jnp.int32, (LANES, LANES), 1)
    upper = jnp.where(ui <= uj, 1.0, 0.0).astype(BF16)
    qa = _aug_queries(q_ref[0] * (HEAD_DIM ** -0.5), slopes, lane)

    def att_tile(kt, carry):
        ties, m, l, acc = carry
        k0 = pl.multiple_of(kt * KEY_TILE, KEY_TILE)
        x = skey_scr[kt]
        parts = []
        for u in range(n_sub):
            xu = x[:, u * LANES:(u + 1) * LANES]
            eq = xu == taub
            eqf = jnp.where(eq, 1.0, 0.0)
            before = jnp.dot(eqf.astype(BF16), upper, preferred_element_type=F32) + ties
            parts.append(jnp.where((xu > taub) | (eq & (before <= need)), 0.0, NEG_INF))
            ties = ties + jnp.sum(eqf, axis=1, keepdims=True)
        madd = jnp.where(tile_pos(kt) <= qpos, jnp.concatenate(parts, axis=1), NEG_INF)
        return (ties,) + _flash_step(qa, ka_scr[pl.ds(k0, KEY_TILE), :], madd, m, l, acc)

    _, _, l, acc = lax.fori_loop(0, n_tiles, att_tile, (jnp.zeros((qb, 1), F32),) + _flash_init(qb))
    o = acc * (1.0 / jnp.maximum(l, 1e-30))
    o_ref[0] = _heads_out([_head_rows(o, h, qb) for h in range(N_HEADS)], lane)


def _dsa_attend(zq, ka, kip, slopes, *, qb, q_base, n_keys):
    b, tq, _ = zq.shape
    t = ka.shape[1]
    assert t % KEY_TILE == 0 and t <= 256 * POS_SPLIT and tq % qb == 0
    k_sel = min(DSA_TOPK, n_keys // 4)
    assert k_sel <= KEY_TILE
    full = lambda w: pl.BlockSpec((1, t, w), lambda i, j: (i, 0, 0))
    qblk = lambda c: pl.BlockSpec((1, qb, LANES), lambda i, j, c=c: (i, j, c))
    return pl.pallas_call(
        functools.partial(_dsa_kernel, t=t, qb=qb, q_base=q_base, k_sel=k_sel, slopes=slopes),
        grid=(b, tq // qb),
        in_specs=[pl.BlockSpec((1, qb, 2 * LANES), lambda i, j: (i, j, 0)),
                  full(2 * LANES), qblk(3), full(LANES), qblk(4)],
        out_specs=pl.BlockSpec((1, qb, GROUP_WIDTH), lambda i, j: (i, j, 0)),
        out_shape=jax.ShapeDtypeStruct((b, tq, GROUP_WIDTH), F32),
        scratch_shapes=[pltpu.VMEM((t // KEY_TILE, qb, KEY_TILE), jnp.int32)],
        compiler_params=_cparams(("parallel", "parallel")),
        name="dsa_attend",
    )(zq, ka, zq, kip, zq)


def _dsa_prompt(zc, slopes):
    ka, kip = _dsa_keyprep(zc, 2, zc, 4)
    return _dsa_attend(zc, ka, kip, slopes, qb=Q_BLOCK, q_base=None, n_keys=zc.shape[1])


def _outproj_kernel(ya_ref, yb_ref, yc_ref, yd_ref, x_ref, g1_ref, sc2_ref, sh2_ref, wo_ref, lg_ref, lb_ref,
                    wr_ref, br_ref, x1_ref, h2_ref, logit_ref, *, alpha):
    mix = jnp.zeros(x_ref.shape[1:], F32)
    for i, ref in enumerate((ya_ref, yb_ref, yc_ref, yd_ref)):
        mix = mix + jnp.dot(ref[0].astype(BF16), wo_ref[i], preferred_element_type=F32)
    x1 = _ln_rows(alpha * x_ref[0] + g1_ref[0] * mix) * lg_ref[...] + lb_ref[...]
    x1_ref[0] = x1
    h2 = _ln_rows(x1) * (1.0 + sc2_ref[0]) + sh2_ref[0]
    h2_ref[0] = h2.astype(BF16)
    logit_ref[0] = jnp.dot(h2, wr_ref[...], preferred_element_type=F32,
                           precision=lax.Precision.HIGHEST) + br_ref[...]


def _outproj(ys, x, g1, sc2, sh2, wo, ln_g, ln_b, wr_pad, br_pad, alpha):
    b, t, d = x.shape
    tm = min(t, 512)
    per_row = g1.shape[1] == t and t > 1
    mod_blk = (1, tm, d) if per_row else (1, 1, d)
    mod_map = (lambda i, j: (i, j, 0)) if per_row else (lambda i, j: (i, 0, 0))
    mod = pl.BlockSpec(mod_blk, mod_map)
    row = lambda w: pl.BlockSpec((1, tm, w), lambda i, j: (i, j, 0))
    const = lambda a: pl.BlockSpec(a.shape, lambda i, j: (0,) * a.ndim)
    return pl.pallas_call(
        functools.partial(_outproj_kernel, alpha=alpha),
        grid=(b, t // tm),
        in_specs=[row(GROUP_WIDTH)] * 4 + [row(d), mod, mod, mod, const(wo), const(ln_g), const(ln_b),
                                           const(wr_pad), const(br_pad)],
        out_specs=[row(d), row(d), row(LANES)],
        out_shape=[jax.ShapeDtypeStruct((b, t, d), F32), jax.ShapeDtypeStruct((b, t, d), BF16),
                   jax.ShapeDtypeStruct((b, t, LANES), F32)],
        compiler_params=_cparams(("parallel", "parallel")),
        name="outproj",
    )(*ys, x, g1, sc2, sh2, wo, ln_g, ln_b, wr_pad, br_pad)


def _moe_kernel(be_ref, nu_ref, x_ref, wgu_ref, bgu_ref, wd_ref, bd_ref, o_ref, wgu_bf, wd_bf):
    i = pl.program_id(0)
    ff = wd_ref.shape[1]

    @pl.when((i == 0) | (be_ref[i] != be_ref[jnp.maximum(i - 1, 0)]))
    def _():
        wgu_bf[...] = wgu_ref[0].astype(BF16)
        wd_bf[...] = wd_ref[0].astype(BF16)

    @pl.when(i < nu_ref[0])
    def _():
        gu = jnp.dot(x_ref[...], wgu_bf[...], preferred_element_type=F32) + bgu_ref[0]
        gt = jnp.minimum(gu[:, :ff], SWIGLU_LIMIT)
        up = jnp.clip(gu[:, ff:], -SWIGLU_LIMIT, SWIGLU_LIMIT)
        act = (up + 1.0) * gt * jax.nn.sigmoid(SWIGLU_ALPHA * gt)
        o_ref[...] = jnp.dot(act.astype(BF16), wd_bf[...], preferred_element_type=F32) + bd_ref[0]

    @pl.when(i >= nu_ref[0])
    def _():
        o_ref[...] = jnp.zeros(o_ref.shape, F32)


def _moe_experts(xg, blk_exp, n_used, wgu, bgu, wd, bd, blk):
    cap, d = xg.shape
    e, _, ff2 = wgu.shape
    ff = ff2 // 2
    grid_spec = pltpu.PrefetchScalarGridSpec(
        num_scalar_prefetch=2,
        grid=(cap // blk,),
        in_specs=[pl.BlockSpec((blk, d), lambda i, be, nu: (i, 0)),
                  pl.BlockSpec((1, d, ff2), lambda i, be, nu: (be[i], 0, 0)),
                  pl.BlockSpec((1, 1, ff2), lambda i, be, nu: (be[i], 0, 0)),
                  pl.BlockSpec((1, ff, d), lambda i, be, nu: (be[i], 0, 0)),
                  pl.BlockSpec((1, 1, d), lambda i, be, nu: (be[i], 0, 0))],
        out_specs=pl.BlockSpec((blk, d), lambda i, be, nu: (i, 0)),
        scratch_shapes=[pltpu.VMEM((d, ff2), BF16), pltpu.VMEM((ff, d), BF16)],
    )
    return pl.pallas_call(
        _moe_kernel,
        grid_spec=grid_spec,
        out_shape=jax.ShapeDtypeStruct((cap, d), F32),
        compiler_params=_cparams(("arbitrary",)),
        name="moe_experts",
    )(blk_exp, n_used, xg, wgu, bgu.reshape(e, 1, ff2), wd, bd.reshape(e, 1, d))


def _slot_tokens_kernel(pos_ref, out_ref, *, chunk, shift):
    c = pl.program_id(0)

    @pl.when(c == 0)
    def _():
        def zero(i, carry):
            out_ref[i] = jnp.int32(0)
            return carry
        lax.fori_loop(0, out_ref.shape[0], zero, 0, unroll=8)

    base = c * chunk

    def body(i, carry):
        out_ref[pos_ref[i]] = lax.shift_right_logical(base + i, shift)
        return carry

    lax.fori_loop(0, chunk, body, 0, unroll=8)


def _slot_tokens(pos, cap):
    m = pos.shape[0]
    assert TOP_K & (TOP_K - 1) == 0
    chunk = min(m, 8192)
    assert m % chunk == 0
    return pl.pallas_call(
        functools.partial(_slot_tokens_kernel, chunk=chunk, shift=TOP_K.bit_length() - 1),
        grid=(m // chunk,),
        in_specs=[pl.BlockSpec((chunk,), lambda c: (c,), memory_space=pltpu.SMEM)],
        out_specs=pl.BlockSpec((cap,), lambda c: (0,), memory_space=pltpu.SMEM),
        out_shape=jax.ShapeDtypeStruct((cap,), jnp.int32),
        compiler_params=pltpu.CompilerParams(dimension_semantics=("arbitrary",)),
        name="slot_tokens",
    )(pos)


def _moe(h2, logits, mw, blk):
    n, d = h2.shape
    n_exp = logits.shape[1]
    top_v, top_e = lax.top_k(logits, TOP_K)
    gate = jax.nn.softmax(top_v, axis=-1)
    m = n * TOP_K
    cap = -(-(m + n_exp * (blk - 1)) // blk) * blk
    e_flat = top_e.reshape(m)
    onehot = (e_flat[:, None] == jnp.arange(n_exp)[None, :]).astype(jnp.int32)
    csum = jnp.cumsum(onehot, axis=0)
    counts = csum[-1]
    rank = jnp.take_along_axis(csum, e_flat[:, None], axis=1)[:, 0] - 1
    padded = (counts + blk - 1) // blk * blk
    pend = jnp.cumsum(padded)
    pstart = pend - padded
    pos = (pstart[e_flat] + rank).astype(jnp.int32)
    nblk = cap // blk
    blk_start = jnp.arange(nblk, dtype=jnp.int32) * blk
    blk_exp = jnp.minimum(jnp.sum((pend[None, :] <= blk_start[:, None]).astype(jnp.int32), axis=1), n_exp - 1)
    row_tok = _slot_tokens(pos, cap)
    n_used = (pend[-1] // blk).astype(jnp.int32).reshape(1)
    xg = h2.at[row_tok].get(mode='promise_in_bounds')
    out = _moe_experts(xg, blk_exp.astype(jnp.int32), n_used, *mw, blk)
    pos = pos.reshape(n, TOP_K)
    picked = [out.at[pos[:, j]].get(mode='promise_in_bounds') for j in range(TOP_K)]
    return picked, jnp.pad(gate, ((0, 0), (0, LANES - TOP_K)))


def _moe_block_rows(n_assign, n_exp):
    per_expert = max(1, n_assign // n_exp)
    return min(256, max(16, 1 << (per_expert.bit_length() - 1)))


def _resln_kernel(x_ref, gate_ref, *refs, alpha):
    picked, (g_ref, lg_ref, lb_ref, o_ref) = refs[:TOP_K], refs[TOP_K:]
    gate = gate_ref[0]
    y = gate[:, 0:1] * picked[0][0]
    for j in range(1, TOP_K):
        y = y + gate[:, j:j + 1] * picked[j][0]
    o_ref[0] = _ln_rows(alpha * x_ref[0] + g_ref[0] * y) * lg_ref[...] + lb_ref[...]


def _resln(x, picked, gate, g, ln_g, ln_b, alpha):
    b, t, d = x.shape
    tm = min(t, 512)
    per_row = g.shape[1] == t and t > 1
    mod = pl.BlockSpec((1, tm, d) if per_row else (1, 1, d),
                       (lambda i, j: (i, j, 0)) if per_row else (lambda i, j: (i, 0, 0)))
    row = lambda w: pl.BlockSpec((1, tm, w), lambda i, j: (i, j, 0))
    const = pl.BlockSpec((1, d), lambda i, j: (0, 0))
    return pl.pallas_call(
        functools.partial(_resln_kernel, alpha=alpha),
        grid=(b, t // tm),
        in_specs=[row(d), row(LANES)] + [row(d)] * TOP_K + [mod, const, const],
        out_specs=row(d),
        out_shape=jax.ShapeDtypeStruct((b, t, d), F32),
        compiler_params=_cparams(("parallel", "parallel")),
        name="resln",
    )(x, gate.reshape(b, t, LANES), *[p.reshape(b, t, d) for p in picked], g, ln_g, ln_b)


def _conv_mixer(zd, buf, conv_w):
    t = zd.shape[1]
    ch = GROUP_WIDTH
    b_gate, c_gate, xin = zd[..., :ch], zd[..., ch:2 * ch], zd[..., 2 * ch:3 * ch]
    ue = jnp.concatenate([buf, c_gate * xin], axis=1)
    y = sum(ue[:, j:j + t] * conv_w[j] for j in range(CONV_K))
    return b_gate * y, ue[:, t:]


def _masked_softmax(s, valid):
    p = jax.nn.softmax(jnp.where(valid, s, NEG_INF), axis=-1)
    return jnp.where(valid, p, 0.0)


def _paged_fetch(pool, page_table, new_rows):
    n_past = page_table.shape[1] * PAGE_SIZE
    n_new = new_rows.shape[1]

    def fetch(pos):
        b = pos.shape[0]
        flat = pos.reshape(b, -1)
        bi = jnp.arange(b)[:, None]
        pc = jnp.clip(flat, 0, n_past - 1)
        past = pool[page_table[bi, pc // PAGE_SIZE], pc % PAGE_SIZE]
        new = new_rows[bi, jnp.clip(flat - n_past, 0, n_new - 1)]
        is_past = (flat < n_past).reshape(flat.shape + (1,) * (past.ndim - 2))
        return jnp.where(is_past, past, new).reshape(pos.shape + past.shape[2:])
    return fetch


def _nsa_attend_sample(q, gates, qpos, kvc, fetch_slc, kvw, wpos, slopes, n_keys):
    b, tq = q.shape[:2]
    q = q * HEAD_DIM ** -0.5
    sl = jnp.asarray(slopes, F32)[:, None, None]
    nc = kvc.shape[1]
    cend = jnp.arange(nc) * NSA_CMP_BLOCK + (NSA_CMP_BLOCK - 1)
    s_c = jnp.einsum('bqhd,bnd->bhqn', q, kvc[:, :, 0]) - sl * (qpos[:, None] - cend[None, :]).astype(F32)
    p_c = _masked_softmax(s_c, cend[None, :] <= qpos[:, None])
    o_c = jnp.einsum('bhqn,bnd->bqhd', p_c, kvc[:, :, 1])
    ratio = NSA_SEL_BLOCK // NSA_CMP_BLOCK
    n_sel = -(-n_keys // NSA_SEL_BLOCK)
    imp = jnp.pad(p_c.sum(1), ((0, 0), (0, 0), (0, n_sel * ratio - nc))).reshape(b, tq, n_sel, ratio).sum(-1)
    blk = jnp.arange(n_sel)[None, :]
    cur = (qpos // NSA_SEL_BLOCK)[:, None]
    forced = (blk == 0) | (blk > cur - NSA_LOCAL_BLOCKS)
    score = jnp.where(blk > cur, NEG_INF, jnp.where(forced, FORCE_SCORE, imp))
    n_top = min(NSA_TOPN, n_sel)
    _, sel = lax.top_k(score, n_top)
    kpos = (sel[..., None] * NSA_SEL_BLOCK + jnp.arange(NSA_SEL_BLOCK)).reshape(b, tq, n_top * NSA_SEL_BLOCK)
    kv_s = fetch_slc(kpos)
    s_s = (jnp.einsum('bqhd,bqnd->bhqn', q, kv_s[..., 0, :])
           - sl * (qpos[None, None, :, None] - kpos[:, None]).astype(F32))
    p_s = _masked_softmax(s_s, (kpos <= qpos[None, :, None])[:, None])
    o_s = jnp.einsum('bhqn,bqnd->bqhd', p_s, kv_s[..., 1, :])
    dw = qpos[:, None] - wpos[None, :]
    s_w = jnp.einsum('bqhd,bnd->bhqn', q, kvw[:, :, 0]) - sl * dw.astype(F32)
    p_w = _masked_softmax(s_w, (dw >= 0) & (dw <= NSA_WINDOW) & (wpos[None, :] >= 0))
    o_w = jnp.einsum('bhqn,bnd->bqhd', p_w, kvw[:, :, 1])
    return gates[..., 0:1] * o_c + gates[..., 1:2] * o_s + gates[..., 2:3] * o_w


SAMPLE_QROWS = 8


def _dsa_sample(zc, c_dsa_kv, c_dsa_idx, page_table, slopes):
    b, t, _ = zc.shape
    assert t == 1
    gw = GROUP_WIDTH
    n_past = page_table.shape[1] * PAGE_SIZE
    n_keys = n_past + t
    pad = _pad_to(n_keys, KEY_TILE) - n_keys
    past_kv = c_dsa_kv[page_table].reshape(b, n_past, 2 * HEAD_DIM)
    kv_all = jnp.concatenate([past_kv, zc[..., gw:gw + 2 * HEAD_DIM], jnp.zeros((b, pad, 2 * HEAD_DIM), F32)], axis=1)
    o = gw + 2 * HEAD_DIM + IDX_HEADS * IDX_DIM
    past_i = c_dsa_idx[page_table].reshape(b, n_past, IDX_DIM)
    ki_all = jnp.concatenate([past_i, zc[..., o:o + IDX_DIM], jnp.zeros((b, pad, IDX_DIM), F32)], axis=1)
    ka, kip = _dsa_keyprep(kv_all, 0, ki_all, 0)
    zq = jnp.pad(zc, ((0, 0), (0, SAMPLE_QROWS - t), (0, 0)))
    return _dsa_attend(zq, ka, kip, slopes, qb=SAMPLE_QROWS, q_base=n_past, n_keys=n_keys)[:, :t]


def _kv_rows(u):
    return u.reshape(u.shape[0], u.shape[1], 2, HEAD_DIM)


def _mixers_prompt(za, zb, zc, zd, lp, sl_nsa, sl_dsa):
    b, t, _ = za.shape
    gw = GROUP_WIDTH
    ya, s_rwkv, shift = _rwkv_mixer(za, jnp.zeros((b, A_PAD), F32),
                                    jnp.zeros((b, N_HEADS, HEAD_DIM, HEAD_DIM), F32), lp)
    kv_c = _kv_rows(zb[..., gw:gw + 2 * HEAD_DIM])
    kv_s = _kv_rows(zb[..., gw + 2 * HEAD_DIM:gw + 4 * HEAD_DIM])
    kv_w = _kv_rows(zb[..., gw + 4 * HEAD_DIM:gw + 6 * HEAD_DIM])
    kvc = _nsa_compress(kv_c, lp).reshape(b, -1, 2 * HEAD_DIM)
    nc = kvc.shape[1]
    kvc = jnp.pad(kvc, ((0, 0), (0, LANES - nc), (0, 0)))
    kvc = jnp.concatenate([kvc[:, 0::2], kvc[:, 1::2]], axis=1)
    yb = _nsa_prompt(zb, kvc, sl_nsa)
    yc = _dsa_prompt(zc, sl_dsa)
    kv_d = _kv_rows(zc[..., gw:gw + 2 * HEAD_DIM])
    o = gw + 2 * HEAD_DIM + IDX_HEADS * IDX_DIM
    ki = zc[..., o:o + IDX_DIM]
    yd, conv_buf = _conv_mixer(zd, jnp.zeros((b, CONV_K - 1, gw), F32), lp['conv_w'])
    return (ya, yb, yc, yd), (s_rwkv, shift, kv_c, kv_s, kv_w[:, -min(NSA_WINDOW, t):], kv_d, ki, conv_buf)


def _mixers_sample(za, zb, zc, zd, lp, sl_nsa, sl_dsa, s_rwkv, s_shift, c_nsa_cmp, c_nsa_slc, s_win,
                   c_dsa_kv, c_dsa_idx, s_conv, page_table):
    b, t, _ = za.shape
    gw = GROUP_WIDTH
    n_past = page_table.shape[1] * PAGE_SIZE
    n_buf = s_win.shape[1]
    qpos = n_past + jnp.arange(t)
    shift_prev = jnp.pad(s_shift, ((0, 0), (0, A_PAD - A_COLS)))
    ya, s_rwkv_new, shift = _rwkv_mixer(za, shift_prev, s_rwkv, lp)
    q = zb[..., :gw].reshape(b, t, N_HEADS, HEAD_DIM)
    kv_c = _kv_rows(zb[..., gw:gw + 2 * HEAD_DIM])
    kv_s = _kv_rows(zb[..., gw + 2 * HEAD_DIM:gw + 4 * HEAD_DIM])
    kv_w = _kv_rows(zb[..., gw + 4 * HEAD_DIM:gw + 6 * HEAD_DIM])
    o = gw + 6 * HEAD_DIM
    gates = jax.nn.sigmoid(zb[..., o:o + 3 * N_HEADS]).reshape(b, t, N_HEADS, 3)
    past_c = c_nsa_cmp[page_table].reshape(b, n_past, 2, HEAD_DIM)
    kvc = _nsa_compress(jnp.concatenate([past_c, kv_c], axis=1), lp)
    kvw = jnp.concatenate([s_win, kv_w], axis=1)
    wpos = n_past - n_buf + jnp.arange(n_buf + t)
    yb = _nsa_attend_sample(q, gates, qpos, kvc, _paged_fetch(c_nsa_slc, page_table, kv_s), kvw, wpos,
                            sl_nsa, n_past + t)
    kv_d = _kv_rows(zc[..., gw:gw + 2 * HEAD_DIM])
    o = gw + 2 * HEAD_DIM + IDX_HEADS * IDX_DIM
    ki = zc[..., o:o + IDX_DIM]
    yc = _dsa_sample(zc, c_dsa_kv, c_dsa_idx, page_table, sl_dsa)
    yd, conv_buf = _conv_mixer(zd, s_conv, lp['conv_w'])
    flat = lambda u: u.reshape(b, t, gw)
    return (ya, flat(yb), flat(yc), yd), (s_rwkv_new, shift, kv_c, kv_s, kvw[:, -n_buf:], kv_d, ki, conv_buf)


def _layer(x, mod, lp, pw, mixer_fn, alpha):
    b, t, d = x.shape
    sh1, sc1, g1, sh2, sc2, g2 = mod
    zs = _inproj(x, sc1, sh1, pw['w_in'])
    ys, states = mixer_fn(*zs)
    x1, h2, logits = _outproj(ys, x, g1, sc2, sh2, pw['w_out'], pw['ln1_g'], pw['ln1_b'],
                              pw['wr'], pw['br'], alpha)
    n_exp = lp['moe_wr'].shape[1]
    blk = _moe_block_rows(b * t * TOP_K, n_exp)
    picked, gate = _moe(h2.reshape(b * t, d), logits.reshape(b * t, LANES)[:, :n_exp], pw['moe'], blk)
    x2 = _resln(x1, picked, gate, g2, pw['ln2_g'], pw['ln2_b'], alpha)
    return x2, states


def _prep_weights(lp):
    d = lp['w_in'].shape[0]
    cols, o = [], 0
    for c, p in ((A_COLS, A_PAD), (B_COLS, B_PAD), (C_COLS, C_PAD), (D_COLS, D_PAD)):
        cols.append(jnp.pad(lp['w_in'][:, o:o + c], ((0, 0), (0, p - c))))
        o += c
    n_exp = lp['moe_wr'].shape[1]
    row = lambda u: u.reshape(1, -1)
    return {
        'w_in': jnp.concatenate(cols, axis=1).astype(BF16),
        'w_out': lp['w_out'].reshape(4, GROUP_WIDTH, d).astype(BF16),
        'ln1_g': row(lp['ln1_g']), 'ln1_b': row(lp['ln1_b']),
        'ln2_g': row(lp['ln2_g']), 'ln2_b': row(lp['ln2_b']),
        'wr': jnp.pad(lp['moe_wr'], ((0, 0), (0, LANES - n_exp))),
        'br': jnp.pad(lp['moe_br'], (0, LANES - n_exp)).reshape(1, LANES),
        'moe': (lp['moe_wgu'], lp['moe_bgu'], lp['moe_wd'], lp['moe_bd']),
    }


def kernel(x_prompt, x_sample, state_rwkv, state_rwkv_shift, cache_nsa_cmp, cache_nsa_slc, state_nsa_win, cache_dsa_kv, cache_dsa_idx, state_conv, page_table, c_prompt, c_sample, w_ada, b_ada, w_in, w_out, ln1_g, ln1_b, ln2_g, ln2_b, rwkv_mu, rwkv_w0, rwkv_w2, rwkv_a0, rwkv_a2, rwkv_g2, rwkv_kk, rwkv_ka, rwkv_rk, rwkv_gn_g, rwkv_gn_b, nsa_pe, nsa_w1, nsa_w2, conv_w, moe_wr, moe_br, moe_wgu, moe_bgu, moe_wd, moe_bd):
    depth = w_in.shape[0]
    alpha = (2 * depth) ** 0.25
    sl_nsa, sl_dsa = _alibi_slopes()
    bp = x_prompt.shape[0]
    hp, hs = x_prompt, x_sample
    c_all = jnp.concatenate([c_prompt, c_sample], axis=0)
    states_p, states_s = [], []
    for l in range(depth):
        lp = {
            'w_in': w_in[l], 'w_out': w_out[l], 'ln1_g': ln1_g[l], 'ln1_b': ln1_b[l],
            'ln2_g': ln2_g[l], 'ln2_b': ln2_b[l],
            'rwkv_mu': jnp.pad(rwkv_mu[l], (0, A_PAD - A_COLS)), 'rwkv_w0': rwkv_w0[l], 'rwkv_w2': rwkv_w2[l],
            'rwkv_a0': rwkv_a0[l], 'rwkv_a2': rwkv_a2[l], 'rwkv_g2': rwkv_g2[l],
            'rwkv_kk': rwkv_kk[l], 'rwkv_ka': rwkv_ka[l], 'rwkv_rk': rwkv_rk[l],
            'rwkv_gn_g': rwkv_gn_g[l], 'rwkv_gn_b': rwkv_gn_b[l],
            'nsa_pe': nsa_pe[l], 'nsa_w1': nsa_w1[l], 'nsa_w2': nsa_w2[l], 'conv_w': conv_w[l],
            'moe_wr': moe_wr[l], 'moe_br': moe_br[l], 'moe_wgu': moe_wgu[l], 'moe_bgu': moe_bgu[l],
            'moe_wd': moe_wd[l], 'moe_bd': moe_bd[l],
        }
        pw = _prep_weights(lp)
        mod_all = jax.nn.silu(c_all) @ w_ada[l] + b_ada[l]
        mod_p = [u[:, None, :] for u in jnp.split(mod_all[:bp], 6, axis=-1)]
        mod_s = [u[:, None, :] for u in jnp.split(mod_all[bp:], 6, axis=-1)]
        mix_p = functools.partial(_mixers_prompt, lp=lp, sl_nsa=sl_nsa, sl_dsa=sl_dsa)
        mix_s = functools.partial(
            _mixers_sample, lp=lp, sl_nsa=sl_nsa, sl_dsa=sl_dsa, s_rwkv=state_rwkv[l],
            s_shift=state_rwkv_shift[l], c_nsa_cmp=cache_nsa_cmp[l], c_nsa_slc=cache_nsa_slc[l],
            s_win=state_nsa_win[l], c_dsa_kv=cache_dsa_kv[l], c_dsa_idx=cache_dsa_idx[l],
            s_conv=state_conv[l], page_table=page_table)
        hp, st_p = _layer(hp, mod_p, lp, pw, mix_p, alpha)
        hs, st_s = _layer(hs, mod_s, lp, pw, mix_s, alpha)
        states_p.append(st_p)
        states_s.append(st_s)
    outs_p = [jnp.stack(v) for v in zip(*states_p)]
    outs_s = [jnp.stack(v) for v in zip(*states_s)]
    return (hp, hs, *outs_p, *outs_s)
```

```python
import functools

import jax
import jax.numpy as jnp
import numpy as np
from jax import lax
from jax.experimental import pallas as pl
from jax.experimental.pallas import tpu as pltpu

F32 = jnp.float32
BF16 = jnp.bfloat16

HEAD_DIM = 64
N_HEADS = 4
GROUP_WIDTH = N_HEADS * HEAD_DIM
RWKV_DECAY_SCALE = 0.606531
RWKV_GN_EPS = 64e-5
NSA_CMP_BLOCK = 32
NSA_SEL_BLOCK = 64
NSA_TOPN = 16
NSA_LOCAL_BLOCKS = 2
NSA_WINDOW = 512
DSA_TOPK = 256
IDX_HEADS = 4
IDX_DIM = 32
CONV_K = 3
TOP_K = 4
SWIGLU_LIMIT = 7.0
SWIGLU_ALPHA = 1.702
NEG_INF = -1e30
FORCE_SCORE = 1e4
Q_BLOCK = 128
PAGE_SIZE = 128

LANES = 128
A_COLS = 3 * GROUP_WIDTH + 32 + 32 + 64
B_COLS = GROUP_WIDTH + 6 * HEAD_DIM + 3 * N_HEADS
C_COLS = GROUP_WIDTH + 2 * HEAD_DIM + IDX_HEADS * IDX_DIM + IDX_DIM + IDX_HEADS
D_COLS = 3 * GROUP_WIDTH


def _pad_to(n, m):
    return -(-n // m) * m


A_PAD, B_PAD, C_PAD, D_PAD = (_pad_to(c, LANES) for c in (A_COLS, B_COLS, C_COLS, D_COLS))
VMEM_LIMIT = 56 * 1024 * 1024

_NT = (((1,), (1,)), ((), ()))


def _alibi_slopes():
    n = 2 * N_HEADS
    s = [2.0 ** (-8.0 * i / n) for i in range(1, n + 1)]
    return s[0::2], s[1::2]


def _ln_rows(x, eps=1e-5):
    mu = jnp.mean(x, axis=-1, keepdims=True)
    xc = x - mu
    var = jnp.mean(xc * xc, axis=-1, keepdims=True)
    return xc * lax.rsqrt(var + eps)


def _cparams(sem):
    return pltpu.CompilerParams(dimension_semantics=sem, vmem_limit_bytes=VMEM_LIMIT)


def _inproj_kernel(x_ref, sc_ref, sh_ref, w_ref, za_ref, zb_ref, zc_ref, zd_ref):
    h = _ln_rows(x_ref[0]) * (1.0 + sc_ref[0]) + sh_ref[0]
    z = jnp.dot(h.astype(BF16), w_ref[...], preferred_element_type=F32)
    o = 0
    for ref, w in ((za_ref, A_PAD), (zb_ref, B_PAD), (zc_ref, C_PAD), (zd_ref, D_PAD)):
        ref[0] = z[:, o:o + w]
        o += w


def _inproj(x, sc, sh, w_pad):
    b, t, d = x.shape
    tm = min(t, 512)
    mod_rows = sc.shape[1]
    mod_blk = (1, tm, d) if mod_rows == t and t > 1 else (1, 1, d)
    mod_map = (lambda i, j: (i, j, 0)) if mod_rows == t and t > 1 else (lambda i, j: (i, 0, 0))
    widths = (A_PAD, B_PAD, C_PAD, D_PAD)
    return pl.pallas_call(
        _inproj_kernel,
        grid=(b, t // tm),
        in_specs=[pl.BlockSpec((1, tm, d), lambda i, j: (i, j, 0)),
                  pl.BlockSpec(mod_blk, mod_map),
                  pl.BlockSpec(mod_blk, mod_map),
                  pl.BlockSpec(w_pad.shape, lambda i, j: (0, 0))],
        out_specs=[pl.BlockSpec((1, tm, w), lambda i, j: (i, j, 0)) for w in widths],
        out_shape=[jax.ShapeDtypeStruct((b, t, w), F32) for w in widths],
        compiler_params=_cparams(("parallel", "parallel")),
        name="inproj",
    )(x, sc, sh, w_pad)


def _rwkv_scan_kernel(r_ref, w_ref, k_ref, v_ref, kk_ref, kka_ref, s0_ref, y_ref, sfin_ref,
                      s_scr, y_scr, *, bg, tc, unroll):
    c = pl.program_id(1)
    rows = bg * HEAD_DIM

    @pl.when(c == 0)
    def _():
        s_scr[...] = s0_ref[...].reshape(rows, GROUP_WIDTH)

    li = lax.broadcasted_iota(jnp.int32, (2 * GROUP_WIDTH, GROUP_WIDTH), 0)
    lj = lax.broadcasted_iota(jnp.int32, (2 * GROUP_WIDTH, GROUP_WIDTH), 1)
    seg_ones2 = jnp.where((li % GROUP_WIDTH) // HEAD_DIM == lj // HEAD_DIM, 1.0, 0.0).astype(BF16)
    seg_ones1 = seg_ones2[:GROUP_WIDTH]
    lane = lax.broadcasted_iota(jnp.int32, (rows, GROUP_WIDTH), 1)
    row = lax.broadcasted_iota(jnp.int32, (rows, GROUP_WIDTH), 0)
    diag = jnp.where(row % HEAD_DIM == lane % HEAD_DIM, 1.0, 0.0)

    def seg_sum(x):
        hi = x.astype(BF16)
        lo = (x - hi.astype(F32)).astype(BF16)
        return jnp.dot(jnp.concatenate([hi, lo], axis=1), seg_ones2, preferred_element_type=F32)

    def bcast_rows(ref, t):
        return jnp.concatenate(
            [jnp.broadcast_to(ref[b, pl.ds(t, 1), :], (HEAD_DIM, GROUP_WIDTH)) for b in range(bg)], axis=0)

    def step(t):
        s = s_scr[...]
        kk = bcast_rows(kk_ref, t)
        sa = seg_sum(s * kk)
        vcol = seg_sum(bcast_rows(v_ref, t) * diag)
        s = s * bcast_rows(w_ref, t) - sa * bcast_rows(kka_ref, t) + vcol * bcast_rows(k_ref, t)
        s_scr[...] = s
        y = jnp.dot((s * bcast_rows(r_ref, t)).astype(BF16), seg_ones1, preferred_element_type=F32)
        y_scr[...] = jnp.where(lane % HEAD_DIM == t, y, y_scr[...])

    def outer(i, carry):
        for u in range(unroll):
            step(i * unroll + u)
        return carry

    y_scr[...] = jnp.zeros(y_scr.shape, F32)
    lax.fori_loop(0, tc // unroll, outer, 0)
    y_ref[...] = y_scr[...].reshape(bg, 1, HEAD_DIM, GROUP_WIDTH)

    @pl.when(c == pl.num_programs(1) - 1)
    def _():
        sfin_ref[...] = s_scr[...].reshape(bg, HEAD_DIM, GROUP_WIDTH)


def _rwkv_scan(r, w, k, v, kk, kka, s0):
    b, t, gw = r.shape
    bg = 8 if b % 8 == 0 else b
    tc = min(t, HEAD_DIM)
    unroll = 4 if tc % 4 == 0 else 1
    nc = t // tc
    s0l = s0.transpose(0, 2, 1, 3).reshape(b, HEAD_DIM, gw)
    seq = pl.BlockSpec((bg, tc, gw), lambda i, c: (i, c, 0))
    st = pl.BlockSpec((bg, HEAD_DIM, gw), lambda i, c: (i, 0, 0))
    y_raw, s_fin = pl.pallas_call(
        functools.partial(_rwkv_scan_kernel, bg=bg, tc=tc, unroll=unroll),
        grid=(b // bg, nc),
        in_specs=[seq] * 6 + [st],
        out_specs=[pl.BlockSpec((bg, 1, HEAD_DIM, gw), lambda i, c: (i, c, 0, 0)), st],
        out_shape=[jax.ShapeDtypeStruct((b, nc, HEAD_DIM, gw), F32),
                   jax.ShapeDtypeStruct((b, HEAD_DIM, gw), F32)],
        scratch_shapes=[pltpu.VMEM((bg * HEAD_DIM, gw), F32), pltpu.VMEM((bg * HEAD_DIM, gw), F32)],
        compiler_params=_cparams(("parallel", "arbitrary")),
        name="rwkv_scan",
    )(r, w, k, v, kk, kka, s0l)
    y = y_raw.reshape(b, nc, HEAD_DIM, N_HEADS, HEAD_DIM)[..., :tc]
    y = y.transpose(0, 1, 4, 3, 2).reshape(b, t, N_HEADS, HEAD_DIM)
    s_fin = s_fin.reshape(b, HEAD_DIM, N_HEADS, HEAD_DIM).transpose(0, 2, 1, 3)
    return y, s_fin


def _rwkv_mixer(za, shift_prev, s0, lp):
    b, t, _ = za.shape
    z_prev = jnp.concatenate([shift_prev[:, None], za[:, :-1]], axis=1)
    zs = za + lp['rwkv_mu'] * (z_prev - za)
    gw = GROUP_WIDTH
    r, k, v = zs[..., :gw], zs[..., gw:2 * gw], zs[..., 2 * gw:3 * gw]
    o = 3 * gw
    w_lo, a_lo, g_lo = zs[..., o:o + 32], zs[..., o + 32:o + 64], zs[..., o + 64:o + 128]
    decay = jnp.exp(-RWKV_DECAY_SCALE * jax.nn.sigmoid(lp['rwkv_w0'] + jnp.tanh(w_lo) @ lp['rwkv_w2']))
    a = jax.nn.sigmoid(lp['rwkv_a0'] + a_lo @ lp['rwkv_a2'])
    g = jax.nn.sigmoid(g_lo) @ lp['rwkv_g2']
    heads = lambda u: u.reshape(b, t, N_HEADS, HEAD_DIM)
    kk = heads(k) * lp['rwkv_kk']
    kk = kk / jnp.maximum(jnp.sqrt(jnp.sum(kk * kk, axis=-1, keepdims=True)), 1e-12)
    kk = kk.reshape(b, t, gw)
    k = k * (1.0 + (a - 1.0) * lp['rwkv_ka'].reshape(gw))
    y, s_fin = _rwkv_scan(r, decay, k, v, kk, kk * a, s0)
    mu = jnp.mean(y, axis=-1, keepdims=True)
    var = jnp.mean(jnp.square(y - mu), axis=-1, keepdims=True)
    yn = ((y - mu) * lax.rsqrt(var + RWKV_GN_EPS) * lp['rwkv_gn_g'].reshape(N_HEADS, HEAD_DIM)
          + lp['rwkv_gn_b'].reshape(N_HEADS, HEAD_DIM))
    bonus = jnp.sum(heads(r) * heads(k) * lp['rwkv_rk'], axis=-1, keepdims=True) * heads(v)
    out = (yn + bonus).reshape(b, t, gw) * g
    return out, s_fin, za[:, -1, :A_COLS]


KEY_TILE = 512
POS_SPLIT = 128
M_FLOOR = -1e29


def _aug_keys(kv, pos):
    lane = lax.broadcasted_iota(jnp.int32, kv.shape, 1)
    hi = (pos // POS_SPLIT).astype(F32)
    lo = (pos % POS_SPLIT).astype(F32)
    extra = jnp.where(lane == 0, hi, jnp.where(lane == 1, lo, 0.0))
    return jnp.concatenate([kv.astype(BF16), extra.astype(BF16)], axis=1)


def _aug_queries(q2, slopes, lane):
    rows = []
    for h in range(N_HEADS):
        pair = q2[:, (h // 2) * LANES:(h // 2 + 1) * LANES]
        if h % 2:
            pair = pltpu.roll(pair, HEAD_DIM, 1)
        extra = jnp.where(lane == 0, POS_SPLIT * slopes[h], jnp.where(lane == 1, slopes[h], 0.0))
        rows.append(jnp.concatenate([jnp.where(lane < HEAD_DIM, pair, 0.0), extra], axis=1))
    return jnp.concatenate(rows, axis=0).astype(BF16)


def _tile_heads(x):
    return jnp.concatenate([x] * N_HEADS, axis=0)


def _softmax_block(qa, ka, madd):
    s = lax.dot_general(qa, ka, _NT, preferred_element_type=F32) + _tile_heads(madd)
    m = jnp.maximum(jnp.max(s, axis=1, keepdims=True), M_FLOOR)
    e = jnp.exp(s - m)
    inv = 1.0 / jnp.maximum(jnp.sum(e, axis=1, keepdims=True), 1e-30)
    o = jnp.dot(e.astype(BF16), ka, preferred_element_type=F32) * inv
    return e, inv, o


def _flash_step(qa, ka, madd, m, l, acc):
    s = lax.dot_general(qa, ka, _NT, preferred_element_type=F32) + _tile_heads(madd)
    m_new = jnp.maximum(m, jnp.max(s, axis=1, keepdims=True))
    alpha = jnp.exp(m - m_new)
    p = jnp.exp(s - m_new)
    l = alpha * l + jnp.sum(p, axis=1, keepdims=True)
    acc = alpha * acc + jnp.dot(p.astype(BF16), ka, preferred_element_type=F32)
    return m_new, l, acc


def _flash_init(qb=Q_BLOCK):
    rows = N_HEADS * qb
    return (jnp.full((rows, 1), M_FLOOR, F32), jnp.zeros((rows, 1), F32), jnp.zeros((rows, 2 * LANES), F32))


def _heads_out(per_head, lane):
    pair = lambda a, b: jnp.where(lane < HEAD_DIM, pltpu.roll(a, HEAD_DIM, 1), b)
    return jnp.concatenate([pair(per_head[0], per_head[1]), pair(per_head[2], per_head[3])], axis=1)


def _head_rows(o, h, qb=Q_BLOCK):
    return o[h * qb:(h + 1) * qb, :LANES]


def _nsa_kernel(q_ref, ksv_ref, kwv_ref, gl_ref, kvc_ref, exp_ref, o_ref, ks_scr, kw_scr,
                *, t, n_win, n_top, slopes):
    qb = Q_BLOCK
    j = pl.program_id(1)
    q0 = j * qb

    @pl.when(j == 0)
    def _():
        for c in range(t // KEY_TILE):
            rows = slice(c * KEY_TILE, (c + 1) * KEY_TILE)
            pos = c * KEY_TILE + lax.broadcasted_iota(jnp.int32, (KEY_TILE, 1), 0)
            ks_scr[rows, :] = _aug_keys(ksv_ref[0, rows, :], pos)
            kw_scr[rows, :] = _aug_keys(kwv_ref[0, rows, :], pos)

    lane = lax.broadcasted_iota(jnp.int32, (qb, LANES), 1)
    qpos = q0 + lax.broadcasted_iota(jnp.int32, (qb, 1), 0)
    qa = _aug_queries(q_ref[0] * (HEAD_DIM ** -0.5), slopes, lane)
    gates = jax.nn.sigmoid(gl_ref[0])

    perm_end = lambda i: (2 * (i % HEAD_DIM) + i // HEAD_DIM) * NSA_CMP_BLOCK + (NSA_CMP_BLOCK - 1)
    kca = _aug_keys(kvc_ref[0], perm_end(lax.broadcasted_iota(jnp.int32, (LANES, 1), 0)))
    cend = perm_end(lax.broadcasted_iota(jnp.int32, (1, LANES), 1))
    e, inv, o_c = _softmax_block(qa, kca, jnp.where(cend <= qpos, 0.0, NEG_INF))
    p_c = e * inv
    imp = _head_rows(p_c, 0)
    for h in range(1, N_HEADS):
        imp = imp + _head_rows(p_c, h)
    imp = imp + pltpu.roll(imp, HEAD_DIM, 1)

    cur = qpos // NSA_SEL_BLOCK
    forced = (lane == 0) | (lane > cur - NSA_LOCAL_BLOCKS)
    score = jnp.where(lane > cur, NEG_INF, jnp.where(forced, FORCE_SCORE, imp))
    score = jnp.where(lane < HEAD_DIM, score, -3e38)
    rank = jnp.zeros((qb, LANES), F32)
    for i in range(NSA_SEL_BLOCK):
        col = score[:, i:i + 1]
        beats = (col > score) | ((col == score) & (lane > i))
        rank = rank + jnp.where(beats, 1.0, 0.0)
    sel = jnp.where((rank < n_top) & (lane < HEAD_DIM), 1.0, 0.0).astype(BF16)

    def sel_tile(kt, carry):
        k0 = pl.multiple_of(kt * KEY_TILE, KEY_TILE)
        kpos = k0 + lax.broadcasted_iota(jnp.int32, (1, KEY_TILE), 1)
        member = jnp.dot(sel, exp_ref[kt], preferred_element_type=F32) > 0.5
        madd = jnp.where(member & (kpos <= qpos), 0.0, NEG_INF)
        return _flash_step(qa, ks_scr[pl.ds(k0, KEY_TILE), :], madd, *carry)

    n_tiles = (q0 + qb - 1) // KEY_TILE + 1
    _, l_s, acc_s = lax.fori_loop(0, n_tiles, sel_tile, _flash_init())
    o_s = acc_s * (1.0 / jnp.maximum(l_s, 1e-30))

    kst = pl.multiple_of(jnp.maximum(q0 + qb - n_win, 0), qb)
    dw = qpos - (kst + lax.broadcasted_iota(jnp.int32, (1, n_win), 1))
    wadd = jnp.where((dw >= 0) & (dw <= NSA_WINDOW), 0.0, NEG_INF)
    _, _, o_w = _softmax_block(qa, kw_scr[pl.ds(kst, n_win), :], wadd)

    outs = []
    for h in range(N_HEADS):
        g = [gates[:, 3 * h + i:3 * h + i + 1] for i in range(3)]
        outs.append(g[0] * _head_rows(o_c, h) + g[1] * _head_rows(o_s, h) + g[2] * _head_rows(o_w, h))
    o_ref[0] = _heads_out(outs, lane)


def _nsa_prompt(zb, kvc_perm, slopes):
    b, t, _ = zb.shape
    assert t % KEY_TILE == 0 and t // NSA_CMP_BLOCK <= LANES and t <= 256 * POS_SPLIT
    n_win = min(NSA_WINDOW + Q_BLOCK, t)
    n_sel = -(-t // NSA_SEL_BLOCK)
    n_kt = t // KEY_TILE
    key_blk = (np.arange(t) // NSA_SEL_BLOCK).reshape(n_kt, 1, KEY_TILE)
    expand = jnp.asarray((np.arange(LANES)[None, :, None] == key_blk).astype(np.float32), BF16)
    full = lambda c: pl.BlockSpec((1, t, LANES), lambda i, j, c=c: (i, 0, c))
    return pl.pallas_call(
        functools.partial(_nsa_kernel, t=t, n_win=n_win, n_top=min(NSA_TOPN, n_sel), slopes=slopes),
        grid=(b, t // Q_BLOCK),
        in_specs=[pl.BlockSpec((1, Q_BLOCK, 2 * LANES), lambda i, j: (i, j, 0)),
                  full(3), full(4),
                  pl.BlockSpec((1, Q_BLOCK, LANES), lambda i, j: (i, j, 5)),
                  pl.BlockSpec((1, LANES, LANES), lambda i, j: (i, 0, 0)),
                  pl.BlockSpec((n_kt, LANES, KEY_TILE), lambda i, j: (0, 0, 0))],
        out_specs=pl.BlockSpec((1, Q_BLOCK, GROUP_WIDTH), lambda i, j: (i, j, 0)),
        out_shape=jax.ShapeDtypeStruct((b, t, GROUP_WIDTH), F32),
        scratch_shapes=[pltpu.VMEM((t, 2 * LANES), BF16), pltpu.VMEM((t, 2 * LANES), BF16)],
        compiler_params=_cparams(("parallel", "arbitrary")),
        name="nsa_prompt",
    )(zb, zb, zb, zb, kvc_perm, expand)


def _nsa_compress(rows, lp):
    b, l = rows.shape[:2]
    nc = l // NSA_CMP_BLOCK
    blk = rows[:, :nc * NSA_CMP_BLOCK].reshape(b, nc, NSA_CMP_BLOCK, 2, HEAD_DIM) + lp['nsa_pe']
    flat = jnp.swapaxes(blk, 2, 3).reshape(b, nc, 2, NSA_CMP_BLOCK * HEAD_DIM)
    hid = jax.nn.gelu(jnp.einsum('bnci,cih->bnch', flat, lp['nsa_w1']))
    return jnp.einsum('bnch,chd->bncd', hid, lp['nsa_w2'])


def _dsa_keyprep_kernel(kv_ref, ki_ref, ka_ref, kip_ref):
    pos = pl.program_id(1) * KEY_TILE + lax.broadcasted_iota(jnp.int32, (KEY_TILE, 1), 0)
    ka_ref[0] = _aug_keys(kv_ref[0], pos)
    ki = ki_ref[0]
    if ki.shape[1] == LANES:
        ki = jnp.where(lax.broadcasted_iota(jnp.int32, ki.shape, 1) < IDX_DIM, ki, 0.0)
    else:
        ki = jnp.concatenate([ki, jnp.zeros((KEY_TILE, LANES - IDX_DIM), F32)], axis=1)
    hi = ki.astype(BF16).astype(F32)
    kip_ref[0] = (hi + pltpu.roll(ki - hi, IDX_DIM, 1) + pltpu.roll(hi, 2 * IDX_DIM, 1)).astype(BF16)


def _dsa_keyprep(kv_src, kv_col, ki_src, ki_col):
    b, t, _ = kv_src.shape
    ki_w = min(ki_src.shape[2], LANES)
    return pl.pallas_call(
        _dsa_keyprep_kernel,
        grid=(b, t // KEY_TILE),
        in_specs=[pl.BlockSpec((1, KEY_TILE, LANES), lambda i, c: (i, c, kv_col)),
                  pl.BlockSpec((1, KEY_TILE, ki_w), lambda i, c: (i, c, ki_col))],
        out_specs=[pl.BlockSpec((1, KEY_TILE, 2 * LANES), lambda i, c: (i, c, 0)),
                   pl.BlockSpec((1, KEY_TILE, LANES), lambda i, c: (i, c, 0))],
        out_shape=[jax.ShapeDtypeStruct((b, t, 2 * LANES), BF16), jax.ShapeDtypeStruct((b, t, LANES), BF16)],
        compiler_params=_cparams(("parallel", "parallel")),
        name="dsa_keyprep",
    )(kv_src, ki_src)


def _dsa_kernel(q_ref, ka_ref, qi_ref, kip_ref, wiq_ref, o_ref, skey_scr, *, t, qb, q_base, k_sel, slopes):
    n_sub = KEY_TILE // LANES
    ka_scr, kip_scr = ka_ref.at[0], kip_ref.at[0]
    lane = lax.broadcasted_iota(jnp.int32, (qb, LANES), 1)
    if q_base is None:
        q0 = pl.program_id(1) * qb
        qpos = q0 + lax.broadcasted_iota(jnp.int32, (qb, 1), 0)
        n_tiles = (q0 + qb - 1) // KEY_TILE + 1
    else:
        qpos = jnp.full((qb, 1), q_base, jnp.int32)
        n_tiles = t // KEY_TILE
    tile_pos = lambda kt: kt * KEY_TILE + lax.broadcasted_iota(jnp.int32, (1, KEY_TILE), 1)

    qi = qi_ref[0]
    wq = wiq_ref[0]
    packed_q = []
    for h in range(IDX_HEADS):
        qh = qi if h == 0 else pltpu.roll(qi, LANES - IDX_DIM * h, 1)
        qh = jnp.where(lane < IDX_DIM, qh, 0.0)
        hi = qh.astype(BF16).astype(F32)
        packed_q.append(hi + pltpu.roll(hi, IDX_DIM, 1) + pltpu.roll(qh - hi, 2 * IDX_DIM, 1))
    packed_q = jnp.concatenate(packed_q, axis=0).astype(BF16)
    w_cols = [wq[:, IDX_DIM + h:IDX_DIM + h + 1] for h in range(IDX_HEADS)]

    def idx_tile(kt, carry):
        k0 = pl.multiple_of(kt * KEY_TILE, KEY_TILE)
        rel = jnp.maximum(lax.dot_general(packed_q, kip_scr[pl.ds(k0, KEY_TILE), :], _NT,
                                          preferred_element_type=F32), 0.0)
        score = w_cols[0] * rel[:qb]
        for h in range(1, IDX_HEADS):
            score = score + w_cols[h] * rel[h * qb:(h + 1) * qb]
        score = jnp.where(score == 0.0, 0.0, score)
        score = jnp.where(tile_pos(kt) <= qpos, score, NEG_INF)
        bits = pltpu.bitcast(score, jnp.int32)
        skey_scr[kt] = bits ^ ((bits >> 31) & 0x7FFFFFFF)
        return carry

    lax.fori_loop(0, n_tiles, idx_tile, 0)

    kf = float(k_sel)

    def count(c, strict):
        cb = jnp.broadcast_to(c, (qb, LANES))

        def body(kt, acc):
            x = skey_scr[kt]
            for u in range(n_sub):
                xu = x[:, u * LANES:(u + 1) * LANES]
                acc = acc + jnp.where((xu > cb) if strict else (xu >= cb), 1.0, 0.0)
            return acc

        acc = lax.fori_loop(0, n_tiles, body, jnp.zeros((qb, LANES), F32))
        return jnp.sum(acc, axis=1, keepdims=True)

    int_min = jnp.int32(-2 ** 31)
    tau = jnp.where(count(jnp.zeros((qb, 1), jnp.int32), False) >= kf, jnp.int32(0), int_min)

    def bit_step(i, tau):
        cand = tau + lax.shift_left(jnp.int32(1), 30 - i)
        return jnp.where(count(cand, False) >= kf, cand, tau)

    tau = lax.fori_loop(0, 31, bit_step, tau)
    need = kf - count(tau, True)
    taub = jnp.broadcast_to(tau, (qb, LANES))
    ui = lax.broadcasted_iota(jnp.int32, (LANES, LANES), 0)
    uj = lax.broadcasted_iota(jnp.int32, (LANES, LANES), 1)
    upper = jnp.where(ui <= uj, 1.0, 0.0).astype(BF16)
    qa = _aug_queries(q_ref[0] * (HEAD_DIM ** -0.5), slopes, lane)

    def att_tile(kt, carry):
        ties, m, l, acc = carry
        k0 = pl.multiple_of(kt * KEY_TILE, KEY_TILE)
        x = skey_scr[kt]
        parts = []
        for u in range(n_sub):
            xu = x[:, u * LANES:(u + 1) * LANES]
            eq = xu == taub
            eqf = jnp.where(eq, 1.0, 0.0)
            before = jnp.dot(eqf.astype(BF16), upper, preferred_element_type=F32) + ties
            parts.append(jnp.where((xu > taub) | (eq & (before <= need)), 0.0, NEG_INF))
            ties = ties + jnp.sum(eqf, axis=1, keepdims=True)
        madd = jnp.where(tile_pos(kt) <= qpos, jnp.concatenate(parts, axis=1), NEG_INF)
        return (ties,) + _flash_step(qa, ka_scr[pl.ds(k0, KEY_TILE), :], madd, m, l, acc)

    _, _, l, acc = lax.fori_loop(0, n_tiles, att_tile, (jnp.zeros((qb, 1), F32),) + _flash_init(qb))
    o = acc * (1.0 / jnp.maximum(l, 1e-30))
    o_ref[0] = _heads_out([_head_rows(o, h, qb) for h in range(N_HEADS)], lane)


def _dsa_attend(zq, ka, kip, slopes, *, qb, q_base, n_keys):
    b, tq, _ = zq.shape
    t = ka.shape[1]
    assert t % KEY_TILE == 0 and t <= 256 * POS_SPLIT and tq % qb == 0
    k_sel = min(DSA_TOPK, n_keys // 4)
    assert k_sel <= KEY_TILE
    full = lambda w: pl.BlockSpec((1, t, w), lambda i, j: (i, 0, 0))
    qblk = lambda c: pl.BlockSpec((1, qb, LANES), lambda i, j, c=c: (i, j, c))
    return pl.pallas_call(
        functools.partial(_dsa_kernel, t=t, qb=qb, q_base=q_base, k_sel=k_sel, slopes=slopes),
        grid=(b, tq // qb),
        in_specs=[pl.BlockSpec((1, qb, 2 * LANES), lambda i, j: (i, j, 0)),
                  full(2 * LANES), qblk(3), full(LANES), qblk(4)],
        out_specs=pl.BlockSpec((1, qb, GROUP_WIDTH), lambda i, j: (i, j, 0)),
        out_shape=jax.ShapeDtypeStruct((b, tq, GROUP_WIDTH), F32),
        scratch_shapes=[pltpu.VMEM((t // KEY_TILE, qb, KEY_TILE), jnp.int32)],
        compiler_params=_cparams(("parallel", "parallel")),
        name="dsa_attend",
    )(zq, ka, zq, kip, zq)


def _dsa_prompt(zc, slopes):
    ka, kip = _dsa_keyprep(zc, 2, zc, 4)
    return _dsa_attend(zc, ka, kip, slopes, qb=Q_BLOCK, q_base=None, n_keys=zc.shape[1])


def _outproj_kernel(ya_ref, yb_ref, yc_ref, yd_ref, x_ref, g1_ref, sc2_ref, sh2_ref, wo_ref, lg_ref, lb_ref,
                    wr_ref, br_ref, x1_ref, h2_ref, logit_ref, *, alpha):
    mix = jnp.zeros(x_ref.shape[1:], F32)
    for i, ref in enumerate((ya_ref, yb_ref, yc_ref, yd_ref)):
        mix = mix + jnp.dot(ref[0].astype(BF16), wo_ref[i], preferred_element_type=F32)
    x1 = _ln_rows(alpha * x_ref[0] + g1_ref[0] * mix) * lg_ref[...] + lb_ref[...]
    x1_ref[0] = x1
    h2 = _ln_rows(x1) * (1.0 + sc2_ref[0]) + sh2_ref[0]
    h2_ref[0] = h2.astype(BF16)
    logit_ref[0] = jnp.dot(h2, wr_ref[...], preferred_element_type=F32,
                           precision=lax.Precision.HIGHEST) + br_ref[...]


def _outproj(ys, x, g1, sc2, sh2, wo, ln_g, ln_b, wr_pad, br_pad, alpha):
    b, t, d = x.shape
    tm = min(t, 512)
    per_row = g1.shape[1] == t and t > 1
    mod_blk = (1, tm, d) if per_row else (1, 1, d)
    mod_map = (lambda i, j: (i, j, 0)) if per_row else (lambda i, j: (i, 0, 0))
    mod = pl.BlockSpec(mod_blk, mod_map)
    row = lambda w: pl.BlockSpec((1, tm, w), lambda i, j: (i, j, 0))
    const = lambda a: pl.BlockSpec(a.shape, lambda i, j: (0,) * a.ndim)
    return pl.pallas_call(
        functools.partial(_outproj_kernel, alpha=alpha),
        grid=(b, t // tm),
        in_specs=[row(GROUP_WIDTH)] * 4 + [row(d), mod, mod, mod, const(wo), const(ln_g), const(ln_b),
                                           const(wr_pad), const(br_pad)],
        out_specs=[row(d), row(d), row(LANES)],
        out_shape=[jax.ShapeDtypeStruct((b, t, d), F32), jax.ShapeDtypeStruct((b, t, d), BF16),
                   jax.ShapeDtypeStruct((b, t, LANES), F32)],
        compiler_params=_cparams(("parallel", "parallel")),
        name="outproj",
    )(*ys, x, g1, sc2, sh2, wo, ln_g, ln_b, wr_pad, br_pad)


def _moe_kernel(be_ref, nu_ref, x_ref, wgu_ref, bgu_ref, wd_ref, bd_ref, o_ref, wgu_bf, wd_bf):
    i = pl.program_id(0)
    ff = wd_ref.shape[1]

    @pl.when((i == 0) | (be_ref[i] != be_ref[jnp.maximum(i - 1, 0)]))
    def _():
        wgu_bf[...] = wgu_ref[0].astype(BF16)
        wd_bf[...] = wd_ref[0].astype(BF16)

    @pl.when(i < nu_ref[0])
    def _():
        gu = jnp.dot(x_ref[...], wgu_bf[...], preferred_element_type=F32) + bgu_ref[0]
        gt = jnp.minimum(gu[:, :ff], SWIGLU_LIMIT)
        up = jnp.clip(gu[:, ff:], -SWIGLU_LIMIT, SWIGLU_LIMIT)
        act = (up + 1.0) * gt * jax.nn.sigmoid(SWIGLU_ALPHA * gt)
        o_ref[...] = jnp.dot(act.astype(BF16), wd_bf[...], preferred_element_type=F32) + bd_ref[0]

    @pl.when(i >= nu_ref[0])
    def _():
        o_ref[...] = jnp.zeros(o_ref.shape, F32)


def _moe_experts(xg, blk_exp, n_used, wgu, bgu, wd, bd, blk):
    cap, d = xg.shape
    e, _, ff2 = wgu.shape
    ff = ff2 // 2
    grid_spec = pltpu.PrefetchScalarGridSpec(
        num_scalar_prefetch=2,
        grid=(cap // blk,),
        in_specs=[pl.BlockSpec((blk, d), lambda i, be, nu: (i, 0)),
                  pl.BlockSpec((1, d, ff2), lambda i, be, nu: (be[i], 0, 0)),
                  pl.BlockSpec((1, 1, ff2), lambda i, be, nu: (be[i], 0, 0)),
                  pl.BlockSpec((1, ff, d), lambda i, be, nu: (be[i], 0, 0)),
                  pl.BlockSpec((1, 1, d), lambda i, be, nu: (be[i], 0, 0))],
        out_specs=pl.BlockSpec((blk, d), lambda i, be, nu: (i, 0)),
        scratch_shapes=[pltpu.VMEM((d, ff2), BF16), pltpu.VMEM((ff, d), BF16)],
    )
    return pl.pallas_call(
        _moe_kernel,
        grid_spec=grid_spec,
        out_shape=jax.ShapeDtypeStruct((cap, d), F32),
        compiler_params=_cparams(("arbitrary",)),
        name="moe_experts",
    )(blk_exp, n_used, xg, wgu, bgu.reshape(e, 1, ff2), wd, bd.reshape(e, 1, d))


def _slot_tokens_kernel(pos_ref, out_ref, *, chunk, shift):
    c = pl.program_id(0)

    @pl.when(c == 0)
    def _():
        def zero(i, carry):
            out_ref[i] = jnp.int32(0)
            return carry
        lax.fori_loop(0, out_ref.shape[0], zero, 0, unroll=8)

    base = c * chunk

    def body(i, carry):
        out_ref[pos_ref[i]] = lax.shift_right_logical(base + i, shift)
        return carry

    lax.fori_loop(0, chunk, body, 0, unroll=8)


def _slot_tokens(pos, cap):
    m = pos.shape[0]
    assert TOP_K & (TOP_K - 1) == 0
    chunk = min(m, 8192)
    assert m % chunk == 0
    return pl.pallas_call(
        functools.partial(_slot_tokens_kernel, chunk=chunk, shift=TOP_K.bit_length() - 1),
        grid=(m // chunk,),
        in_specs=[pl.BlockSpec((chunk,), lambda c: (c,), memory_space=pltpu.SMEM)],
        out_specs=pl.BlockSpec((cap,), lambda c: (0,), memory_space=pltpu.SMEM),
        out_shape=jax.ShapeDtypeStruct((cap,), jnp.int32),
        compiler_params=pltpu.CompilerParams(dimension_semantics=("arbitrary",)),
        name="slot_tokens",
    )(pos)


def _moe(h2, logits, mw, blk):
    n, d = h2.shape
    n_exp = logits.shape[1]
    top_v, top_e = lax.top_k(logits, TOP_K)
    gate = jax.nn.softmax(top_v, axis=-1)
    m = n * TOP_K
    cap = -(-(m + n_exp * (blk - 1)) // blk) * blk
    e_flat = top_e.reshape(m)
    onehot = (e_flat[:, None] == jnp.arange(n_exp)[None, :]).astype(jnp.int32)
    csum = jnp.cumsum(onehot, axis=0)
    counts = csum[-1]
    rank = jnp.take_along_axis(csum, e_flat[:, None], axis=1)[:, 0] - 1
    padded = (counts + blk - 1) // blk * blk
    pend = jnp.cumsum(padded)
    pstart = pend - padded
    pos = (pstart[e_flat] + rank).astype(jnp.int32)
    nblk = cap // blk
    blk_start = jnp.arange(nblk, dtype=jnp.int32) * blk
    blk_exp = jnp.minimum(jnp.sum((pend[None, :] <= blk_start[:, None]).astype(jnp.int32), axis=1), n_exp - 1)
    row_tok = _slot_tokens(pos, cap)
    n_used = (pend[-1] // blk).astype(jnp.int32).reshape(1)
    xg = h2.at[row_tok].get(mode='promise_in_bounds')
    out = _moe_experts(xg, blk_exp.astype(jnp.int32), n_used, *mw, blk)
    pos = pos.reshape(n, TOP_K)
    picked = [out.at[pos[:, j]].get(mode='promise_in_bounds') for j in range(TOP_K)]
    return picked, jnp.pad(gate, ((0, 0), (0, LANES - TOP_K)))


def _moe_block_rows(n_assign, n_exp):
    per_expert = max(1, n_assign // n_exp)
    return min(256, max(16, 1 << (per_expert.bit_length() - 1)))


def _resln_kernel(x_ref, gate_ref, *refs, alpha):
    picked, (g_ref, lg_ref, lb_ref, o_ref) = refs[:TOP_K], refs[TOP_K:]
    gate = gate_ref[0]
    y = gate[:, 0:1] * picked[0][0]
    for j in range(1, TOP_K):
        y = y + gate[:, j:j + 1] * picked[j][0]
    o_ref[0] = _ln_rows(alpha * x_ref[0] + g_ref[0] * y) * lg_ref[...] + lb_ref[...]


def _resln(x, picked, gate, g, ln_g, ln_b, alpha):
    b, t, d = x.shape
    tm = min(t, 512)
    per_row = g.shape[1] == t and t > 1
    mod = pl.BlockSpec((1, tm, d) if per_row else (1, 1, d),
                       (lambda i, j: (i, j, 0)) if per_row else (lambda i, j: (i, 0, 0)))
    row = lambda w: pl.BlockSpec((1, tm, w), lambda i, j: (i, j, 0))
    const = pl.BlockSpec((1, d), lambda i, j: (0, 0))
    return pl.pallas_call(
        functools.partial(_resln_kernel, alpha=alpha),
        grid=(b, t // tm),
        in_specs=[row(d), row(LANES)] + [row(d)] * TOP_K + [mod, const, const],
        out_specs=row(d),
        out_shape=jax.ShapeDtypeStruct((b, t, d), F32),
        compiler_params=_cparams(("parallel", "parallel")),
        name="resln",
    )(x, gate.reshape(b, t, LANES), *[p.reshape(b, t, d) for p in picked], g, ln_g, ln_b)


def _conv_mixer(zd, buf, conv_w):
    t = zd.shape[1]
    ch = GROUP_WIDTH
    b_gate, c_gate, xin = zd[..., :ch], zd[..., ch:2 * ch], zd[..., 2 * ch:3 * ch]
    ue = jnp.concatenate([buf, c_gate * xin], axis=1)
    y = sum(ue[:, j:j + t] * conv_w[j] for j in range(CONV_K))
    return b_gate * y, ue[:, t:]


def _masked_softmax(s, valid):
    p = jax.nn.softmax(jnp.where(valid, s, NEG_INF), axis=-1)
    return jnp.where(valid, p, 0.0)


def _paged_fetch(pool, page_table, new_rows):
    n_past = page_table.shape[1] * PAGE_SIZE
    n_new = new_rows.shape[1]

    def fetch(pos):
        b = pos.shape[0]
        flat = pos.reshape(b, -1)
        bi = jnp.arange(b)[:, None]
        pc = jnp.clip(flat, 0, n_past - 1)
        past = pool[page_table[bi, pc // PAGE_SIZE], pc % PAGE_SIZE]
        new = new_rows[bi, jnp.clip(flat - n_past, 0, n_new - 1)]
        is_past = (flat < n_past).reshape(flat.shape + (1,) * (past.ndim - 2))
        return jnp.where(is_past, past, new).reshape(pos.shape + past.shape[2:])
    return fetch


def _nsa_attend_sample(q, gates, qpos, kvc, fetch_slc, kvw, wpos, slopes, n_keys):
    b, tq = q.shape[:2]
    q = q * HEAD_DIM ** -0.5
    sl = jnp.asarray(slopes, F32)[:, None, None]
    nc = kvc.shape[1]
    cend = jnp.arange(nc) * NSA_CMP_BLOCK + (NSA_CMP_BLOCK - 1)
    s_c = jnp.einsum('bqhd,bnd->bhqn', q, kvc[:, :, 0]) - sl * (qpos[:, None] - cend[None, :]).astype(F32)
    p_c = _masked_softmax(s_c, cend[None, :] <= qpos[:, None])
    o_c = jnp.einsum('bhqn,bnd->bqhd', p_c, kvc[:, :, 1])
    ratio = NSA_SEL_BLOCK // NSA_CMP_BLOCK
    n_sel = -(-n_keys // NSA_SEL_BLOCK)
    imp = jnp.pad(p_c.sum(1), ((0, 0), (0, 0), (0, n_sel * ratio - nc))).reshape(b, tq, n_sel, ratio).sum(-1)
    blk = jnp.arange(n_sel)[None, :]
    cur = (qpos // NSA_SEL_BLOCK)[:, None]
    forced = (blk == 0) | (blk > cur - NSA_LOCAL_BLOCKS)
    score = jnp.where(blk > cur, NEG_INF, jnp.where(forced, FORCE_SCORE, imp))
    n_top = min(NSA_TOPN, n_sel)
    _, sel = lax.top_k(score, n_top)
    kpos = (sel[..., None] * NSA_SEL_BLOCK + jnp.arange(NSA_SEL_BLOCK)).reshape(b, tq, n_top * NSA_SEL_BLOCK)
    kv_s = fetch_slc(kpos)
    s_s = (jnp.einsum('bqhd,bqnd->bhqn', q, kv_s[..., 0, :])
           - sl * (qpos[None, None, :, None] - kpos[:, None]).astype(F32))
    p_s = _masked_softmax(s_s, (kpos <= qpos[None, :, None])[:, None])
    o_s = jnp.einsum('bhqn,bqnd->bqhd', p_s, kv_s[..., 1, :])
    dw = qpos[:, None] - wpos[None, :]
    s_w = jnp.einsum('bqhd,bnd->bhqn', q, kvw[:, :, 0]) - sl * dw.astype(F32)
    p_w = _masked_softmax(s_w, (dw >= 0) & (dw <= NSA_WINDOW) & (wpos[None, :] >= 0))
    o_w = jnp.einsum('bhqn,bnd->bqhd', p_w, kvw[:, :, 1])
    return gates[..., 0:1] * o_c + gates[..., 1:2] * o_s + gates[..., 2:3] * o_w


def _dsa_sample_kernel(pt_ref, zt_ref, kv_hbm, idx_hbm, o_ref, kvbuf, idxbuf, sem, score_scr, s_scr, p_scr,
                       *, layer, n_pages, k_sel, slopes):
    b = pl.program_id(0)
    nb = pl.num_programs(0)
    slot = b % 2
    ps = PAGE_SIZE
    gw = GROUP_WIDTH

    def page_copies(seq, sl, p):
        pg = pt_ref[seq, p]
        return (pltpu.make_async_copy(kv_hbm.at[layer, pg], kvbuf.at[sl, p], sem.at[0, sl]),
                pltpu.make_async_copy(idx_hbm.at[layer, pg], idxbuf.at[sl, p], sem.at[1, sl]))

    def fetch(seq, sl):
        def body(p, carry):
            for cp in page_copies(seq, sl, p):
                cp.start()
            return carry
        lax.fori_loop(0, n_pages, body, 0)

    @pl.when(b == 0)
    def _():
        fetch(0, 0)

    @pl.when(b + 1 < nb)
    def _():
        fetch(b + 1, 1 - slot)

    def wait_body(p, carry):
        for cp in page_copies(b, slot, p):
            cp.wait()
        return carry

    lax.fori_loop(0, n_pages, wait_body, 0)

    zt = zt_ref[0]
    o_kv = gw
    o_qi = gw + 2 * HEAD_DIM
    o_ki = o_qi + IDX_HEADS * IDX_DIM
    o_wi = o_ki + IDX_DIM
    q_cols = [zt[h * HEAD_DIM:(h + 1) * HEAD_DIM] * (HEAD_DIM ** -0.5) for h in range(N_HEADS)]
    qi_cols = [zt[o_qi + h * IDX_DIM:o_qi + (h + 1) * IDX_DIM] for h in range(IDX_HEADS)]
    w_idx = [zt[o_wi + h:o_wi + h + 1] for h in range(IDX_HEADS)]
    k_new, v_new = zt[o_kv:o_kv + HEAD_DIM], zt[o_kv + HEAD_DIM:o_kv + 2 * HEAD_DIM]
    ki_new = zt[o_ki:o_ki + IDX_DIM]
    colsum = lambda x: jnp.sum(x, axis=0, keepdims=True)
    total = lambda x: jnp.sum(colsum(x), axis=1, keepdims=True)
    unzero = lambda x: jnp.where(x == 0.0, 0.0, x)

    def order_key(x):
        bits = pltpu.bitcast(x, jnp.int32)
        return bits ^ ((bits >> 31) & 0x7FFFFFFF)

    qi_wide = [jnp.broadcast_to(c, (IDX_DIM, ps)) for c in qi_cols]
    w_wide = [jnp.broadcast_to(w, (1, ps)) for w in w_idx]

    def score_page(p, carry):
        kit = idxbuf[slot, p]
        row = w_wide[0] * jnp.maximum(colsum(kit * qi_wide[0]), 0.0)
        for h in range(1, IDX_HEADS):
            row = row + w_wide[h] * jnp.maximum(colsum(kit * qi_wide[h]), 0.0)
        score_scr[pl.ds(p, 1), :] = row
        return carry

    lax.fori_loop(0, n_pages, score_page, 0)
    sc_new = w_idx[0] * jnp.maximum(colsum(ki_new * qi_cols[0]), 0.0)
    for h in range(1, IDX_HEADS):
        sc_new = sc_new + w_idx[h] * jnp.maximum(colsum(ki_new * qi_cols[h]), 0.0)
    key = order_key(unzero(score_scr[...]))
    key_new = order_key(unzero(sc_new))

    kf = float(k_sel)

    def count(c, strict):
        past = jnp.where((key > c) if strict else (key >= c), 1.0, 0.0)
        new = jnp.where((key_new > c) if strict else (key_new >= c), 1.0, 0.0)
        return total(past) + new

    int_min = jnp.int32(-2 ** 31)
    tau = jnp.where(count(jnp.zeros((1, 1), jnp.int32), False) >= kf, jnp.int32(0), int_min)

    def bit_step(i, tau):
        cand = tau + lax.shift_left(jnp.int32(1), 30 - i)
        return jnp.where(count(cand, False) >= kf, cand, tau)

    tau = lax.fori_loop(0, 31, bit_step, tau)
    need = kf - count(tau, True)
    eq = key == tau
    eqf = jnp.where(eq, 1.0, 0.0).astype(BF16)
    ri = lax.broadcasted_iota(jnp.int32, (ps, ps), 0)
    ci = lax.broadcasted_iota(jnp.int32, (ps, ps), 1)
    in_row = jnp.dot(eqf, jnp.where(ri <= ci, 1.0, 0.0).astype(BF16), preferred_element_type=F32)
    row_tot = jnp.broadcast_to(in_row[:, ps - 1:ps], (n_pages, ps)).astype(BF16)
    pi = lax.broadcasted_iota(jnp.int32, (n_pages, n_pages), 0)
    pj = lax.broadcasted_iota(jnp.int32, (n_pages, n_pages), 1)
    rows_before = jnp.dot(jnp.where(pj < pi, 1.0, 0.0).astype(BF16), row_tot, preferred_element_type=F32)
    sel = (key > tau) | (eq & (in_row + rows_before <= need))
    eq_new = key_new == tau
    sel_new = (key_new > tau) | (eq_new & (total(jnp.where(eq, 1.0, 0.0)) + 1.0 <= need))

    q_wide = [jnp.broadcast_to(c, (HEAD_DIM, ps)) for c in q_cols]

    def qk_page(p, carry):
        kt = kvbuf[slot, p, 0]
        for h in range(N_HEADS):
            s_scr[h, pl.ds(p, 1), :] = colsum(kt * q_wide[h])
        return carry

    lax.fori_loop(0, n_pages, qk_page, 0)
    kpos = (lax.broadcasted_iota(jnp.int32, (n_pages, ps), 0) * ps
            + lax.broadcasted_iota(jnp.int32, (n_pages, ps), 1)).astype(F32)
    e_new, inv = [], []
    for h in range(N_HEADS):
        s = jnp.where(sel, s_scr[h] + slopes[h] * kpos, NEG_INF)
        s_n = jnp.where(sel_new, colsum(k_new * q_cols[h]) + slopes[h] * float(n_pages * ps), NEG_INF)
        m = jnp.maximum(jnp.max(jnp.max(s, axis=0, keepdims=True), axis=1, keepdims=True), s_n)
        e = jnp.exp(s - m)
        e_n = jnp.exp(s_n - m)
        p_scr[h] = e
        e_new.append(e_n)
        inv.append(1.0 / (total(e) + e_n))

    def pv_page(p, accs):
        vt = kvbuf[slot, p, 1]
        return tuple(accs[h] + vt * p_scr[h, pl.ds(p, 1), :] for h in range(N_HEADS))

    accs = lax.fori_loop(0, n_pages, pv_page, tuple(jnp.zeros((HEAD_DIM, ps), F32) for _ in range(N_HEADS)))
    outs = [(jnp.sum(accs[h], axis=1, keepdims=True) + e_new[h] * v_new) * inv[h] for h in range(N_HEADS)]
    o_ref[0] = jnp.concatenate(outs, axis=0)


def _dsa_sample(zc, cache_dsa_kv, cache_dsa_idx, layer, page_table, slopes):
    b, t, c_pad = zc.shape
    assert t == 1
    n_pages = page_table.shape[1]
    n_keys = n_pages * PAGE_SIZE + t
    k_sel = min(DSA_TOPK, n_keys // 4)
    kv_t = cache_dsa_kv.transpose(0, 1, 3, 4, 2)
    idx_t = cache_dsa_idx.transpose(0, 1, 3, 2)
    grid_spec = pltpu.PrefetchScalarGridSpec(
        num_scalar_prefetch=1,
        grid=(b,),
        in_specs=[pl.BlockSpec((1, c_pad, 1), lambda i, pt: (i, 0, 0)),
                  pl.BlockSpec(memory_space=pl.ANY), pl.BlockSpec(memory_space=pl.ANY)],
        out_specs=pl.BlockSpec((1, GROUP_WIDTH, 1), lambda i, pt: (i, 0, 0)),
        scratch_shapes=[pltpu.VMEM((2, n_pages, 2, HEAD_DIM, PAGE_SIZE), F32),
                        pltpu.VMEM((2, n_pages, IDX_DIM, PAGE_SIZE), F32),
                        pltpu.SemaphoreType.DMA((2, 2)),
                        pltpu.VMEM((n_pages, PAGE_SIZE), F32),
                        pltpu.VMEM((N_HEADS, n_pages, PAGE_SIZE), F32),
                        pltpu.VMEM((N_HEADS, n_pages, PAGE_SIZE), F32)],
    )
    out = pl.pallas_call(
        functools.partial(_dsa_sample_kernel, layer=layer, n_pages=n_pages, k_sel=k_sel, slopes=slopes),
        grid_spec=grid_spec,
        out_shape=jax.ShapeDtypeStruct((b, GROUP_WIDTH, 1), F32),
        compiler_params=_cparams(("arbitrary",)),
        name="dsa_sample",
    )(page_table, zc.transpose(0, 2, 1), kv_t, idx_t)
    return out.transpose(0, 2, 1)


def _kv_rows(u):
    return u.reshape(u.shape[0], u.shape[1], 2, HEAD_DIM)


def _mixers_prompt(za, zb, zc, zd, lp, sl_nsa, sl_dsa):
    b, t, _ = za.shape
    gw = GROUP_WIDTH
    ya, s_rwkv, shift = _rwkv_mixer(za, jnp.zeros((b, A_PAD), F32),
                                    jnp.zeros((b, N_HEADS, HEAD_DIM, HEAD_DIM), F32), lp)
    kv_c = _kv_rows(zb[..., gw:gw + 2 * HEAD_DIM])
    kv_s = _kv_rows(zb[..., gw + 2 * HEAD_DIM:gw + 4 * HEAD_DIM])
    kv_w = _kv_rows(zb[..., gw + 4 * HEAD_DIM:gw + 6 * HEAD_DIM])
    kvc = _nsa_compress(kv_c, lp).reshape(b, -1, 2 * HEAD_DIM)
    nc = kvc.shape[1]
    kvc = jnp.pad(kvc, ((0, 0), (0, LANES - nc), (0, 0)))
    kvc = jnp.concatenate([kvc[:, 0::2], kvc[:, 1::2]], axis=1)
    yb = _nsa_prompt(zb, kvc, sl_nsa)
    yc = _dsa_prompt(zc, sl_dsa)
    kv_d = _kv_rows(zc[..., gw:gw + 2 * HEAD_DIM])
    o = gw + 2 * HEAD_DIM + IDX_HEADS * IDX_DIM
    ki = zc[..., o:o + IDX_DIM]
    yd, conv_buf = _conv_mixer(zd, jnp.zeros((b, CONV_K - 1, gw), F32), lp['conv_w'])
    return (ya, yb, yc, yd), (s_rwkv, shift, kv_c, kv_s, kv_w[:, -min(NSA_WINDOW, t):], kv_d, ki, conv_buf)


def _mixers_sample(za, zb, zc, zd, lp, sl_nsa, sl_dsa, s_rwkv, s_shift, c_nsa_cmp, c_nsa_slc, s_win,
                   cache_dsa_kv, cache_dsa_idx, layer, s_conv, page_table):
    b, t, _ = za.shape
    gw = GROUP_WIDTH
    n_past = page_table.shape[1] * PAGE_SIZE
    n_buf = s_win.shape[1]
    qpos = n_past + jnp.arange(t)
    shift_prev = jnp.pad(s_shift, ((0, 0), (0, A_PAD - A_COLS)))
    ya, s_rwkv_new, shift = _rwkv_mixer(za, shift_prev, s_rwkv, lp)
    q = zb[..., :gw].reshape(b, t, N_HEADS, HEAD_DIM)
    kv_c = _kv_rows(zb[..., gw:gw + 2 * HEAD_DIM])
    kv_s = _kv_rows(zb[..., gw + 2 * HEAD_DIM:gw + 4 * HEAD_DIM])
    kv_w = _kv_rows(zb[..., gw + 4 * HEAD_DIM:gw + 6 * HEAD_DIM])
    o = gw + 6 * HEAD_DIM
    gates = jax.nn.sigmoid(zb[..., o:o + 3 * N_HEADS]).reshape(b, t, N_HEADS, 3)
    past_c = c_nsa_cmp[page_table].reshape(b, n_past, 2, HEAD_DIM)
    kvc = _nsa_compress(jnp.concatenate([past_c, kv_c], axis=1), lp)
    kvw = jnp.concatenate([s_win, kv_w], axis=1)
    wpos = n_past - n_buf + jnp.arange(n_buf + t)
    yb = _nsa_attend_sample(q, gates, qpos, kvc, _paged_fetch(c_nsa_slc, page_table, kv_s), kvw, wpos,
                            sl_nsa, n_past + t)
    kv_d = _kv_rows(zc[..., gw:gw + 2 * HEAD_DIM])
    o = gw + 2 * HEAD_DIM + IDX_HEADS * IDX_DIM
    ki = zc[..., o:o + IDX_DIM]
    yc = _dsa_sample(zc, cache_dsa_kv, cache_dsa_idx, layer, page_table, sl_dsa)
    yd, conv_buf = _conv_mixer(zd, s_conv, lp['conv_w'])
    flat = lambda u: u.reshape(b, t, gw)
    return (ya, flat(yb), flat(yc), yd), (s_rwkv_new, shift, kv_c, kv_s, kvw[:, -n_buf:], kv_d, ki, conv_buf)


def _layer(x, mod, lp, pw, mixer_fn, alpha):
    b, t, d = x.shape
    sh1, sc1, g1, sh2, sc2, g2 = mod
    zs = _inproj(x, sc1, sh1, pw['w_in'])
    ys, states = mixer_fn(*zs)
    x1, h2, logits = _outproj(ys, x, g1, sc2, sh2, pw['w_out'], pw['ln1_g'], pw['ln1_b'],
                              pw['wr'], pw['br'], alpha)
    n_exp = lp['moe_wr'].shape[1]
    blk = _moe_block_rows(b * t * TOP_K, n_exp)
    picked, gate = _moe(h2.reshape(b * t, d), logits.reshape(b * t, LANES)[:, :n_exp], pw['moe'], blk)
    x2 = _resln(x1, picked, gate, g2, pw['ln2_g'], pw['ln2_b'], alpha)
    return x2, states


def _prep_weights(lp):
    d = lp['w_in'].shape[0]
    cols, o = [], 0
    for c, p in ((A_COLS, A_PAD), (B_COLS, B_PAD), (C_COLS, C_PAD), (D_COLS, D_PAD)):
        cols.append(jnp.pad(lp['w_in'][:, o:o + c], ((0, 0), (0, p - c))))
        o += c
    n_exp = lp['moe_wr'].shape[1]
    row = lambda u: u.reshape(1, -1)
    return {
        'w_in': jnp.concatenate(cols, axis=1).astype(BF16),
        'w_out': lp['w_out'].reshape(4, GROUP_WIDTH, d).astype(BF16),
        'ln1_g': row(lp['ln1_g']), 'ln1_b': row(lp['ln1_b']),
        'ln2_g': row(lp['ln2_g']), 'ln2_b': row(lp['ln2_b']),
        'wr': jnp.pad(lp['moe_wr'], ((0, 0), (0, LANES - n_exp))),
        'br': jnp.pad(lp['moe_br'], (0, LANES - n_exp)).reshape(1, LANES),
        'moe': (lp['moe_wgu'], lp['moe_bgu'], lp['moe_wd'], lp['moe_bd']),
    }


def kernel(x_prompt, x_sample, state_rwkv, state_rwkv_shift, cache_nsa_cmp, cache_nsa_slc, state_nsa_win, cache_dsa_kv, cache_dsa_idx, state_conv, page_table, c_prompt, c_sample, w_ada, b_ada, w_in, w_out, ln1_g, ln1_b, ln2_g, ln2_b, rwkv_mu, rwkv_w0, rwkv_w2, rwkv_a0, rwkv_a2, rwkv_g2, rwkv_kk, rwkv_ka, rwkv_rk, rwkv_gn_g, rwkv_gn_b, nsa_pe, nsa_w1, nsa_w2, conv_w, moe_wr, moe_br, moe_wgu, moe_bgu, moe_wd, moe_bd):
    depth = w_in.shape[0]
    alpha = (2 * depth) ** 0.25
    sl_nsa, sl_dsa = _alibi_slopes()
    bp = x_prompt.shape[0]
    hp, hs = x_prompt, x_sample
    c_all = jnp.concatenate([c_prompt, c_sample], axis=0)
    states_p, states_s = [], []
    for l in range(depth):
        lp = {
            'w_in': w_in[l], 'w_out': w_out[l], 'ln1_g': ln1_g[l], 'ln1_b': ln1_b[l],
            'ln2_g': ln2_g[l], 'ln2_b': ln2_b[l],
            'rwkv_mu': jnp.pad(rwkv_mu[l], (0, A_PAD - A_COLS)), 'rwkv_w0': rwkv_w0[l], 'rwkv_w2': rwkv_w2[l],
            'rwkv_a0': rwkv_a0[l], 'rwkv_a2': rwkv_a2[l], 'rwkv_g2': rwkv_g2[l],
            'rwkv_kk': rwkv_kk[l], 'rwkv_ka': rwkv_ka[l], 'rwkv_rk': rwkv_rk[l],
            'rwkv_gn_g': rwkv_gn_g[l], 'rwkv_gn_b': rwkv_gn_b[l],
            'nsa_pe': nsa_pe[l], 'nsa_w1': nsa_w1[l], 'nsa_w2': nsa_w2[l], 'conv_w': conv_w[l],
            'moe_wr': moe_wr[l], 'moe_br': moe_br[l], 'moe_wgu': moe_wgu[l], 'moe_bgu': moe_bgu[l],
            'moe_wd': moe_wd[l], 'moe_bd': moe_bd[l],
        }
        pw = _prep_weights(lp)
        mod_all = jax.nn.silu(c_all) @ w_ada[l] + b_ada[l]
        mod_p = [u[:, None, :] for u in jnp.split(mod_all[:bp], 6, axis=-1)]
        mod_s = [u[:, None, :] for u in jnp.split(mod_all[bp:], 6, axis=-1)]
        mix_p = functools.partial(_mixers_prompt, lp=lp, sl_nsa=sl_nsa, sl_dsa=sl_dsa)
        mix_s = functools.partial(
            _mixers_sample, lp=lp, sl_nsa=sl_nsa, sl_dsa=sl_dsa, s_rwkv=state_rwkv[l],
            s_shift=state_rwkv_shift[l], c_nsa_cmp=cache_nsa_cmp[l], c_nsa_slc=cache_nsa_slc[l],
            s_win=state_nsa_win[l], cache_dsa_kv=cache_dsa_kv, cache_dsa_idx=cache_dsa_idx, layer=l,
            s_conv=state_conv[l], page_table=page_table)
        hs, st_s = _layer(hs, mod_s, lp, pw, mix_s, alpha)
        hp, st_p = _layer(hp, mod_p, lp, pw, mix_p, alpha)
        states_p.append(st_p)
        states_s.append(st_s)
    outs_p = [jnp.stack(v) for v in zip(*states_p)]
    outs_s = [jnp.stack(v) for v in zip(*states_s)]
    return (hp, hs, *outs_p, *outs_s)
```

```python
import functools

import jax
import jax.numpy as jnp
import numpy as np
from jax import lax
from jax.experimental import pallas as pl
from jax.experimental.pallas import tpu as pltpu

F32 = jnp.float32
BF16 = jnp.bfloat16

HEAD_DIM = 64
N_HEADS = 4
GROUP_WIDTH = N_HEADS * HEAD_DIM
RWKV_DECAY_SCALE = 0.606531
RWKV_GN_EPS = 64e-5
NSA_CMP_BLOCK = 32
NSA_SEL_BLOCK = 64
NSA_TOPN = 16
NSA_LOCAL_BLOCKS = 2
NSA_WINDOW = 512
DSA_TOPK = 256
IDX_HEADS = 4
IDX_DIM = 32
CONV_K = 3
TOP_K = 4
SWIGLU_LIMIT = 7.0
SWIGLU_ALPHA = 1.702
NEG_INF = -1e30
FORCE_SCORE = 1e4
Q_BLOCK = 128
PAGE_SIZE = 128

LANES = 128
A_COLS = 3 * GROUP_WIDTH + 32 + 32 + 64
B_COLS = GROUP_WIDTH + 6 * HEAD_DIM + 3 * N_HEADS
C_COLS = GROUP_WIDTH + 2 * HEAD_DIM + IDX_HEADS * IDX_DIM + IDX_DIM + IDX_HEADS
D_COLS = 3 * GROUP_WIDTH


def _pad_to(n, m):
    return -(-n // m) * m


A_PAD, B_PAD, C_PAD, D_PAD = (_pad_to(c, LANES) for c in (A_COLS, B_COLS, C_COLS, D_COLS))
VMEM_LIMIT = 56 * 1024 * 1024

_NT = (((1,), (1,)), ((), ()))


def _alibi_slopes():
    n = 2 * N_HEADS
    s = [2.0 ** (-8.0 * i / n) for i in range(1, n + 1)]
    return s[0::2], s[1::2]


def _ln_rows(x, eps=1e-5):
    mu = jnp.mean(x, axis=-1, keepdims=True)
    xc = x - mu
    var = jnp.mean(xc * xc, axis=-1, keepdims=True)
    return xc * lax.rsqrt(var + eps)


def _cparams(sem):
    return pltpu.CompilerParams(dimension_semantics=sem, vmem_limit_bytes=VMEM_LIMIT)


def _inproj_kernel(x_ref, sc_ref, sh_ref, w_ref, za_ref, zb_ref, zc_ref, zd_ref):
    h = _ln_rows(x_ref[0]) * (1.0 + sc_ref[0]) + sh_ref[0]
    z = jnp.dot(h.astype(BF16), w_ref[...], preferred_element_type=F32)
    o = 0
    for ref, w in ((za_ref, A_PAD), (zb_ref, B_PAD), (zc_ref, C_PAD), (zd_ref, D_PAD)):
        ref[0] = z[:, o:o + w]
        o += w


def _inproj(x, sc, sh, w_pad):
    b, t, d = x.shape
    tm = min(t, 512)
    mod_rows = sc.shape[1]
    mod_blk = (1, tm, d) if mod_rows == t and t > 1 else (1, 1, d)
    mod_map = (lambda i, j: (i, j, 0)) if mod_rows == t and t > 1 else (lambda i, j: (i, 0, 0))
    widths = (A_PAD, B_PAD, C_PAD, D_PAD)
    return pl.pallas_call(
        _inproj_kernel,
        grid=(b, t // tm),
        in_specs=[pl.BlockSpec((1, tm, d), lambda i, j: (i, j, 0)),
                  pl.BlockSpec(mod_blk, mod_map),
                  pl.BlockSpec(mod_blk, mod_map),
                  pl.BlockSpec(w_pad.shape, lambda i, j: (0, 0))],
        out_specs=[pl.BlockSpec((1, tm, w), lambda i, j: (i, j, 0)) for w in widths],
        out_shape=[jax.ShapeDtypeStruct((b, t, w), F32) for w in widths],
        compiler_params=_cparams(("parallel", "parallel")),
        name="inproj",
    )(x, sc, sh, w_pad)


def _rwkv_scan_kernel(r_ref, w_ref, k_ref, v_ref, kk_ref, kka_ref, s0_ref, y_ref, sfin_ref,
                      s_scr, y_scr, *, bg, tc, unroll):
    c = pl.program_id(1)
    rows = bg * HEAD_DIM

    @pl.when(c == 0)
    def _():
        s_scr[...] = s0_ref[...].reshape(rows, GROUP_WIDTH)

    li = lax.broadcasted_iota(jnp.int32, (2 * GROUP_WIDTH, GROUP_WIDTH), 0)
    lj = lax.broadcasted_iota(jnp.int32, (2 * GROUP_WIDTH, GROUP_WIDTH), 1)
    seg_ones2 = jnp.where((li % GROUP_WIDTH) // HEAD_DIM == lj // HEAD_DIM, 1.0, 0.0).astype(BF16)
    seg_ones1 = seg_ones2[:GROUP_WIDTH]
    lane = lax.broadcasted_iota(jnp.int32, (rows, GROUP_WIDTH), 1)
    row = lax.broadcasted_iota(jnp.int32, (rows, GROUP_WIDTH), 0)
    diag = jnp.where(row % HEAD_DIM == lane % HEAD_DIM, 1.0, 0.0)

    def seg_sum(x):
        hi = x.astype(BF16)
        lo = (x - hi.astype(F32)).astype(BF16)
        return jnp.dot(jnp.concatenate([hi, lo], axis=1), seg_ones2, preferred_element_type=F32)

    def bcast_rows(ref, t):
        return jnp.concatenate(
            [jnp.broadcast_to(ref[b, pl.ds(t, 1), :], (HEAD_DIM, GROUP_WIDTH)) for b in range(bg)], axis=0)

    def step(t):
        s = s_scr[...]
        kk = bcast_rows(kk_ref, t)
        sa = seg_sum(s * kk)
        vcol = seg_sum(bcast_rows(v_ref, t) * diag)
        s = s * bcast_rows(w_ref, t) - sa * bcast_rows(kka_ref, t) + vcol * bcast_rows(k_ref, t)
        s_scr[...] = s
        y = jnp.dot((s * bcast_rows(r_ref, t)).astype(BF16), seg_ones1, preferred_element_type=F32)
        y_scr[...] = jnp.where(lane % HEAD_DIM == t, y, y_scr[...])

    def outer(i, carry):
        for u in range(unroll):
            step(i * unroll + u)
        return carry

    y_scr[...] = jnp.zeros(y_scr.shape, F32)
    lax.fori_loop(0, tc // unroll, outer, 0)
    y_ref[...] = y_scr[...].reshape(bg, 1, HEAD_DIM, GROUP_WIDTH)

    @pl.when(c == pl.num_programs(1) - 1)
    def _():
        sfin_ref[...] = s_scr[...].reshape(bg, HEAD_DIM, GROUP_WIDTH)


def _rwkv_scan(r, w, k, v, kk, kka, s0):
    b, t, gw = r.shape
    bg = 8 if b % 8 == 0 else b
    tc = min(t, HEAD_DIM)
    unroll = 4 if tc % 4 == 0 else 1
    nc = t // tc
    s0l = s0.transpose(0, 2, 1, 3).reshape(b, HEAD_DIM, gw)
    seq = pl.BlockSpec((bg, tc, gw), lambda i, c: (i, c, 0))
    st = pl.BlockSpec((bg, HEAD_DIM, gw), lambda i, c: (i, 0, 0))
    y_raw, s_fin = pl.pallas_call(
        functools.partial(_rwkv_scan_kernel, bg=bg, tc=tc, unroll=unroll),
        grid=(b // bg, nc),
        in_specs=[seq] * 6 + [st],
        out_specs=[pl.BlockSpec((bg, 1, HEAD_DIM, gw), lambda i, c: (i, c, 0, 0)), st],
        out_shape=[jax.ShapeDtypeStruct((b, nc, HEAD_DIM, gw), F32),
                   jax.ShapeDtypeStruct((b, HEAD_DIM, gw), F32)],
        scratch_shapes=[pltpu.VMEM((bg * HEAD_DIM, gw), F32), pltpu.VMEM((bg * HEAD_DIM, gw), F32)],
        compiler_params=_cparams(("parallel", "arbitrary")),
        name="rwkv_scan",
    )(r, w, k, v, kk, kka, s0l)
    y = y_raw.reshape(b, nc, HEAD_DIM, N_HEADS, HEAD_DIM)[..., :tc]
    y = y.transpose(0, 1, 4, 3, 2).reshape(b, t, N_HEADS, HEAD_DIM)
    s_fin = s_fin.reshape(b, HEAD_DIM, N_HEADS, HEAD_DIM).transpose(0, 2, 1, 3)
    return y, s_fin


def _rwkv_mixer(za, shift_prev, s0, lp):
    b, t, _ = za.shape
    z_prev = jnp.concatenate([shift_prev[:, None], za[:, :-1]], axis=1)
    zs = za + lp['rwkv_mu'] * (z_prev - za)
    gw = GROUP_WIDTH
    r, k, v = zs[..., :gw], zs[..., gw:2 * gw], zs[..., 2 * gw:3 * gw]
    o = 3 * gw
    w_lo, a_lo, g_lo = zs[..., o:o + 32], zs[..., o + 32:o + 64], zs[..., o + 64:o + 128]
    decay = jnp.exp(-RWKV_DECAY_SCALE * jax.nn.sigmoid(lp['rwkv_w0'] + jnp.tanh(w_lo) @ lp['rwkv_w2']))
    a = jax.nn.sigmoid(lp['rwkv_a0'] + a_lo @ lp['rwkv_a2'])
    g = jax.nn.sigmoid(g_lo) @ lp['rwkv_g2']
    heads = lambda u: u.reshape(b, t, N_HEADS, HEAD_DIM)
    kk = heads(k) * lp['rwkv_kk']
    kk = kk / jnp.maximum(jnp.sqrt(jnp.sum(kk * kk, axis=-1, keepdims=True)), 1e-12)
    kk = kk.reshape(b, t, gw)
    k = k * (1.0 + (a - 1.0) * lp['rwkv_ka'].reshape(gw))
    y, s_fin = _rwkv_scan(r, decay, k, v, kk, kk * a, s0)
    mu = jnp.mean(y, axis=-1, keepdims=True)
    var = jnp.mean(jnp.square(y - mu), axis=-1, keepdims=True)
    yn = ((y - mu) * lax.rsqrt(var + RWKV_GN_EPS) * lp['rwkv_gn_g'].reshape(N_HEADS, HEAD_DIM)
          + lp['rwkv_gn_b'].reshape(N_HEADS, HEAD_DIM))
    bonus = jnp.sum(heads(r) * heads(k) * lp['rwkv_rk'], axis=-1, keepdims=True) * heads(v)
    out = (yn + bonus).reshape(b, t, gw) * g
    return out, s_fin, za[:, -1, :A_COLS]


KEY_TILE = 512
POS_SPLIT = 128
M_FLOOR = -1e29


def _aug_keys(kv, pos):
    lane = lax.broadcasted_iota(jnp.int32, kv.shape, 1)
    hi = (pos // POS_SPLIT).astype(F32)
    lo = (pos % POS_SPLIT).astype(F32)
    extra = jnp.where(lane == 0, hi, jnp.where(lane == 1, lo, 0.0))
    return jnp.concatenate([kv.astype(BF16), extra.astype(BF16)], axis=1)


def _aug_queries(q2, slopes, lane):
    rows = []
    for h in range(N_HEADS):
        pair = q2[:, (h // 2) * LANES:(h // 2 + 1) * LANES]
        if h % 2:
            pair = pltpu.roll(pair, HEAD_DIM, 1)
        extra = jnp.where(lane == 0, POS_SPLIT * slopes[h], jnp.where(lane == 1, slopes[h], 0.0))
        rows.append(jnp.concatenate([jnp.where(lane < HEAD_DIM, pair, 0.0), extra], axis=1))
    return jnp.concatenate(rows, axis=0).astype(BF16)


def _tile_heads(x):
    return jnp.concatenate([x] * N_HEADS, axis=0)


def _softmax_block(qa, ka, madd):
    s = lax.dot_general(qa, ka, _NT, preferred_element_type=F32) + _tile_heads(madd)
    m = jnp.maximum(jnp.max(s, axis=1, keepdims=True), M_FLOOR)
    e = jnp.exp(s - m)
    inv = 1.0 / jnp.maximum(jnp.sum(e, axis=1, keepdims=True), 1e-30)
    o = jnp.dot(e.astype(BF16), ka, preferred_element_type=F32) * inv
    return e, inv, o


def _flash_step(qa, ka, madd, m, l, acc):
    s = lax.dot_general(qa, ka, _NT, preferred_element_type=F32) + _tile_heads(madd)
    m_new = jnp.maximum(m, jnp.max(s, axis=1, keepdims=True))
    alpha = jnp.exp(m - m_new)
    p = jnp.exp(s - m_new)
    l = alpha * l + jnp.sum(p, axis=1, keepdims=True)
    acc = alpha * acc + jnp.dot(p.astype(BF16), ka, preferred_element_type=F32)
    return m_new, l, acc


def _flash_init(qb=Q_BLOCK):
    rows = N_HEADS * qb
    return (jnp.full((rows, 1), M_FLOOR, F32), jnp.zeros((rows, 1), F32), jnp.zeros((rows, 2 * LANES), F32))


def _heads_out(per_head, lane):
    pair = lambda a, b: jnp.where(lane < HEAD_DIM, pltpu.roll(a, HEAD_DIM, 1), b)
    return jnp.concatenate([pair(per_head[0], per_head[1]), pair(per_head[2], per_head[3])], axis=1)


def _head_rows(o, h, qb=Q_BLOCK):
    return o[h * qb:(h + 1) * qb, :LANES]


def _nsa_kernel(q_ref, ksv_ref, kwv_ref, gl_ref, kvc_ref, exp_ref, o_ref, ks_scr, kw_scr,
                *, t, n_win, n_top, slopes):
    qb = Q_BLOCK
    j = pl.program_id(1)
    q0 = j * qb

    @pl.when(j == 0)
    def _():
        for c in range(t // KEY_TILE):
            rows = slice(c * KEY_TILE, (c + 1) * KEY_TILE)
            pos = c * KEY_TILE + lax.broadcasted_iota(jnp.int32, (KEY_TILE, 1), 0)
            ks_scr[rows, :] = _aug_keys(ksv_ref[0, rows, :], pos)
            kw_scr[rows, :] = _aug_keys(kwv_ref[0, rows, :], pos)

    lane = lax.broadcasted_iota(jnp.int32, (qb, LANES), 1)
    qpos = q0 + lax.broadcasted_iota(jnp.int32, (qb, 1), 0)
    qa = _aug_queries(q_ref[0] * (HEAD_DIM ** -0.5), slopes, lane)
    gates = jax.nn.sigmoid(gl_ref[0])

    perm_end = lambda i: (2 * (i % HEAD_DIM) + i // HEAD_DIM) * NSA_CMP_BLOCK + (NSA_CMP_BLOCK - 1)
    kca = _aug_keys(kvc_ref[0], perm_end(lax.broadcasted_iota(jnp.int32, (LANES, 1), 0)))
    cend = perm_end(lax.broadcasted_iota(jnp.int32, (1, LANES), 1))
    e, inv, o_c = _softmax_block(qa, kca, jnp.where(cend <= qpos, 0.0, NEG_INF))
    p_c = e * inv
    imp = _head_rows(p_c, 0)
    for h in range(1, N_HEADS):
        imp = imp + _head_rows(p_c, h)
    imp = imp + pltpu.roll(imp, HEAD_DIM, 1)

    cur = qpos // NSA_SEL_BLOCK
    forced = (lane == 0) | (lane > cur - NSA_LOCAL_BLOCKS)
    score = jnp.where(lane > cur, NEG_INF, jnp.where(forced, FORCE_SCORE, imp))
    score = jnp.where(lane < HEAD_DIM, score, -3e38)
    rank = jnp.zeros((qb, LANES), F32)
    for i in range(NSA_SEL_BLOCK):
        col = score[:, i:i + 1]
        beats = (col > score) | ((col == score) & (lane > i))
        rank = rank + jnp.where(beats, 1.0, 0.0)
    sel = jnp.where((rank < n_top) & (lane < HEAD_DIM), 1.0, 0.0).astype(BF16)

    def sel_tile(kt, carry):
        k0 = pl.multiple_of(kt * KEY_TILE, KEY_TILE)
        kpos = k0 + lax.broadcasted_iota(jnp.int32, (1, KEY_TILE), 1)
        member = jnp.dot(sel, exp_ref[kt], preferred_element_type=F32) > 0.5
        madd = jnp.where(member & (kpos <= qpos), 0.0, NEG_INF)
        return _flash_step(qa, ks_scr[pl.ds(k0, KEY_TILE), :], madd, *carry)

    n_tiles = (q0 + qb - 1) // KEY_TILE + 1
    _, l_s, acc_s = lax.fori_loop(0, n_tiles, sel_tile, _flash_init())
    o_s = acc_s * (1.0 / jnp.maximum(l_s, 1e-30))

    kst = pl.multiple_of(jnp.maximum(q0 + qb - n_win, 0), qb)
    dw = qpos - (kst + lax.broadcasted_iota(jnp.int32, (1, n_win), 1))
    wadd = jnp.where((dw >= 0) & (dw <= NSA_WINDOW), 0.0, NEG_INF)
    _, _, o_w = _softmax_block(qa, kw_scr[pl.ds(kst, n_win), :], wadd)

    outs = []
    for h in range(N_HEADS):
        g = [gates[:, 3 * h + i:3 * h + i + 1] for i in range(3)]
        outs.append(g[0] * _head_rows(o_c, h) + g[1] * _head_rows(o_s, h) + g[2] * _head_rows(o_w, h))
    o_ref[0] = _heads_out(outs, lane)


def _nsa_prompt(zb, kvc_perm, slopes):
    b, t, _ = zb.shape
    assert t % KEY_TILE == 0 and t // NSA_CMP_BLOCK <= LANES and t <= 256 * POS_SPLIT
    n_win = min(NSA_WINDOW + Q_BLOCK, t)
    n_sel = -(-t // NSA_SEL_BLOCK)
    n_kt = t // KEY_TILE
    key_blk = (np.arange(t) // NSA_SEL_BLOCK).reshape(n_kt, 1, KEY_TILE)
    expand = jnp.asarray((np.arange(LANES)[None, :, None] == key_blk).astype(np.float32), BF16)
    full = lambda c: pl.BlockSpec((1, t, LANES), lambda i, j, c=c: (i, 0, c))
    return pl.pallas_call(
        functools.partial(_nsa_kernel, t=t, n_win=n_win, n_top=min(NSA_TOPN, n_sel), slopes=slopes),
        grid=(b, t // Q_BLOCK),
        in_specs=[pl.BlockSpec((1, Q_BLOCK, 2 * LANES), lambda i, j: (i, j, 0)),
                  full(3), full(4),
                  pl.BlockSpec((1, Q_BLOCK, LANES), lambda i, j: (i, j, 5)),
                  pl.BlockSpec((1, LANES, LANES), lambda i, j: (i, 0, 0)),
                  pl.BlockSpec((n_kt, LANES, KEY_TILE), lambda i, j: (0, 0, 0))],
        out_specs=pl.BlockSpec((1, Q_BLOCK, GROUP_WIDTH), lambda i, j: (i, j, 0)),
        out_shape=jax.ShapeDtypeStruct((b, t, GROUP_WIDTH), F32),
        scratch_shapes=[pltpu.VMEM((t, 2 * LANES), BF16), pltpu.VMEM((t, 2 * LANES), BF16)],
        compiler_params=_cparams(("parallel", "arbitrary")),
        name="nsa_prompt",
    )(zb, zb, zb, zb, kvc_perm, expand)


def _nsa_compress(rows, lp):
    b, l = rows.shape[:2]
    nc = l // NSA_CMP_BLOCK
    blk = rows[:, :nc * NSA_CMP_BLOCK].reshape(b, nc, NSA_CMP_BLOCK, 2, HEAD_DIM) + lp['nsa_pe']
    flat = jnp.swapaxes(blk, 2, 3).reshape(b, nc, 2, NSA_CMP_BLOCK * HEAD_DIM)
    hid = jax.nn.gelu(jnp.einsum('bnci,cih->bnch', flat, lp['nsa_w1']))
    return jnp.einsum('bnch,chd->bncd', hid, lp['nsa_w2'])


def _dsa_keyprep_kernel(kv_ref, ki_ref, ka_ref, kip_ref):
    pos = pl.program_id(1) * KEY_TILE + lax.broadcasted_iota(jnp.int32, (KEY_TILE, 1), 0)
    ka_ref[0] = _aug_keys(kv_ref[0], pos)
    ki = ki_ref[0]
    if ki.shape[1] == LANES:
        ki = jnp.where(lax.broadcasted_iota(jnp.int32, ki.shape, 1) < IDX_DIM, ki, 0.0)
    else:
        ki = jnp.concatenate([ki, jnp.zeros((KEY_TILE, LANES - IDX_DIM), F32)], axis=1)
    hi = ki.astype(BF16).astype(F32)
    kip_ref[0] = (hi + pltpu.roll(ki - hi, IDX_DIM, 1) + pltpu.roll(hi, 2 * IDX_DIM, 1)).astype(BF16)


def _dsa_keyprep(kv_src, kv_col, ki_src, ki_col):
    b, t, _ = kv_src.shape
    ki_w = min(ki_src.shape[2], LANES)
    return pl.pallas_call(
        _dsa_keyprep_kernel,
        grid=(b, t // KEY_TILE),
        in_specs=[pl.BlockSpec((1, KEY_TILE, LANES), lambda i, c: (i, c, kv_col)),
                  pl.BlockSpec((1, KEY_TILE, ki_w), lambda i, c: (i, c, ki_col))],
        out_specs=[pl.BlockSpec((1, KEY_TILE, 2 * LANES), lambda i, c: (i, c, 0)),
                   pl.BlockSpec((1, KEY_TILE, LANES), lambda i, c: (i, c, 0))],
        out_shape=[jax.ShapeDtypeStruct((b, t, 2 * LANES), BF16), jax.ShapeDtypeStruct((b, t, LANES), BF16)],
        compiler_params=_cparams(("parallel", "parallel")),
        name="dsa_keyprep",
    )(kv_src, ki_src)


def _dsa_kernel(q_ref, ka_ref, qi_ref, kip_ref, wiq_ref, o_ref, skey_scr, *, t, qb, q_base, k_sel, slopes):
    n_sub = KEY_TILE // LANES
    ka_scr, kip_scr = ka_ref.at[0], kip_ref.at[0]
    lane = lax.broadcasted_iota(jnp.int32, (qb, LANES), 1)
    if q_base is None:
        q0 = pl.program_id(1) * qb
        qpos = q0 + lax.broadcasted_iota(jnp.int32, (qb, 1), 0)
        n_tiles = (q0 + qb - 1) // KEY_TILE + 1
    else:
        qpos = jnp.full((qb, 1), q_base, jnp.int32)
        n_tiles = t // KEY_TILE
    tile_pos = lambda kt: kt * KEY_TILE + lax.broadcasted_iota(jnp.int32, (1, KEY_TILE), 1)

    qi = qi_ref[0]
    wq = wiq_ref[0]
    packed_q = []
    for h in range(IDX_HEADS):
        qh = qi if h == 0 else pltpu.roll(qi, LANES - IDX_DIM * h, 1)
        qh = jnp.where(lane < IDX_DIM, qh, 0.0)
        hi = qh.astype(BF16).astype(F32)
        packed_q.append(hi + pltpu.roll(hi, IDX_DIM, 1) + pltpu.roll(qh - hi, 2 * IDX_DIM, 1))
    packed_q = jnp.concatenate(packed_q, axis=0).astype(BF16)
    w_cols = [wq[:, IDX_DIM + h:IDX_DIM + h + 1] for h in range(IDX_HEADS)]

    def idx_tile(kt, carry):
        k0 = pl.multiple_of(kt * KEY_TILE, KEY_TILE)
        rel = jnp.maximum(lax.dot_general(packed_q, kip_scr[pl.ds(k0, KEY_TILE), :], _NT,
                                          preferred_element_type=F32), 0.0)
        score = w_cols[0] * rel[:qb]
        for h in range(1, IDX_HEADS):
            score = score + w_cols[h] * rel[h * qb:(h + 1) * qb]
        score = jnp.where(score == 0.0, 0.0, score)
        score = jnp.where(tile_pos(kt) <= qpos, score, NEG_INF)
        bits = pltpu.bitcast(score, jnp.int32)
        skey_scr[kt] = bits ^ ((bits >> 31) & 0x7FFFFFFF)
        return carry

    lax.fori_loop(0, n_tiles, idx_tile, 0)

    kf = float(k_sel)

    def count(c, strict):
        cb = jnp.broadcast_to(c, (qb, LANES))

        def body(kt, acc):
            x = skey_scr[kt]
            for u in range(n_sub):
                xu = x[:, u * LANES:(u + 1) * LANES]
                acc = acc + jnp.where((xu > cb) if strict else (xu >= cb), 1.0, 0.0)
            return acc

        acc = lax.fori_loop(0, n_tiles, body, jnp.zeros((qb, LANES), F32))
        return jnp.sum(acc, axis=1, keepdims=True)

    int_min = jnp.int32(-2 ** 31)
    tau = jnp.where(count(jnp.zeros((qb, 1), jnp.int32), False) >= kf, jnp.int32(0), int_min)

    def bit_step(i, tau):
        cand = tau + lax.shift_left(jnp.int32(1), 30 - i)
        return jnp.where(count(cand, False) >= kf, cand, tau)

    tau = lax.fori_loop(0, 31, bit_step, tau)
    need = kf - count(tau, True)
    taub = jnp.broadcast_to(tau, (qb, LANES))
    ui = lax.broadcasted_iota(jnp.int32, (LANES, LANES), 0)
    uj = lax.broadcasted_iota(jnp.int32, (LANES, LANES), 1)
    upper = jnp.where(ui <= uj, 1.0, 0.0).astype(BF16)
    qa = _aug_queries(q_ref[0] * (HEAD_DIM ** -0.5), slopes, lane)

    def att_tile(kt, carry):
        ties, m, l, acc = carry
        k0 = pl.multiple_of(kt * KEY_TILE, KEY_TILE)
        x = skey_scr[kt]
        parts = []
        for u in range(n_sub):
            xu = x[:, u * LANES:(u + 1) * LANES]
            eq = xu == taub
            eqf = jnp.where(eq, 1.0, 0.0)
            before = jnp.dot(eqf.astype(BF16), upper, preferred_element_type=F32) + ties
            parts.append(jnp.where((xu > taub) | (eq & (before <= need)), 0.0, NEG_INF))
            ties = ties + jnp.sum(eqf, axis=1, keepdims=True)
        madd = jnp.where(tile_pos(kt) <= qpos, jnp.concatenate(parts, axis=1), NEG_INF)
        return (ties,) + _flash_step(qa, ka_scr[pl.ds(k0, KEY_TILE), :], madd, m, l, acc)

    _, _, l, acc = lax.fori_loop(0, n_tiles, att_tile, (jnp.zeros((qb, 1), F32),) + _flash_init(qb))
    o = acc * (1.0 / jnp.maximum(l, 1e-30))
    o_ref[0] = _heads_out([_head_rows(o, h, qb) for h in range(N_HEADS)], lane)


def _dsa_attend(zq, ka, kip, slopes, *, qb, q_base, n_keys):
    b, tq, _ = zq.shape
    t = ka.shape[1]
    assert t % KEY_TILE == 0 and t <= 256 * POS_SPLIT and tq % qb == 0
    k_sel = min(DSA_TOPK, n_keys // 4)
    assert k_sel <= KEY_TILE
    full = lambda w: pl.BlockSpec((1, t, w), lambda i, j: (i, 0, 0))
    qblk = lambda c: pl.BlockSpec((1, qb, LANES), lambda i, j, c=c: (i, j, c))
    return pl.pallas_call(
        functools.partial(_dsa_kernel, t=t, qb=qb, q_base=q_base, k_sel=k_sel, slopes=slopes),
        grid=(b, tq // qb),
        in_specs=[pl.BlockSpec((1, qb, 2 * LANES), lambda i, j: (i, j, 0)),
                  full(2 * LANES), qblk(3), full(LANES), qblk(4)],
        out_specs=pl.BlockSpec((1, qb, GROUP_WIDTH), lambda i, j: (i, j, 0)),
        out_shape=jax.ShapeDtypeStruct((b, tq, GROUP_WIDTH), F32),
        scratch_shapes=[pltpu.VMEM((t // KEY_TILE, qb, KEY_TILE), jnp.int32)],
        compiler_params=_cparams(("parallel", "parallel")),
        name="dsa_attend",
    )(zq, ka, zq, kip, zq)


def _dsa_prompt(zc, slopes):
    ka, kip = _dsa_keyprep(zc, 2, zc, 4)
    return _dsa_attend(zc, ka, kip, slopes, qb=Q_BLOCK, q_base=None, n_keys=zc.shape[1])


def _outproj_kernel(ya_ref, yb_ref, yc_ref, yd_ref, x_ref, g1_ref, sc2_ref, sh2_ref, wo_ref, lg_ref, lb_ref,
                    wr_ref, br_ref, x1_ref, h2_ref, logit_ref, *, alpha):
    mix = jnp.zeros(x_ref.shape[1:], F32)
    for i, ref in enumerate((ya_ref, yb_ref, yc_ref, yd_ref)):
        mix = mix + jnp.dot(ref[0].astype(BF16), wo_ref[i], preferred_element_type=F32)
    x1 = _ln_rows(alpha * x_ref[0] + g1_ref[0] * mix) * lg_ref[...] + lb_ref[...]
    x1_ref[0] = x1
    h2 = _ln_rows(x1) * (1.0 + sc2_ref[0]) + sh2_ref[0]
    h2_ref[0] = h2.astype(BF16)
    logit_ref[0] = jnp.dot(h2, wr_ref[...], preferred_element_type=F32,
                           precision=lax.Precision.HIGHEST) + br_ref[...]


def _outproj(ys, x, g1, sc2, sh2, wo, ln_g, ln_b, wr_pad, br_pad, alpha):
    b, t, d = x.shape
    tm = min(t, 512)
    per_row = g1.shape[1] == t and t > 1
    mod_blk = (1, tm, d) if per_row else (1, 1, d)
    mod_map = (lambda i, j: (i, j, 0)) if per_row else (lambda i, j: (i, 0, 0))
    mod = pl.BlockSpec(mod_blk, mod_map)
    row = lambda w: pl.BlockSpec((1, tm, w), lambda i, j: (i, j, 0))
    const = lambda a: pl.BlockSpec(a.shape, lambda i, j: (0,) * a.ndim)
    return pl.pallas_call(
        functools.partial(_outproj_kernel, alpha=alpha),
        grid=(b, t // tm),
        in_specs=[row(GROUP_WIDTH)] * 4 + [row(d), mod, mod, mod, const(wo), const(ln_g), const(ln_b),
                                           const(wr_pad), const(br_pad)],
        out_specs=[row(d), row(d), row(LANES)],
        out_shape=[jax.ShapeDtypeStruct((b, t, d), F32), jax.ShapeDtypeStruct((b, t, d), BF16),
                   jax.ShapeDtypeStruct((b, t, LANES), F32)],
        compiler_params=_cparams(("parallel", "parallel")),
        name="outproj",
    )(*ys, x, g1, sc2, sh2, wo, ln_g, ln_b, wr_pad, br_pad)


def _moe_kernel(be_ref, nu_ref, x_ref, wgu_ref, bgu_ref, wd_ref, bd_ref, o_ref, wgu_bf, wd_bf):
    i = pl.program_id(0)
    ff = wd_ref.shape[1]

    @pl.when((i == 0) | (be_ref[i] != be_ref[jnp.maximum(i - 1, 0)]))
    def _():
        wgu_bf[...] = wgu_ref[0].astype(BF16)
        wd_bf[...] = wd_ref[0].astype(BF16)

    @pl.when(i < nu_ref[0])
    def _():
        gu = jnp.dot(x_ref[...], wgu_bf[...], preferred_element_type=F32) + bgu_ref[0]
        gt = jnp.minimum(gu[:, :ff], SWIGLU_LIMIT)
        up = jnp.clip(gu[:, ff:], -SWIGLU_LIMIT, SWIGLU_LIMIT)
        act = (up + 1.0) * gt * jax.nn.sigmoid(SWIGLU_ALPHA * gt)
        o_ref[...] = jnp.dot(act.astype(BF16), wd_bf[...], preferred_element_type=F32) + bd_ref[0]

    @pl.when(i >= nu_ref[0])
    def _():
        o_ref[...] = jnp.zeros(o_ref.shape, F32)


def _moe_experts(xg, blk_exp, n_used, wgu, bgu, wd, bd, blk):
    cap, d = xg.shape
    e, _, ff2 = wgu.shape
    ff = ff2 // 2
    grid_spec = pltpu.PrefetchScalarGridSpec(
        num_scalar_prefetch=2,
        grid=(cap // blk,),
        in_specs=[pl.BlockSpec((blk, d), lambda i, be, nu: (i, 0)),
                  pl.BlockSpec((1, d, ff2), lambda i, be, nu: (be[i], 0, 0)),
                  pl.BlockSpec((1, 1, ff2), lambda i, be, nu: (be[i], 0, 0)),
                  pl.BlockSpec((1, ff, d), lambda i, be, nu: (be[i], 0, 0)),
                  pl.BlockSpec((1, 1, d), lambda i, be, nu: (be[i], 0, 0))],
        out_specs=pl.BlockSpec((blk, d), lambda i, be, nu: (i, 0)),
        scratch_shapes=[pltpu.VMEM((d, ff2), BF16), pltpu.VMEM((ff, d), BF16)],
    )
    return pl.pallas_call(
        _moe_kernel,
        grid_spec=grid_spec,
        out_shape=jax.ShapeDtypeStruct((cap, d), F32),
        compiler_params=_cparams(("arbitrary",)),
        name="moe_experts",
    )(blk_exp, n_used, xg, wgu, bgu.reshape(e, 1, ff2), wd, bd.reshape(e, 1, d))


def _slot_tokens_kernel(pos_ref, out_ref, *, chunk, shift):
    c = pl.program_id(0)

    @pl.when(c == 0)
    def _():
        def zero(i, carry):
            out_ref[i] = jnp.int32(0)
            return carry
        lax.fori_loop(0, out_ref.shape[0], zero, 0, unroll=8)

    base = c * chunk

    def body(i, carry):
        out_ref[pos_ref[i]] = lax.shift_right_logical(base + i, shift)
        return carry

    lax.fori_loop(0, chunk, body, 0, unroll=8)


def _slot_tokens(pos, cap):
    m = pos.shape[0]
    assert TOP_K & (TOP_K - 1) == 0
    chunk = min(m, 8192)
    assert m % chunk == 0
    return pl.pallas_call(
        functools.partial(_slot_tokens_kernel, chunk=chunk, shift=TOP_K.bit_length() - 1),
        grid=(m // chunk,),
        in_specs=[pl.BlockSpec((chunk,), lambda c: (c,), memory_space=pltpu.SMEM)],
        out_specs=pl.BlockSpec((cap,), lambda c: (0,), memory_space=pltpu.SMEM),
        out_shape=jax.ShapeDtypeStruct((cap,), jnp.int32),
        compiler_params=pltpu.CompilerParams(dimension_semantics=("arbitrary",)),
        name="slot_tokens",
    )(pos)


def _moe(h2, logits, mw, blk):
    n, d = h2.shape
    n_exp = logits.shape[1]
    top_v, top_e = lax.top_k(logits, TOP_K)
    gate = jax.nn.softmax(top_v, axis=-1)
    m = n * TOP_K
    cap = -(-(m + n_exp * (blk - 1)) // blk) * blk
    e_flat = top_e.reshape(m)
    onehot = (e_flat[:, None] == jnp.arange(n_exp)[None, :]).astype(jnp.int32)
    csum = jnp.cumsum(onehot, axis=0)
    counts = csum[-1]
    rank = jnp.take_along_axis(csum, e_flat[:, None], axis=1)[:, 0] - 1
    padded = (counts + blk - 1) // blk * blk
    pend = jnp.cumsum(padded)
    pstart = pend - padded
    pos = (pstart[e_flat] + rank).astype(jnp.int32)
    nblk = cap // blk
    blk_start = jnp.arange(nblk, dtype=jnp.int32) * blk
    blk_exp = jnp.minimum(jnp.sum((pend[None, :] <= blk_start[:, None]).astype(jnp.int32), axis=1), n_exp - 1)
    row_tok = _slot_tokens(pos, cap)
    n_used = (pend[-1] // blk).astype(jnp.int32).reshape(1)
    xg = h2.at[row_tok].get(mode='promise_in_bounds')
    out = _moe_experts(xg, blk_exp.astype(jnp.int32), n_used, *mw, blk)
    pos = pos.reshape(n, TOP_K)
    picked = [out.at[pos[:, j]].get(mode='promise_in_bounds') for j in range(TOP_K)]
    return picked, jnp.pad(gate, ((0, 0), (0, LANES - TOP_K)))


def _moe_block_rows(n_assign, n_exp):
    per_expert = max(1, n_assign // n_exp)
    return min(512, max(16, 1 << (per_expert.bit_length() - 1)))


def _resln_kernel(x_ref, gate_ref, *refs, alpha):
    picked, (g_ref, lg_ref, lb_ref, o_ref) = refs[:TOP_K], refs[TOP_K:]
    gate = gate_ref[0]
    y = gate[:, 0:1] * picked[0][0]
    for j in range(1, TOP_K):
        y = y + gate[:, j:j + 1] * picked[j][0]
    o_ref[0] = _ln_rows(alpha * x_ref[0] + g_ref[0] * y) * lg_ref[...] + lb_ref[...]


def _resln(x, picked, gate, g, ln_g, ln_b, alpha):
    b, t, d = x.shape
    tm = min(t, 512)
    per_row = g.shape[1] == t and t > 1
    mod = pl.BlockSpec((1, tm, d) if per_row else (1, 1, d),
                       (lambda i, j: (i, j, 0)) if per_row else (lambda i, j: (i, 0, 0)))
    row = lambda w: pl.BlockSpec((1, tm, w), lambda i, j: (i, j, 0))
    const = pl.BlockSpec((1, d), lambda i, j: (0, 0))
    return pl.pallas_call(
        functools.partial(_resln_kernel, alpha=alpha),
        grid=(b, t // tm),
        in_specs=[row(d), row(LANES)] + [row(d)] * TOP_K + [mod, const, const],
        out_specs=row(d),
        out_shape=jax.ShapeDtypeStruct((b, t, d), F32),
        compiler_params=_cparams(("parallel", "parallel")),
        name="resln",
    )(x, gate.reshape(b, t, LANES), *[p.reshape(b, t, d) for p in picked], g, ln_g, ln_b)


def _conv_mixer(zd, buf, conv_w):
    t = zd.shape[1]
    ch = GROUP_WIDTH
    b_gate, c_gate, xin = zd[..., :ch], zd[..., ch:2 * ch], zd[..., 2 * ch:3 * ch]
    ue = jnp.concatenate([buf, c_gate * xin], axis=1)
    y = sum(ue[:, j:j + t] * conv_w[j] for j in range(CONV_K))
    return b_gate * y, ue[:, t:]


def _nsa_compress_pages(pages, lp):
    b, p, _, dh, rows = pages.shape
    nj = rows // NSA_CMP_BLOCK
    x = pages.reshape(b, p, 2, dh, nj, NSA_CMP_BLOCK) + lp['nsa_pe'].transpose(1, 2, 0)[:, :, None, :]
    w1 = lp['nsa_w1'].reshape(2, NSA_CMP_BLOCK, dh, -1)
    hid = jax.nn.gelu(jnp.einsum('bpcdjr,crdh->bpjch', x, w1))
    return jnp.einsum('bpjch,chd->bpjcd', hid, lp['nsa_w2']).reshape(b, p * nj, 2, dh)


def _masked_softmax(s, valid):
    p = jax.nn.softmax(jnp.where(valid, s, NEG_INF), axis=-1)
    return jnp.where(valid, p, 0.0)


def _nsa_attend_sample(q, gates, qpos, kvc, slc_pool, page_table, new_slc, kvw, wpos, slopes, n_keys):
    b, tq = q.shape[:2]
    q = q * HEAD_DIM ** -0.5
    sl = jnp.asarray(slopes, F32)[:, None, None]
    nc = kvc.shape[1]
    cend = jnp.arange(nc) * NSA_CMP_BLOCK + (NSA_CMP_BLOCK - 1)
    s_c = jnp.einsum('bqhd,bnd->bhqn', q, kvc[:, :, 0]) - sl * (qpos[:, None] - cend[None, :]).astype(F32)
    p_c = _masked_softmax(s_c, cend[None, :] <= qpos[:, None])
    o_c = jnp.einsum('bhqn,bnd->bqhd', p_c, kvc[:, :, 1])
    ratio = NSA_SEL_BLOCK // NSA_CMP_BLOCK
    n_sel = -(-n_keys // NSA_SEL_BLOCK)
    imp = jnp.pad(p_c.sum(1), ((0, 0), (0, 0), (0, n_sel * ratio - nc))).reshape(b, tq, n_sel, ratio).sum(-1)
    blk = jnp.arange(n_sel)[None, :]
    cur = (qpos // NSA_SEL_BLOCK)[:, None]
    forced = (blk == 0) | (blk > cur - NSA_LOCAL_BLOCKS)
    score = jnp.where(blk > cur, NEG_INF, jnp.where(forced, FORCE_SCORE, imp))
    n_top = min(NSA_TOPN, n_sel)
    _, sel = lax.top_k(score, n_top)
    assert tq == 1 and PAGE_SIZE == 2 * NSA_SEL_BLOCK
    sel = sel[:, 0]
    n_pages = page_table.shape[1]
    n_past = n_pages * PAGE_SIZE
    pages = jnp.take_along_axis(page_table, jnp.minimum(sel // 2, n_pages - 1), axis=1)
    slabs = slc_pool[pages]
    odd = (sel % 2 == 1)[:, :, None, None, None]
    half = jnp.where(odd, slabs[..., NSA_SEL_BLOCK:], slabs[..., :NSA_SEL_BLOCK])
    kpos = sel[:, :, None] * NSA_SEL_BLOCK + jnp.arange(NSA_SEL_BLOCK)
    q1 = q[:, 0]
    s_past = jnp.einsum('bhd,bndr->bhnr', q1, half[:, :, 0]) + sl * kpos[:, None].astype(F32)
    s_past = jnp.where((kpos < n_past)[:, None], s_past, NEG_INF).reshape(b, N_HEADS, -1)
    new_sel = jnp.any(sel == n_past // NSA_SEL_BLOCK, axis=1)[:, None, None]
    s_new = jnp.einsum('bhd,bd->bh', q1, new_slc[:, 0, 0])[..., None] + sl[None, :, 0] * float(n_past)
    s_all = jnp.concatenate([s_past, jnp.where(new_sel, s_new, NEG_INF)], axis=-1)
    p_s = _masked_softmax(s_all, s_all > 0.5 * NEG_INF)
    o_s = (jnp.einsum('bhnr,bndr->bhd', p_s[..., :-1].reshape(b, N_HEADS, n_top, NSA_SEL_BLOCK), half[:, :, 1])
           + p_s[..., -1:] * new_slc[:, 0, 1][:, None, :])[:, None]
    dw = qpos[:, None] - wpos[None, :]
    s_w = jnp.einsum('bqhd,bnd->bhqn', q, kvw[:, :, 0]) - sl * dw.astype(F32)
    p_w = _masked_softmax(s_w, (dw >= 0) & (dw <= NSA_WINDOW) & (wpos[None, :] >= 0))
    o_w = jnp.einsum('bhqn,bnd->bqhd', p_w, kvw[:, :, 1])
    return gates[..., 0:1] * o_c + gates[..., 1:2] * o_s + gates[..., 2:3] * o_w


def _dsa_sample_kernel(pt_ref, zt_ref, kv_hbm, idx_hbm, o_ref, kvbuf, idxbuf, sem, score_scr, s_scr, p_scr,
                       *, layer, n_pages, k_sel, slopes):
    b = pl.program_id(0)
    nb = pl.num_programs(0)
    slot = b % 2
    ps = PAGE_SIZE
    gw = GROUP_WIDTH

    def page_copies(seq, sl, p):
        pg = pt_ref[seq, p]
        return (pltpu.make_async_copy(kv_hbm.at[layer, pg], kvbuf.at[sl, p], sem.at[0, sl]),
                pltpu.make_async_copy(idx_hbm.at[layer, pg], idxbuf.at[sl, p], sem.at[1, sl]))

    def fetch(seq, sl):
        def body(p, carry):
            for cp in page_copies(seq, sl, p):
                cp.start()
            return carry
        lax.fori_loop(0, n_pages, body, 0)

    @pl.when(b == 0)
    def _():
        fetch(0, 0)

    @pl.when(b + 1 < nb)
    def _():
        fetch(b + 1, 1 - slot)

    def wait_body(p, carry):
        for cp in page_copies(b, slot, p):
            cp.wait()
        return carry

    lax.fori_loop(0, n_pages, wait_body, 0)

    zt = zt_ref[0]
    o_kv = gw
    o_qi = gw + 2 * HEAD_DIM
    o_ki = o_qi + IDX_HEADS * IDX_DIM
    o_wi = o_ki + IDX_DIM
    q_cols = [zt[h * HEAD_DIM:(h + 1) * HEAD_DIM] * (HEAD_DIM ** -0.5) for h in range(N_HEADS)]
    qi_cols = [zt[o_qi + h * IDX_DIM:o_qi + (h + 1) * IDX_DIM] for h in range(IDX_HEADS)]
    w_idx = [zt[o_wi + h:o_wi + h + 1] for h in range(IDX_HEADS)]
    k_new, v_new = zt[o_kv:o_kv + HEAD_DIM], zt[o_kv + HEAD_DIM:o_kv + 2 * HEAD_DIM]
    ki_new = zt[o_ki:o_ki + IDX_DIM]
    colsum = lambda x: jnp.sum(x, axis=0, keepdims=True)
    total = lambda x: jnp.sum(colsum(x), axis=1, keepdims=True)
    unzero = lambda x: jnp.where(x == 0.0, 0.0, x)

    def order_key(x):
        bits = pltpu.bitcast(x, jnp.int32)
        return bits ^ ((bits >> 31) & 0x7FFFFFFF)

    qi_wide = [jnp.broadcast_to(c, (IDX_DIM, ps)) for c in qi_cols]
    w_wide = [jnp.broadcast_to(w, (1, ps)) for w in w_idx]

    def score_page(p, carry):
        kit = idxbuf[slot, p]
        row = w_wide[0] * jnp.maximum(colsum(kit * qi_wide[0]), 0.0)
        for h in range(1, IDX_HEADS):
            row = row + w_wide[h] * jnp.maximum(colsum(kit * qi_wide[h]), 0.0)
        score_scr[pl.ds(p, 1), :] = row
        return carry

    lax.fori_loop(0, n_pages, score_page, 0)
    sc_new = w_idx[0] * jnp.maximum(colsum(ki_new * qi_cols[0]), 0.0)
    for h in range(1, IDX_HEADS):
        sc_new = sc_new + w_idx[h] * jnp.maximum(colsum(ki_new * qi_cols[h]), 0.0)
    key = order_key(unzero(score_scr[...]))
    key_new = order_key(unzero(sc_new))

    kf = float(k_sel)

    def count(c, strict):
        past = jnp.where((key > c) if strict else (key >= c), 1.0, 0.0)
        new = jnp.where((key_new > c) if strict else (key_new >= c), 1.0, 0.0)
        return total(past) + new

    int_min = jnp.int32(-2 ** 31)
    tau = jnp.where(count(jnp.zeros((1, 1), jnp.int32), False) >= kf, jnp.int32(0), int_min)

    def bit_step(i, tau):
        cand = tau + lax.shift_left(jnp.int32(1), 30 - i)
        return jnp.where(count(cand, False) >= kf, cand, tau)

    tau = lax.fori_loop(0, 31, bit_step, tau)
    need = kf - count(tau, True)
    eq = key == tau
    eqf = jnp.where(eq, 1.0, 0.0).astype(BF16)
    ri = lax.broadcasted_iota(jnp.int32, (ps, ps), 0)
    ci = lax.broadcasted_iota(jnp.int32, (ps, ps), 1)
    in_row = jnp.dot(eqf, jnp.where(ri <= ci, 1.0, 0.0).astype(BF16), preferred_element_type=F32)
    row_tot = jnp.broadcast_to(in_row[:, ps - 1:ps], (n_pages, ps)).astype(BF16)
    pi = lax.broadcasted_iota(jnp.int32, (n_pages, n_pages), 0)
    pj = lax.broadcasted_iota(jnp.int32, (n_pages, n_pages), 1)
    rows_before = jnp.dot(jnp.where(pj < pi, 1.0, 0.0).astype(BF16), row_tot, preferred_element_type=F32)
    sel = (key > tau) | (eq & (in_row + rows_before <= need))
    eq_new = key_new == tau
    sel_new = (key_new > tau) | (eq_new & (total(jnp.where(eq, 1.0, 0.0)) + 1.0 <= need))

    q_wide = [jnp.broadcast_to(c, (HEAD_DIM, ps)) for c in q_cols]

    def qk_page(p, carry):
        kt = kvbuf[slot, p, 0]
        for h in range(N_HEADS):
            s_scr[h, pl.ds(p, 1), :] = colsum(kt * q_wide[h])
        return carry

    lax.fori_loop(0, n_pages, qk_page, 0)
    kpos = (lax.broadcasted_iota(jnp.int32, (n_pages, ps), 0) * ps
            + lax.broadcasted_iota(jnp.int32, (n_pages, ps), 1)).astype(F32)
    e_new, inv = [], []
    for h in range(N_HEADS):
        s = jnp.where(sel, s_scr[h] + slopes[h] * kpos, NEG_INF)
        s_n = jnp.where(sel_new, colsum(k_new * q_cols[h]) + slopes[h] * float(n_pages * ps), NEG_INF)
        m = jnp.maximum(jnp.max(jnp.max(s, axis=0, keepdims=True), axis=1, keepdims=True), s_n)
        e = jnp.exp(s - m)
        e_n = jnp.exp(s_n - m)
        p_scr[h] = e
        e_new.append(e_n)
        inv.append(1.0 / (total(e) + e_n))

    def pv_page(p, accs):
        vt = kvbuf[slot, p, 1]
        return tuple(accs[h] + vt * p_scr[h, pl.ds(p, 1), :] for h in range(N_HEADS))

    accs = lax.fori_loop(0, n_pages, pv_page, tuple(jnp.zeros((HEAD_DIM, ps), F32) for _ in range(N_HEADS)))
    outs = [(jnp.sum(accs[h], axis=1, keepdims=True) + e_new[h] * v_new) * inv[h] for h in range(N_HEADS)]
    o_ref[0] = jnp.concatenate(outs, axis=0)


def _dsa_sample(zc, cache_dsa_kv, cache_dsa_idx, layer, page_table, slopes):
    b, t, c_pad = zc.shape
    assert t == 1
    n_pages = page_table.shape[1]
    n_keys = n_pages * PAGE_SIZE + t
    k_sel = min(DSA_TOPK, n_keys // 4)
    kv_t = cache_dsa_kv.transpose(0, 1, 3, 4, 2)
    idx_t = cache_dsa_idx.transpose(0, 1, 3, 2)
    grid_spec = pltpu.PrefetchScalarGridSpec(
        num_scalar_prefetch=1,
        grid=(b,),
        in_specs=[pl.BlockSpec((1, c_pad, 1), lambda i, pt: (i, 0, 0)),
                  pl.BlockSpec(memory_space=pl.ANY), pl.BlockSpec(memory_space=pl.ANY)],
        out_specs=pl.BlockSpec((1, GROUP_WIDTH, 1), lambda i, pt: (i, 0, 0)),
        scratch_shapes=[pltpu.VMEM((2, n_pages, 2, HEAD_DIM, PAGE_SIZE), F32),
                        pltpu.VMEM((2, n_pages, IDX_DIM, PAGE_SIZE), F32),
                        pltpu.SemaphoreType.DMA((2, 2)),
                        pltpu.VMEM((n_pages, PAGE_SIZE), F32),
                        pltpu.VMEM((N_HEADS, n_pages, PAGE_SIZE), F32),
                        pltpu.VMEM((N_HEADS, n_pages, PAGE_SIZE), F32)],
    )
    out = pl.pallas_call(
        functools.partial(_dsa_sample_kernel, layer=layer, n_pages=n_pages, k_sel=k_sel, slopes=slopes),
        grid_spec=grid_spec,
        out_shape=jax.ShapeDtypeStruct((b, GROUP_WIDTH, 1), F32),
        compiler_params=_cparams(("arbitrary",)),
        name="dsa_sample",
    )(page_table, zc.transpose(0, 2, 1), kv_t, idx_t)
    return out.transpose(0, 2, 1)


def _kv_rows(u):
    return u.reshape(u.shape[0], u.shape[1], 2, HEAD_DIM)


def _mixers_prompt(za, zb, zc, zd, lp, sl_nsa, sl_dsa):
    b, t, _ = za.shape
    gw = GROUP_WIDTH
    ya, s_rwkv, shift = _rwkv_mixer(za, jnp.zeros((b, A_PAD), F32),
                                    jnp.zeros((b, N_HEADS, HEAD_DIM, HEAD_DIM), F32), lp)
    kv_c = _kv_rows(zb[..., gw:gw + 2 * HEAD_DIM])
    kv_s = _kv_rows(zb[..., gw + 2 * HEAD_DIM:gw + 4 * HEAD_DIM])
    kv_w = _kv_rows(zb[..., gw + 4 * HEAD_DIM:gw + 6 * HEAD_DIM])
    kvc = _nsa_compress(kv_c, lp).reshape(b, -1, 2 * HEAD_DIM)
    nc = kvc.shape[1]
    kvc = jnp.pad(kvc, ((0, 0), (0, LANES - nc), (0, 0)))
    kvc = jnp.concatenate([kvc[:, 0::2], kvc[:, 1::2]], axis=1)
    yb = _nsa_prompt(zb, kvc, sl_nsa)
    yc = _dsa_prompt(zc, sl_dsa)
    kv_d = _kv_rows(zc[..., gw:gw + 2 * HEAD_DIM])
    o = gw + 2 * HEAD_DIM + IDX_HEADS * IDX_DIM
    ki = zc[..., o:o + IDX_DIM]
    yd, conv_buf = _conv_mixer(zd, jnp.zeros((b, CONV_K - 1, gw), F32), lp['conv_w'])
    return (ya, yb, yc, yd), (s_rwkv, shift, kv_c, kv_s, kv_w[:, -min(NSA_WINDOW, t):], kv_d, ki, conv_buf)


def _mixers_sample(za, zb, zc, zd, lp, sl_nsa, sl_dsa, s_rwkv, s_shift, cache_nsa_cmp, cache_nsa_slc, s_win,
                   cache_dsa_kv, cache_dsa_idx, layer, s_conv, page_table):
    b, t, _ = za.shape
    gw = GROUP_WIDTH
    n_past = page_table.shape[1] * PAGE_SIZE
    n_buf = s_win.shape[1]
    qpos = n_past + jnp.arange(t)
    shift_prev = jnp.pad(s_shift, ((0, 0), (0, A_PAD - A_COLS)))
    ya, s_rwkv_new, shift = _rwkv_mixer(za, shift_prev, s_rwkv, lp)
    q = zb[..., :gw].reshape(b, t, N_HEADS, HEAD_DIM)
    kv_c = _kv_rows(zb[..., gw:gw + 2 * HEAD_DIM])
    kv_s = _kv_rows(zb[..., gw + 2 * HEAD_DIM:gw + 4 * HEAD_DIM])
    kv_w = _kv_rows(zb[..., gw + 4 * HEAD_DIM:gw + 6 * HEAD_DIM])
    o = gw + 6 * HEAD_DIM
    gates = jax.nn.sigmoid(zb[..., o:o + 3 * N_HEADS]).reshape(b, t, N_HEADS, 3)
    assert (n_past + t) // NSA_CMP_BLOCK * NSA_CMP_BLOCK == n_past
    kvc = _nsa_compress_pages(cache_nsa_cmp.transpose(0, 1, 3, 4, 2)[layer][page_table], lp)
    kvw = jnp.concatenate([s_win, kv_w], axis=1)
    wpos = n_past - n_buf + jnp.arange(n_buf + t)
    yb = _nsa_attend_sample(q, gates, qpos, kvc, cache_nsa_slc.transpose(0, 1, 3, 4, 2)[layer], page_table, kv_s,
                            kvw, wpos, sl_nsa, n_past + t)
    kv_d = _kv_rows(zc[..., gw:gw + 2 * HEAD_DIM])
    o = gw + 2 * HEAD_DIM + IDX_HEADS * IDX_DIM
    ki = zc[..., o:o + IDX_DIM]
    yc = _dsa_sample(zc, cache_dsa_kv, cache_dsa_idx, layer, page_table, sl_dsa)
    yd, conv_buf = _conv_mixer(zd, s_conv, lp['conv_w'])
    flat = lambda u: u.reshape(b, t, gw)
    return (ya, flat(yb), flat(yc), yd), (s_rwkv_new, shift, kv_c, kv_s, kvw[:, -n_buf:], kv_d, ki, conv_buf)


def _layer(x, mod, lp, pw, mixer_fn, alpha):
    b, t, d = x.shape
    sh1, sc1, g1, sh2, sc2, g2 = mod
    zs = _inproj(x, sc1, sh1, pw['w_in'])
    ys, states = mixer_fn(*zs)
    x1, h2, logits = _outproj(ys, x, g1, sc2, sh2, pw['w_out'], pw['ln1_g'], pw['ln1_b'],
                              pw['wr'], pw['br'], alpha)
    n_exp = lp['moe_wr'].shape[1]
    blk = _moe_block_rows(b * t * TOP_K, n_exp)
    picked, gate = _moe(h2.reshape(b * t, d), logits.reshape(b * t, LANES)[:, :n_exp], pw['moe'], blk)
    x2 = _resln(x1, picked, gate, g2, pw['ln2_g'], pw['ln2_b'], alpha)
    return x2, states


def _prep_weights(lp):
    d = lp['w_in'].shape[0]
    cols, o = [], 0
    for c, p in ((A_COLS, A_PAD), (B_COLS, B_PAD), (C_COLS, C_PAD), (D_COLS, D_PAD)):
        cols.append(jnp.pad(lp['w_in'][:, o:o + c], ((0, 0), (0, p - c))))
        o += c
    n_exp = lp['moe_wr'].shape[1]
    row = lambda u: u.reshape(1, -1)
    return {
        'w_in': jnp.concatenate(cols, axis=1).astype(BF16),
        'w_out': lp['w_out'].reshape(4, GROUP_WIDTH, d).astype(BF16),
        'ln1_g': row(lp['ln1_g']), 'ln1_b': row(lp['ln1_b']),
        'ln2_g': row(lp['ln2_g']), 'ln2_b': row(lp['ln2_b']),
        'wr': jnp.pad(lp['moe_wr'], ((0, 0), (0, LANES - n_exp))),
        'br': jnp.pad(lp['moe_br'], (0, LANES - n_exp)).reshape(1, LANES),
        'moe': (lp['moe_wgu'], lp['moe_bgu'], lp['moe_wd'], lp['moe_bd']),
    }


def kernel(x_prompt, x_sample, state_rwkv, state_rwkv_shift, cache_nsa_cmp, cache_nsa_slc, state_nsa_win, cache_dsa_kv, cache_dsa_idx, state_conv, page_table, c_prompt, c_sample, w_ada, b_ada, w_in, w_out, ln1_g, ln1_b, ln2_g, ln2_b, rwkv_mu, rwkv_w0, rwkv_w2, rwkv_a0, rwkv_a2, rwkv_g2, rwkv_kk, rwkv_ka, rwkv_rk, rwkv_gn_g, rwkv_gn_b, nsa_pe, nsa_w1, nsa_w2, conv_w, moe_wr, moe_br, moe_wgu, moe_bgu, moe_wd, moe_bd):
    depth = w_in.shape[0]
    alpha = (2 * depth) ** 0.25
    sl_nsa, sl_dsa = _alibi_slopes()
    bp = x_prompt.shape[0]
    hp, hs = x_prompt, x_sample
    c_all = jnp.concatenate([c_prompt, c_sample], axis=0)
    states_p, states_s = [], []
    for l in range(depth):
        lp = {
            'w_in': w_in[l], 'w_out': w_out[l], 'ln1_g': ln1_g[l], 'ln1_b': ln1_b[l],
            'ln2_g': ln2_g[l], 'ln2_b': ln2_b[l],
            'rwkv_mu': jnp.pad(rwkv_mu[l], (0, A_PAD - A_COLS)), 'rwkv_w0': rwkv_w0[l], 'rwkv_w2': rwkv_w2[l],
            'rwkv_a0': rwkv_a0[l], 'rwkv_a2': rwkv_a2[l], 'rwkv_g2': rwkv_g2[l],
            'rwkv_kk': rwkv_kk[l], 'rwkv_ka': rwkv_ka[l], 'rwkv_rk': rwkv_rk[l],
            'rwkv_gn_g': rwkv_gn_g[l], 'rwkv_gn_b': rwkv_gn_b[l],
            'nsa_pe': nsa_pe[l], 'nsa_w1': nsa_w1[l], 'nsa_w2': nsa_w2[l], 'conv_w': conv_w[l],
            'moe_wr': moe_wr[l], 'moe_br': moe_br[l], 'moe_wgu': moe_wgu[l], 'moe_bgu': moe_bgu[l],
            'moe_wd': moe_wd[l], 'moe_bd': moe_bd[l],
        }
        pw = _prep_weights(lp)
        mod_all = jax.nn.silu(c_all) @ w_ada[l] + b_ada[l]
        mod_p = [u[:, None, :] for u in jnp.split(mod_all[:bp], 6, axis=-1)]
        mod_s = [u[:, None, :] for u in jnp.split(mod_all[bp:], 6, axis=-1)]
        mix_p = functools.partial(_mixers_prompt, lp=lp, sl_nsa=sl_nsa, sl_dsa=sl_dsa)
        mix_s = functools.partial(
            _mixers_sample, lp=lp, sl_nsa=sl_nsa, sl_dsa=sl_dsa, s_rwkv=state_rwkv[l],
            s_shift=state_rwkv_shift[l], cache_nsa_cmp=cache_nsa_cmp, cache_nsa_slc=cache_nsa_slc,
            s_win=state_nsa_win[l], cache_dsa_kv=cache_dsa_kv, cache_dsa_idx=cache_dsa_idx, layer=l,
            s_conv=state_conv[l], page_table=page_table)
        hs, st_s = _layer(hs, mod_s, lp, pw, mix_s, alpha)
        hp, st_p = _layer(hp, mod_p, lp, pw, mix_p, alpha)
        states_p.append(st_p)
        states_s.append(st_s)
    outs_p = [jnp.stack(v) for v in zip(*states_p)]
    outs_s = [jnp.stack(v) for v in zip(*states_s)]
    return (hp, hs, *outs_p, *outs_s)
```

```python
import functools

import jax
import jax.numpy as jnp
import numpy as np
from jax import lax
from jax.experimental import pallas as pl
from jax.experimental.pallas import tpu as pltpu

F32 = jnp.float32
BF16 = jnp.bfloat16

HEAD_DIM = 64
N_HEADS = 4
GROUP_WIDTH = N_HEADS * HEAD_DIM
RWKV_DECAY_SCALE = 0.606531
RWKV_GN_EPS = 64e-5
NSA_CMP_BLOCK = 32
NSA_SEL_BLOCK = 64
NSA_TOPN = 16
NSA_LOCAL_BLOCKS = 2
NSA_WINDOW = 512
DSA_TOPK = 256
IDX_HEADS = 4
IDX_DIM = 32
CONV_K = 3
TOP_K = 4
SWIGLU_LIMIT = 7.0
SWIGLU_ALPHA = 1.702
NEG_INF = -1e30
FORCE_SCORE = 1e4
Q_BLOCK = 128
PAGE_SIZE = 128

LANES = 128
A_COLS = 3 * GROUP_WIDTH + 32 + 32 + 64
B_COLS = GROUP_WIDTH + 6 * HEAD_DIM + 3 * N_HEADS
C_COLS = GROUP_WIDTH + 2 * HEAD_DIM + IDX_HEADS * IDX_DIM + IDX_DIM + IDX_HEADS
D_COLS = 3 * GROUP_WIDTH


def _pad_to(n, m):
    return -(-n // m) * m


A_PAD, B_PAD, C_PAD, D_PAD = (_pad_to(c, LANES) for c in (A_COLS, B_COLS, C_COLS, D_COLS))
VMEM_LIMIT = 56 * 1024 * 1024

_NT = (((1,), (1,)), ((), ()))


def _alibi_slopes():
    n = 2 * N_HEADS
    s = [2.0 ** (-8.0 * i / n) for i in range(1, n + 1)]
    return s[0::2], s[1::2]


def _ln_rows(x, eps=1e-5):
    mu = jnp.mean(x, axis=-1, keepdims=True)
    xc = x - mu
    var = jnp.mean(xc * xc, axis=-1, keepdims=True)
    return xc * lax.rsqrt(var + eps)


def _cparams(sem):
    return pltpu.CompilerParams(dimension_semantics=sem, vmem_limit_bytes=VMEM_LIMIT)


def _inproj_kernel(x_ref, sc_ref, sh_ref, w_ref, za_ref, zb_ref, zc_ref, zd_ref):
    h = _ln_rows(x_ref[0]) * (1.0 + sc_ref[0]) + sh_ref[0]
    z = jnp.dot(h.astype(BF16), w_ref[...], preferred_element_type=F32)
    o = 0
    for ref, w in ((za_ref, A_PAD), (zb_ref, B_PAD), (zc_ref, C_PAD), (zd_ref, D_PAD)):
        ref[0] = z[:, o:o + w]
        o += w


def _inproj(x, sc, sh, w_pad):
    b, t, d = x.shape
    tm = min(t, 512)
    mod_rows = sc.shape[1]
    mod_blk = (1, tm, d) if mod_rows == t and t > 1 else (1, 1, d)
    mod_map = (lambda i, j: (i, j, 0)) if mod_rows == t and t > 1 else (lambda i, j: (i, 0, 0))
    widths = (A_PAD, B_PAD, C_PAD, D_PAD)
    return pl.pallas_call(
        _inproj_kernel,
        grid=(b, t // tm),
        in_specs=[pl.BlockSpec((1, tm, d), lambda i, j: (i, j, 0)),
                  pl.BlockSpec(mod_blk, mod_map),
                  pl.BlockSpec(mod_blk, mod_map),
                  pl.BlockSpec(w_pad.shape, lambda i, j: (0, 0))],
        out_specs=[pl.BlockSpec((1, tm, w), lambda i, j: (i, j, 0)) for w in widths],
        out_shape=[jax.ShapeDtypeStruct((b, t, w), F32) for w in widths],
        compiler_params=_cparams(("parallel", "parallel")),
        name="inproj",
    )(x, sc, sh, w_pad)


def _rwkv_scan_kernel(r_ref, w_ref, k_ref, v_ref, kk_ref, kka_ref, s0_ref, y_ref, sfin_ref,
                      s_scr, y_scr, *, bg, tc, unroll):
    c = pl.program_id(1)
    rows = bg * HEAD_DIM

    @pl.when(c == 0)
    def _():
        s_scr[...] = s0_ref[...].reshape(rows, GROUP_WIDTH)

    li = lax.broadcasted_iota(jnp.int32, (2 * GROUP_WIDTH, GROUP_WIDTH), 0)
    lj = lax.broadcasted_iota(jnp.int32, (2 * GROUP_WIDTH, GROUP_WIDTH), 1)
    seg_ones2 = jnp.where((li % GROUP_WIDTH) // HEAD_DIM == lj // HEAD_DIM, 1.0, 0.0).astype(BF16)
    seg_ones1 = seg_ones2[:GROUP_WIDTH]
    lane = lax.broadcasted_iota(jnp.int32, (rows, GROUP_WIDTH), 1)
    row = lax.broadcasted_iota(jnp.int32, (rows, GROUP_WIDTH), 0)
    diag = jnp.where(row % HEAD_DIM == lane % HEAD_DIM, 1.0, 0.0)

    def seg_sum(x):
        hi = x.astype(BF16)
        lo = (x - hi.astype(F32)).astype(BF16)
        return jnp.dot(jnp.concatenate([hi, lo], axis=1), seg_ones2, preferred_element_type=F32)

    def bcast_rows(ref, t):
        return jnp.concatenate(
            [jnp.broadcast_to(ref[b, pl.ds(t, 1), :], (HEAD_DIM, GROUP_WIDTH)) for b in range(bg)], axis=0)

    def step(t):
        s = s_scr[...]
        kk = bcast_rows(kk_ref, t)
        sa = seg_sum(s * kk)
        vcol = seg_sum(bcast_rows(v_ref, t) * diag)
        s = s * bcast_rows(w_ref, t) - sa * bcast_rows(kka_ref, t) + vcol * bcast_rows(k_ref, t)
        s_scr[...] = s
        y = jnp.dot((s * bcast_rows(r_ref, t)).astype(BF16), seg_ones1, preferred_element_type=F32)
        y_scr[...] = jnp.where(lane % HEAD_DIM == t, y, y_scr[...])

    def outer(i, carry):
        for u in range(unroll):
            step(i * unroll + u)
        return carry

    y_scr[...] = jnp.zeros(y_scr.shape, F32)
    lax.fori_loop(0, tc // unroll, outer, 0)
    y_ref[...] = y_scr[...].reshape(bg, 1, HEAD_DIM, GROUP_WIDTH)

    @pl.when(c == pl.num_programs(1) - 1)
    def _():
        sfin_ref[...] = s_scr[...].reshape(bg, HEAD_DIM, GROUP_WIDTH)


def _rwkv_scan(r, w, k, v, kk, kka, s0):
    b, t, gw = r.shape
    bg = 8 if b % 8 == 0 else b
    tc = min(t, HEAD_DIM)
    unroll = 4 if tc % 4 == 0 else 1
    nc = t // tc
    s0l = s0.transpose(0, 2, 1, 3).reshape(b, HEAD_DIM, gw)
    seq = pl.BlockSpec((bg, tc, gw), lambda i, c: (i, c, 0))
    st = pl.BlockSpec((bg, HEAD_DIM, gw), lambda i, c: (i, 0, 0))
    y_raw, s_fin = pl.pallas_call(
        functools.partial(_rwkv_scan_kernel, bg=bg, tc=tc, unroll=unroll),
        grid=(b // bg, nc),
        in_specs=[seq] * 6 + [st],
        out_specs=[pl.BlockSpec((bg, 1, HEAD_DIM, gw), lambda i, c: (i, c, 0, 0)), st],
        out_shape=[jax.ShapeDtypeStruct((b, nc, HEAD_DIM, gw), F32),
                   jax.ShapeDtypeStruct((b, HEAD_DIM, gw), F32)],
        scratch_shapes=[pltpu.VMEM((bg * HEAD_DIM, gw), F32), pltpu.VMEM((bg * HEAD_DIM, gw), F32)],
        compiler_params=_cparams(("parallel", "arbitrary")),
        name="rwkv_scan",
    )(r, w, k, v, kk, kka, s0l)
    y = y_raw.reshape(b, nc, HEAD_DIM, N_HEADS, HEAD_DIM)[..., :tc]
    y = y.transpose(0, 1, 4, 3, 2).reshape(b, t, N_HEADS, HEAD_DIM)
    s_fin = s_fin.reshape(b, HEAD_DIM, N_HEADS, HEAD_DIM).transpose(0, 2, 1, 3)
    return y, s_fin


def _rwkv_mixer(za, shift_prev, s0, lp):
    b, t, _ = za.shape
    z_prev = jnp.concatenate([shift_prev[:, None], za[:, :-1]], axis=1)
    zs = za + lp['rwkv_mu'] * (z_prev - za)
    gw = GROUP_WIDTH
    r, k, v = zs[..., :gw], zs[..., gw:2 * gw], zs[..., 2 * gw:3 * gw]
    o = 3 * gw
    w_lo, a_lo, g_lo = zs[..., o:o + 32], zs[..., o + 32:o + 64], zs[..., o + 64:o + 128]
    decay = jnp.exp(-RWKV_DECAY_SCALE * jax.nn.sigmoid(lp['rwkv_w0'] + jnp.tanh(w_lo) @ lp['rwkv_w2']))
    a = jax.nn.sigmoid(lp['rwkv_a0'] + a_lo @ lp['rwkv_a2'])
    g = jax.nn.sigmoid(g_lo) @ lp['rwkv_g2']
    heads = lambda u: u.reshape(b, t, N_HEADS, HEAD_DIM)
    kk = heads(k) * lp['rwkv_kk']
    kk = kk / jnp.maximum(jnp.sqrt(jnp.sum(kk * kk, axis=-1, keepdims=True)), 1e-12)
    kk = kk.reshape(b, t, gw)
    k = k * (1.0 + (a - 1.0) * lp['rwkv_ka'].reshape(gw))
    y, s_fin = _rwkv_scan(r, decay, k, v, kk, kk * a, s0)
    mu = jnp.mean(y, axis=-1, keepdims=True)
    var = jnp.mean(jnp.square(y - mu), axis=-1, keepdims=True)
    yn = ((y - mu) * lax.rsqrt(var + RWKV_GN_EPS) * lp['rwkv_gn_g'].reshape(N_HEADS, HEAD_DIM)
          + lp['rwkv_gn_b'].reshape(N_HEADS, HEAD_DIM))
    bonus = jnp.sum(heads(r) * heads(k) * lp['rwkv_rk'], axis=-1, keepdims=True) * heads(v)
    out = (yn + bonus).reshape(b, t, gw) * g
    return out, s_fin, za[:, -1, :A_COLS]


KEY_TILE = 512
POS_SPLIT = 128
M_FLOOR = -1e29


def _aug_keys(kv, pos):
    lane = lax.broadcasted_iota(jnp.int32, kv.shape, 1)
    hi = (pos // POS_SPLIT).astype(F32)
    lo = (pos % POS_SPLIT).astype(F32)
    extra = jnp.where(lane == 0, hi, jnp.where(lane == 1, lo, 0.0))
    return jnp.concatenate([kv.astype(BF16), extra.astype(BF16)], axis=1)


def _aug_queries(q2, slopes, lane):
    rows = []
    for h in range(N_HEADS):
        pair = q2[:, (h // 2) * LANES:(h // 2 + 1) * LANES]
        if h % 2:
            pair = pltpu.roll(pair, HEAD_DIM, 1)
        extra = jnp.where(lane == 0, POS_SPLIT * slopes[h], jnp.where(lane == 1, slopes[h], 0.0))
        rows.append(jnp.concatenate([jnp.where(lane < HEAD_DIM, pair, 0.0), extra], axis=1))
    return jnp.concatenate(rows, axis=0).astype(BF16)


def _tile_heads(x):
    return jnp.concatenate([x] * N_HEADS, axis=0)


def _softmax_block(qa, ka, madd):
    s = lax.dot_general(qa, ka, _NT, preferred_element_type=F32) + _tile_heads(madd)
    m = jnp.maximum(jnp.max(s, axis=1, keepdims=True), M_FLOOR)
    e = jnp.exp(s - m)
    inv = 1.0 / jnp.maximum(jnp.sum(e, axis=1, keepdims=True), 1e-30)
    o = jnp.dot(e.astype(BF16), ka, preferred_element_type=F32) * inv
    return e, inv, o


def _flash_step(qa, ka, madd, m, l, acc):
    s = lax.dot_general(qa, ka, _NT, preferred_element_type=F32) + _tile_heads(madd)
    m_new = jnp.maximum(m, jnp.max(s, axis=1, keepdims=True))
    alpha = jnp.exp(m - m_new)
    p = jnp.exp(s - m_new)
    l = alpha * l + jnp.sum(p, axis=1, keepdims=True)
    acc = alpha * acc + jnp.dot(p.astype(BF16), ka, preferred_element_type=F32)
    return m_new, l, acc


def _flash_init(qb=Q_BLOCK):
    rows = N_HEADS * qb
    return (jnp.full((rows, 1), M_FLOOR, F32), jnp.zeros((rows, 1), F32), jnp.zeros((rows, 2 * LANES), F32))


def _heads_out(per_head, lane):
    pair = lambda a, b: jnp.where(lane < HEAD_DIM, pltpu.roll(a, HEAD_DIM, 1), b)
    return jnp.concatenate([pair(per_head[0], per_head[1]), pair(per_head[2], per_head[3])], axis=1)


def _head_rows(o, h, qb=Q_BLOCK):
    return o[h * qb:(h + 1) * qb, :LANES]


def _nsa_kernel(q_ref, ksv_ref, kwv_ref, gl_ref, kvc_ref, exp_ref, o_ref, ks_scr, kw_scr,
                *, t, n_win, n_top, slopes):
    qb = Q_BLOCK
    j = pl.program_id(1)
    q0 = j * qb

    @pl.when(j == 0)
    def _():
        for c in range(t // KEY_TILE):
            rows = slice(c * KEY_TILE, (c + 1) * KEY_TILE)
            pos = c * KEY_TILE + lax.broadcasted_iota(jnp.int32, (KEY_TILE, 1), 0)
            ks_scr[rows, :] = _aug_keys(ksv_ref[0, rows, :], pos)
            kw_scr[rows, :] = _aug_keys(kwv_ref[0, rows, :], pos)

    lane = lax.broadcasted_iota(jnp.int32, (qb, LANES), 1)
    qpos = q0 + lax.broadcasted_iota(jnp.int32, (qb, 1), 0)
    qa = _aug_queries(q_ref[0] * (HEAD_DIM ** -0.5), slopes, lane)
    gates = jax.nn.sigmoid(gl_ref[0])

    perm_end = lambda i: (2 * (i % HEAD_DIM) + i // HEAD_DIM) * NSA_CMP_BLOCK + (NSA_CMP_BLOCK - 1)
    kca = _aug_keys(kvc_ref[0], perm_end(lax.broadcasted_iota(jnp.int32, (LANES, 1), 0)))
    cend = perm_end(lax.broadcasted_iota(jnp.int32, (1, LANES), 1))
    e, inv, o_c = _softmax_block(qa, kca, jnp.where(cend <= qpos, 0.0, NEG_INF))
    p_c = e * inv
    imp = _head_rows(p_c, 0)
    for h in range(1, N_HEADS):
        imp = imp + _head_rows(p_c, h)
    imp = imp + pltpu.roll(imp, HEAD_DIM, 1)

    cur = qpos // NSA_SEL_BLOCK
    forced = (lane == 0) | (lane > cur - NSA_LOCAL_BLOCKS)
    score = jnp.where(lane > cur, NEG_INF, jnp.where(forced, FORCE_SCORE, imp))
    score = jnp.where(lane < HEAD_DIM, score, -3e38)
    rank = jnp.zeros((qb, LANES), F32)
    for i in range(NSA_SEL_BLOCK):
        col = score[:, i:i + 1]
        beats = (col > score) | ((col == score) & (lane > i))
        rank = rank + jnp.where(beats, 1.0, 0.0)
    sel = jnp.where((rank < n_top) & (lane < HEAD_DIM), 1.0, 0.0).astype(BF16)

    def sel_tile(kt, carry):
        k0 = pl.multiple_of(kt * KEY_TILE, KEY_TILE)
        kpos = k0 + lax.broadcasted_iota(jnp.int32, (1, KEY_TILE), 1)
        member = jnp.dot(sel, exp_ref[kt], preferred_element_type=F32) > 0.5
        madd = jnp.where(member & (kpos <= qpos), 0.0, NEG_INF)
        return _flash_step(qa, ks_scr[pl.ds(k0, KEY_TILE), :], madd, *carry)

    n_tiles = (q0 + qb - 1) // KEY_TILE + 1
    _, l_s, acc_s = lax.fori_loop(0, n_tiles, sel_tile, _flash_init())
    o_s = acc_s * (1.0 / jnp.maximum(l_s, 1e-30))

    kst = pl.multiple_of(jnp.maximum(q0 + qb - n_win, 0), qb)
    dw = qpos - (kst + lax.broadcasted_iota(jnp.int32, (1, n_win), 1))
    wadd = jnp.where((dw >= 0) & (dw <= NSA_WINDOW), 0.0, NEG_INF)
    _, _, o_w = _softmax_block(qa, kw_scr[pl.ds(kst, n_win), :], wadd)

    outs = []
    for h in range(N_HEADS):
        g = [gates[:, 3 * h + i:3 * h + i + 1] for i in range(3)]
        outs.append(g[0] * _head_rows(o_c, h) + g[1] * _head_rows(o_s, h) + g[2] * _head_rows(o_w, h))
    o_ref[0] = _heads_out(outs, lane)


def _nsa_prompt(zb, kvc_perm, slopes):
    b, t, _ = zb.shape
    assert t % KEY_TILE == 0 and t // NSA_CMP_BLOCK <= LANES and t <= 256 * POS_SPLIT
    n_win = min(NSA_WINDOW + Q_BLOCK, t)
    n_sel = -(-t // NSA_SEL_BLOCK)
    n_kt = t // KEY_TILE
    key_blk = (np.arange(t) // NSA_SEL_BLOCK).reshape(n_kt, 1, KEY_TILE)
    expand = jnp.asarray((np.arange(LANES)[None, :, None] == key_blk).astype(np.float32), BF16)
    full = lambda c: pl.BlockSpec((1, t, LANES), lambda i, j, c=c: (i, 0, c))
    return pl.pallas_call(
        functools.partial(_nsa_kernel, t=t, n_win=n_win, n_top=min(NSA_TOPN, n_sel), slopes=slopes),
        grid=(b, t // Q_BLOCK),
        in_specs=[pl.BlockSpec((1, Q_BLOCK, 2 * LANES), lambda i, j: (i, j, 0)),
                  full(3), full(4),
                  pl.BlockSpec((1, Q_BLOCK, LANES), lambda i, j: (i, j, 5)),
                  pl.BlockSpec((1, LANES, LANES), lambda i, j: (i, 0, 0)),
                  pl.BlockSpec((n_kt, LANES, KEY_TILE), lambda i, j: (0, 0, 0))],
        out_specs=pl.BlockSpec((1, Q_BLOCK, GROUP_WIDTH), lambda i, j: (i, j, 0)),
        out_shape=jax.ShapeDtypeStruct((b, t, GROUP_WIDTH), F32),
        scratch_shapes=[pltpu.VMEM((t, 2 * LANES), BF16), pltpu.VMEM((t, 2 * LANES), BF16)],
        compiler_params=_cparams(("parallel", "arbitrary")),
        name="nsa_prompt",
    )(zb, zb, zb, zb, kvc_perm, expand)


def _nsa_compress(rows, lp):
    b, l = rows.shape[:2]
    nc = l // NSA_CMP_BLOCK
    blk = rows[:, :nc * NSA_CMP_BLOCK].reshape(b, nc, NSA_CMP_BLOCK, 2, HEAD_DIM) + lp['nsa_pe']
    flat = jnp.swapaxes(blk, 2, 3).reshape(b, nc, 2, NSA_CMP_BLOCK * HEAD_DIM)
    hid = jax.nn.gelu(jnp.einsum('bnci,cih->bnch', flat, lp['nsa_w1']))
    return jnp.einsum('bnch,chd->bncd', hid, lp['nsa_w2'])


def _dsa_keyprep_kernel(kv_ref, ki_ref, ka_ref, kip_ref):
    pos = pl.program_id(1) * KEY_TILE + lax.broadcasted_iota(jnp.int32, (KEY_TILE, 1), 0)
    ka_ref[0] = _aug_keys(kv_ref[0], pos)
    ki = ki_ref[0]
    if ki.shape[1] == LANES:
        ki = jnp.where(lax.broadcasted_iota(jnp.int32, ki.shape, 1) < IDX_DIM, ki, 0.0)
    else:
        ki = jnp.concatenate([ki, jnp.zeros((KEY_TILE, LANES - IDX_DIM), F32)], axis=1)
    hi = ki.astype(BF16).astype(F32)
    kip_ref[0] = (hi + pltpu.roll(ki - hi, IDX_DIM, 1) + pltpu.roll(hi, 2 * IDX_DIM, 1)).astype(BF16)


def _dsa_keyprep(kv_src, kv_col, ki_src, ki_col):
    b, t, _ = kv_src.shape
    ki_w = min(ki_src.shape[2], LANES)
    return pl.pallas_call(
        _dsa_keyprep_kernel,
        grid=(b, t // KEY_TILE),
        in_specs=[pl.BlockSpec((1, KEY_TILE, LANES), lambda i, c: (i, c, kv_col)),
                  pl.BlockSpec((1, KEY_TILE, ki_w), lambda i, c: (i, c, ki_col))],
        out_specs=[pl.BlockSpec((1, KEY_TILE, 2 * LANES), lambda i, c: (i, c, 0)),
                   pl.BlockSpec((1, KEY_TILE, LANES), lambda i, c: (i, c, 0))],
        out_shape=[jax.ShapeDtypeStruct((b, t, 2 * LANES), BF16), jax.ShapeDtypeStruct((b, t, LANES), BF16)],
        compiler_params=_cparams(("parallel", "parallel")),
        name="dsa_keyprep",
    )(kv_src, ki_src)


def _dsa_kernel(q_ref, ka_ref, qi_ref, kip_ref, wiq_ref, o_ref, skey_scr, *, t, qb, q_base, k_sel, slopes):
    n_sub = KEY_TILE // LANES
    ka_scr, kip_scr = ka_ref.at[0], kip_ref.at[0]
    lane = lax.broadcasted_iota(jnp.int32, (qb, LANES), 1)
    if q_base is None:
        q0 = pl.program_id(1) * qb
        qpos = q0 + lax.broadcasted_iota(jnp.int32, (qb, 1), 0)
        n_tiles = (q0 + qb - 1) // KEY_TILE + 1
    else:
        qpos = jnp.full((qb, 1), q_base, jnp.int32)
        n_tiles = t // KEY_TILE
    tile_pos = lambda kt: kt * KEY_TILE + lax.broadcasted_iota(jnp.int32, (1, KEY_TILE), 1)

    qi = qi_ref[0]
    wq = wiq_ref[0]
    packed_q = []
    for h in range(IDX_HEADS):
        qh = qi if h == 0 else pltpu.roll(qi, LANES - IDX_DIM * h, 1)
        qh = jnp.where(lane < IDX_DIM, qh, 0.0)
        hi = qh.astype(BF16).astype(F32)
        packed_q.append(hi + pltpu.roll(hi, IDX_DIM, 1) + pltpu.roll(qh - hi, 2 * IDX_DIM, 1))
    packed_q = jnp.concatenate(packed_q, axis=0).astype(BF16)
    w_cols = [wq[:, IDX_DIM + h:IDX_DIM + h + 1] for h in range(IDX_HEADS)]

    def idx_tile(kt, carry):
        k0 = pl.multiple_of(kt * KEY_TILE, KEY_TILE)
        rel = jnp.maximum(lax.dot_general(packed_q, kip_scr[pl.ds(k0, KEY_TILE), :], _NT,
                                          preferred_element_type=F32), 0.0)
        score = w_cols[0] * rel[:qb]
        for h in range(1, IDX_HEADS):
            score = score + w_cols[h] * rel[h * qb:(h + 1) * qb]
        score = jnp.where(score == 0.0, 0.0, score)
        score = jnp.where(tile_pos(kt) <= qpos, score, NEG_INF)
        bits = pltpu.bitcast(score, jnp.int32)
        skey_scr[kt] = bits ^ ((bits >> 31) & 0x7FFFFFFF)
        return carry

    lax.fori_loop(0, n_tiles, idx_tile, 0)

    kf = float(k_sel)

    def count(c, strict):
        cb = jnp.broadcast_to(c, (qb, LANES))

        def body(kt, acc):
            x = skey_scr[kt]
            for u in range(n_sub):
                xu = x[:, u * LANES:(u + 1) * LANES]
                acc = acc + jnp.where((xu > cb) if strict else (xu >= cb), 1.0, 0.0)
            return acc

        acc = lax.fori_loop(0, n_tiles, body, jnp.zeros((qb, LANES), F32))
        return jnp.sum(acc, axis=1, keepdims=True)

    int_min = jnp.int32(-2 ** 31)
    tau = jnp.where(count(jnp.zeros((qb, 1), jnp.int32), False) >= kf, jnp.int32(0), int_min)

    def bit_step(i, tau):
        cand = tau + lax.shift_left(jnp.int32(1), 30 - i)
        return jnp.where(count(cand, False) >= kf, cand, tau)

    tau = lax.fori_loop(0, 31, bit_step, tau)
    need = kf - count(tau, True)
    taub = jnp.broadcast_to(tau, (qb, LANES))
    ui = lax.broadcasted_iota(jnp.int32, (LANES, LANES), 0)
    uj = lax.broadcasted_iota(jnp.int32, (LANES, LANES), 1)
    upper = jnp.where(ui <= uj, 1.0, 0.0).astype(BF16)
    qa = _aug_queries(q_ref[0] * (HEAD_DIM ** -0.5), slopes, lane)

    def att_tile(kt, carry):
        ties, m, l, acc = carry
        k0 = pl.multiple_of(kt * KEY_TILE, KEY_TILE)
        x = skey_scr[kt]
        parts = []
        for u in range(n_sub):
            xu = x[:, u * LANES:(u + 1) * LANES]
            eq = xu == taub
            eqf = jnp.where(eq, 1.0, 0.0)
            before = jnp.dot(eqf.astype(BF16), upper, preferred_element_type=F32) + ties
            parts.append(jnp.where((xu > taub) | (eq & (before <= need)), 0.0, NEG_INF))
            ties = ties + jnp.sum(eqf, axis=1, keepdims=True)
        madd = jnp.where(tile_pos(kt) <= qpos, jnp.concatenate(parts, axis=1), NEG_INF)
        return (ties,) + _flash_step(qa, ka_scr[pl.ds(k0, KEY_TILE), :], madd, m, l, acc)

    _, _, l, acc = lax.fori_loop(0, n_tiles, att_tile, (jnp.zeros((qb, 1), F32),) + _flash_init(qb))
    o = acc * (1.0 / jnp.maximum(l, 1e-30))
    o_ref[0] = _heads_out([_head_rows(o, h, qb) for h in range(N_HEADS)], lane)


def _dsa_attend(zq, ka, kip, slopes, *, qb, q_base, n_keys):
    b, tq, _ = zq.shape
    t = ka.shape[1]
    assert t % KEY_TILE == 0 and t <= 256 * POS_SPLIT and tq % qb == 0
    k_sel = min(DSA_TOPK, n_keys // 4)
    assert k_sel <= KEY_TILE
    full = lambda w: pl.BlockSpec((1, t, w), lambda i, j: (i, 0, 0))
    qblk = lambda c: pl.BlockSpec((1, qb, LANES), lambda i, j, c=c: (i, j, c))
    return pl.pallas_call(
        functools.partial(_dsa_kernel, t=t, qb=qb, q_base=q_base, k_sel=k_sel, slopes=slopes),
        grid=(b, tq // qb),
        in_specs=[pl.BlockSpec((1, qb, 2 * LANES), lambda i, j: (i, j, 0)),
                  full(2 * LANES), qblk(3), full(LANES), qblk(4)],
        out_specs=pl.BlockSpec((1, qb, GROUP_WIDTH), lambda i, j: (i, j, 0)),
        out_shape=jax.ShapeDtypeStruct((b, tq, GROUP_WIDTH), F32),
        scratch_shapes=[pltpu.VMEM((t // KEY_TILE, qb, KEY_TILE), jnp.int32)],
        compiler_params=_cparams(("parallel", "parallel")),
        name="dsa_attend",
    )(zq, ka, zq, kip, zq)


def _dsa_prompt(zc, slopes):
    ka, kip = _dsa_keyprep(zc, 2, zc, 4)
    return _dsa_attend(zc, ka, kip, slopes, qb=Q_BLOCK, q_base=None, n_keys=zc.shape[1])


def _outproj_kernel(ya_ref, yb_ref, yc_ref, yd_ref, x_ref, g1_ref, sc2_ref, sh2_ref, wo_ref, lg_ref, lb_ref,
                    wr_ref, br_ref, x1_ref, h2_ref, logit_ref, *, alpha):
    mix = jnp.zeros(x_ref.shape[1:], F32)
    for i, ref in enumerate((ya_ref, yb_ref, yc_ref, yd_ref)):
        mix = mix + jnp.dot(ref[0].astype(BF16), wo_ref[i], preferred_element_type=F32)
    x1 = _ln_rows(alpha * x_ref[0] + g1_ref[0] * mix) * lg_ref[...] + lb_ref[...]
    x1_ref[0] = x1
    h2 = _ln_rows(x1) * (1.0 + sc2_ref[0]) + sh2_ref[0]
    h2_ref[0] = h2.astype(BF16)
    logit_ref[0] = jnp.dot(h2, wr_ref[...], preferred_element_type=F32,
                           precision=lax.Precision.HIGHEST) + br_ref[...]


def _outproj(ys, x, g1, sc2, sh2, wo, ln_g, ln_b, wr_pad, br_pad, alpha):
    b, t, d = x.shape
    tm = min(t, 512)
    per_row = g1.shape[1] == t and t > 1
    mod_blk = (1, tm, d) if per_row else (1, 1, d)
    mod_map = (lambda i, j: (i, j, 0)) if per_row else (lambda i, j: (i, 0, 0))
    mod = pl.BlockSpec(mod_blk, mod_map)
    row = lambda w: pl.BlockSpec((1, tm, w), lambda i, j: (i, j, 0))
    const = lambda a: pl.BlockSpec(a.shape, lambda i, j: (0,) * a.ndim)
    return pl.pallas_call(
        functools.partial(_outproj_kernel, alpha=alpha),
        grid=(b, t // tm),
        in_specs=[row(GROUP_WIDTH)] * 4 + [row(d), mod, mod, mod, const(wo), const(ln_g), const(ln_b),
                                           const(wr_pad), const(br_pad)],
        out_specs=[row(d), row(d), row(LANES)],
        out_shape=[jax.ShapeDtypeStruct((b, t, d), F32), jax.ShapeDtypeStruct((b, t, d), BF16),
                   jax.ShapeDtypeStruct((b, t, LANES), F32)],
        compiler_params=_cparams(("parallel", "parallel")),
        name="outproj",
    )(*ys, x, g1, sc2, sh2, wo, ln_g, ln_b, wr_pad, br_pad)


def _moe_kernel(be_ref, nu_ref, x_ref, wgu_ref, bgu_ref, wd_ref, bd_ref, o_ref, wgu_bf, wd_bf):
    i = pl.program_id(0)
    ff = wd_ref.shape[1]

    @pl.when((i == 0) | (be_ref[i] != be_ref[jnp.maximum(i - 1, 0)]))
    def _():
        wgu_bf[...] = wgu_ref[0].astype(BF16)
        wd_bf[...] = wd_ref[0].astype(BF16)

    @pl.when(i < nu_ref[0])
    def _():
        gu = jnp.dot(x_ref[...], wgu_bf[...], preferred_element_type=F32) + bgu_ref[0]
        gt = jnp.minimum(gu[:, :ff], SWIGLU_LIMIT)
        up = jnp.clip(gu[:, ff:], -SWIGLU_LIMIT, SWIGLU_LIMIT)
        act = (up + 1.0) * gt * jax.nn.sigmoid(SWIGLU_ALPHA * gt)
        o_ref[...] = jnp.dot(act.astype(BF16), wd_bf[...], preferred_element_type=F32) + bd_ref[0]

    @pl.when(i >= nu_ref[0])
    def _():
        o_ref[...] = jnp.zeros(o_ref.shape, F32)


def _moe_experts(xg, blk_exp, n_used, wgu, bgu, wd, bd, blk):
    cap, d = xg.shape
    e, _, ff2 = wgu.shape
    ff = ff2 // 2
    grid_spec = pltpu.PrefetchScalarGridSpec(
        num_scalar_prefetch=2,
        grid=(cap // blk,),
        in_specs=[pl.BlockSpec((blk, d), lambda i, be, nu: (i, 0)),
                  pl.BlockSpec((1, d, ff2), lambda i, be, nu: (be[i], 0, 0)),
                  pl.BlockSpec((1, 1, ff2), lambda i, be, nu: (be[i], 0, 0)),
                  pl.BlockSpec((1, ff, d), lambda i, be, nu: (be[i], 0, 0)),
                  pl.BlockSpec((1, 1, d), lambda i, be, nu: (be[i], 0, 0))],
        out_specs=pl.BlockSpec((blk, d), lambda i, be, nu: (i, 0)),
        scratch_shapes=[pltpu.VMEM((d, ff2), BF16), pltpu.VMEM((ff, d), BF16)],
    )
    return pl.pallas_call(
        _moe_kernel,
        grid_spec=grid_spec,
        out_shape=jax.ShapeDtypeStruct((cap, d), F32),
        compiler_params=_cparams(("arbitrary",)),
        name="moe_experts",
    )(blk_exp, n_used, xg, wgu, bgu.reshape(e, 1, ff2), wd, bd.reshape(e, 1, d))


def _slot_tokens_kernel(pos_ref, out_ref, *, chunk, shift):
    c = pl.program_id(0)

    @pl.when(c == 0)
    def _():
        def zero(i, carry):
            out_ref[i] = jnp.int32(0)
            return carry
        lax.fori_loop(0, out_ref.shape[0], zero, 0, unroll=8)

    base = c * chunk

    def body(i, carry):
        out_ref[pos_ref[i]] = lax.shift_right_logical(base + i, shift)
        return carry

    lax.fori_loop(0, chunk, body, 0, unroll=8)


def _slot_tokens(pos, cap):
    m = pos.shape[0]
    assert TOP_K & (TOP_K - 1) == 0
    chunk = min(m, 8192)
    assert m % chunk == 0
    return pl.pallas_call(
        functools.partial(_slot_tokens_kernel, chunk=chunk, shift=TOP_K.bit_length() - 1),
        grid=(m // chunk,),
        in_specs=[pl.BlockSpec((chunk,), lambda c: (c,), memory_space=pltpu.SMEM)],
        out_specs=pl.BlockSpec((cap,), lambda c: (0,), memory_space=pltpu.SMEM),
        out_shape=jax.ShapeDtypeStruct((cap,), jnp.int32),
        compiler_params=pltpu.CompilerParams(dimension_semantics=("arbitrary",)),
        name="slot_tokens",
    )(pos)


def _moe(h2, logits, mw, layer, blk):
    n, d = h2.shape
    n_exp = logits.shape[1]
    top_v, top_e = lax.top_k(logits, TOP_K)
    gate = jax.nn.softmax(top_v, axis=-1)
    m = n * TOP_K
    cap = -(-(m + n_exp * (blk - 1)) // blk) * blk
    e_flat = top_e.reshape(m)
    onehot = (e_flat[:, None] == jnp.arange(n_exp)[None, :]).astype(jnp.int32)
    csum = jnp.cumsum(onehot, axis=0)
    counts = csum[-1]
    rank = jnp.take_along_axis(csum, e_flat[:, None], axis=1)[:, 0] - 1
    padded = (counts + blk - 1) // blk * blk
    pend = jnp.cumsum(padded)
    pstart = pend - padded
    pos = (pstart[e_flat] + rank).astype(jnp.int32)
    nblk = cap // blk
    blk_start = jnp.arange(nblk, dtype=jnp.int32) * blk
    blk_exp = jnp.minimum(jnp.sum((pend[None, :] <= blk_start[:, None]).astype(jnp.int32), axis=1), n_exp - 1)
    row_tok = _slot_tokens(pos, cap)
    n_used = (pend[-1] // blk).astype(jnp.int32).reshape(1)
    xg = h2.at[row_tok].get(mode='promise_in_bounds')
    out = _moe_experts(xg, (blk_exp + layer * n_exp).astype(jnp.int32), n_used, *mw, blk)
    pos = pos.reshape(n, TOP_K)
    picked = [out.at[pos[:, j]].get(mode='promise_in_bounds') for j in range(TOP_K)]
    return picked, jnp.pad(gate, ((0, 0), (0, LANES - TOP_K)))


def _moe_block_rows(n_assign, n_exp):
    per_expert = max(1, n_assign // n_exp)
    return min(512, max(16, 1 << (per_expert.bit_length() - 1)))


def _resln_kernel(x_ref, gate_ref, *refs, alpha):
    picked, (g_ref, lg_ref, lb_ref, o_ref) = refs[:TOP_K], refs[TOP_K:]
    gate = gate_ref[0]
    y = gate[:, 0:1] * picked[0][0]
    for j in range(1, TOP_K):
        y = y + gate[:, j:j + 1] * picked[j][0]
    o_ref[0] = _ln_rows(alpha * x_ref[0] + g_ref[0] * y) * lg_ref[...] + lb_ref[...]


def _resln(x, picked, gate, g, ln_g, ln_b, alpha):
    b, t, d = x.shape
    tm = min(t, 512)
    per_row = g.shape[1] == t and t > 1
    mod = pl.BlockSpec((1, tm, d) if per_row else (1, 1, d),
                       (lambda i, j: (i, j, 0)) if per_row else (lambda i, j: (i, 0, 0)))
    row = lambda w: pl.BlockSpec((1, tm, w), lambda i, j: (i, j, 0))
    const = pl.BlockSpec((1, d), lambda i, j: (0, 0))
    return pl.pallas_call(
        functools.partial(_resln_kernel, alpha=alpha),
        grid=(b, t // tm),
        in_specs=[row(d), row(LANES)] + [row(d)] * TOP_K + [mod, const, const],
        out_specs=row(d),
        out_shape=jax.ShapeDtypeStruct((b, t, d), F32),
        compiler_params=_cparams(("parallel", "parallel")),
        name="resln",
    )(x, gate.reshape(b, t, LANES), *[p.reshape(b, t, d) for p in picked], g, ln_g, ln_b)


def _conv_mixer(zd, buf, conv_w):
    t = zd.shape[1]
    ch = GROUP_WIDTH
    b_gate, c_gate, xin = zd[..., :ch], zd[..., ch:2 * ch], zd[..., 2 * ch:3 * ch]
    ue = jnp.concatenate([buf, c_gate * xin], axis=1)
    y = sum(ue[:, j:j + t] * conv_w[j] for j in range(CONV_K))
    return b_gate * y, ue[:, t:]


def _nsa_compress_pages(pages, lp):
    b, p, _, dh, rows = pages.shape
    nj = rows // NSA_CMP_BLOCK
    x = pages.reshape(b, p, 2, dh, nj, NSA_CMP_BLOCK) + lp['nsa_pe'].transpose(1, 2, 0)[:, :, None, :]
    w1 = lp['nsa_w1'].reshape(2, NSA_CMP_BLOCK, dh, -1)
    hid = jax.nn.gelu(jnp.einsum('bpcdjr,crdh->bpjch', x, w1))
    return jnp.einsum('bpjch,chd->bpjcd', hid, lp['nsa_w2']).reshape(b, p * nj, 2, dh)


def _masked_softmax(s, valid):
    p = jax.nn.softmax(jnp.where(valid, s, NEG_INF), axis=-1)
    return jnp.where(valid, p, 0.0)


def _nsa_attend_sample(q, gates, qpos, kvc, slc_pool, page_table, new_slc, kvw, wpos, slopes, n_keys):
    b, tq = q.shape[:2]
    q = q * HEAD_DIM ** -0.5
    sl = jnp.asarray(slopes, F32)[:, None, None]
    nc = kvc.shape[1]
    cend = jnp.arange(nc) * NSA_CMP_BLOCK + (NSA_CMP_BLOCK - 1)
    s_c = jnp.einsum('bqhd,bnd->bhqn', q, kvc[:, :, 0]) - sl * (qpos[:, None] - cend[None, :]).astype(F32)
    p_c = _masked_softmax(s_c, cend[None, :] <= qpos[:, None])
    o_c = jnp.einsum('bhqn,bnd->bqhd', p_c, kvc[:, :, 1])
    ratio = NSA_SEL_BLOCK // NSA_CMP_BLOCK
    n_sel = -(-n_keys // NSA_SEL_BLOCK)
    imp = jnp.pad(p_c.sum(1), ((0, 0), (0, 0), (0, n_sel * ratio - nc))).reshape(b, tq, n_sel, ratio).sum(-1)
    blk = jnp.arange(n_sel)[None, :]
    cur = (qpos // NSA_SEL_BLOCK)[:, None]
    forced = (blk == 0) | (blk > cur - NSA_LOCAL_BLOCKS)
    score = jnp.where(blk > cur, NEG_INF, jnp.where(forced, FORCE_SCORE, imp))
    n_top = min(NSA_TOPN, n_sel)
    _, sel = lax.top_k(score, n_top)
    assert tq == 1 and PAGE_SIZE == 2 * NSA_SEL_BLOCK
    sel = sel[:, 0]
    n_pages = page_table.shape[1]
    n_past = n_pages * PAGE_SIZE
    pages = jnp.take_along_axis(page_table, jnp.minimum(sel // 2, n_pages - 1), axis=1)
    slabs = slc_pool[pages]
    odd = (sel % 2 == 1)[:, :, None, None, None]
    half = jnp.where(odd, slabs[..., NSA_SEL_BLOCK:], slabs[..., :NSA_SEL_BLOCK])
    kpos = sel[:, :, None] * NSA_SEL_BLOCK + jnp.arange(NSA_SEL_BLOCK)
    q1 = q[:, 0]
    s_past = jnp.einsum('bhd,bndr->bhnr', q1, half[:, :, 0]) + sl * kpos[:, None].astype(F32)
    s_past = jnp.where((kpos < n_past)[:, None], s_past, NEG_INF).reshape(b, N_HEADS, -1)
    new_sel = jnp.any(sel == n_past // NSA_SEL_BLOCK, axis=1)[:, None, None]
    s_new = jnp.einsum('bhd,bd->bh', q1, new_slc[:, 0, 0])[..., None] + sl[None, :, 0] * float(n_past)
    s_all = jnp.concatenate([s_past, jnp.where(new_sel, s_new, NEG_INF)], axis=-1)
    p_s = _masked_softmax(s_all, s_all > 0.5 * NEG_INF)
    o_s = (jnp.einsum('bhnr,bndr->bhd', p_s[..., :-1].reshape(b, N_HEADS, n_top, NSA_SEL_BLOCK), half[:, :, 1])
           + p_s[..., -1:] * new_slc[:, 0, 1][:, None, :])[:, None]
    dw = qpos[:, None] - wpos[None, :]
    s_w = jnp.einsum('bqhd,bnd->bhqn', q, kvw[:, :, 0]) - sl * dw.astype(F32)
    p_w = _masked_softmax(s_w, (dw >= 0) & (dw <= NSA_WINDOW) & (wpos[None, :] >= 0))
    o_w = jnp.einsum('bhqn,bnd->bqhd', p_w, kvw[:, :, 1])
    return gates[..., 0:1] * o_c + gates[..., 1:2] * o_s + gates[..., 2:3] * o_w


def _dsa_sample_kernel(pt_ref, zt_ref, kv_hbm, idx_hbm, o_ref, kvbuf, idxbuf, sem, score_scr, s_scr, p_scr,
                       *, layer, n_pages, k_sel, slopes):
    b = pl.program_id(0)
    nb = pl.num_programs(0)
    slot = b % 2
    ps = PAGE_SIZE
    gw = GROUP_WIDTH

    def page_copies(seq, sl, p):
        pg = pt_ref[seq, p]
        return (pltpu.make_async_copy(kv_hbm.at[layer, pg], kvbuf.at[sl, p], sem.at[0, sl]),
                pltpu.make_async_copy(idx_hbm.at[layer, pg], idxbuf.at[sl, p], sem.at[1, sl]))

    def fetch(seq, sl):
        def body(p, carry):
            for cp in page_copies(seq, sl, p):
                cp.start()
            return carry
        lax.fori_loop(0, n_pages, body, 0)

    @pl.when(b == 0)
    def _():
        fetch(0, 0)

    @pl.when(b + 1 < nb)
    def _():
        fetch(b + 1, 1 - slot)

    def wait_body(p, carry):
        for cp in page_copies(b, slot, p):
            cp.wait()
        return carry

    lax.fori_loop(0, n_pages, wait_body, 0)

    zt = zt_ref[0]
    o_kv = gw
    o_qi = gw + 2 * HEAD_DIM
    o_ki = o_qi + IDX_HEADS * IDX_DIM
    o_wi = o_ki + IDX_DIM
    q_cols = [zt[h * HEAD_DIM:(h + 1) * HEAD_DIM] * (HEAD_DIM ** -0.5) for h in range(N_HEADS)]
    qi_cols = [zt[o_qi + h * IDX_DIM:o_qi + (h + 1) * IDX_DIM] for h in range(IDX_HEADS)]
    w_idx = [zt[o_wi + h:o_wi + h + 1] for h in range(IDX_HEADS)]
    k_new, v_new = zt[o_kv:o_kv + HEAD_DIM], zt[o_kv + HEAD_DIM:o_kv + 2 * HEAD_DIM]
    ki_new = zt[o_ki:o_ki + IDX_DIM]
    colsum = lambda x: jnp.sum(x, axis=0, keepdims=True)
    total = lambda x: jnp.sum(colsum(x), axis=1, keepdims=True)
    unzero = lambda x: jnp.where(x == 0.0, 0.0, x)

    def order_key(x):
        bits = pltpu.bitcast(x, jnp.int32)
        return bits ^ ((bits >> 31) & 0x7FFFFFFF)

    qi_wide = [jnp.broadcast_to(c, (IDX_DIM, ps)) for c in qi_cols]
    w_wide = [jnp.broadcast_to(w, (1, ps)) for w in w_idx]

    def score_page(p, carry):
        kit = idxbuf[slot, p]
        row = w_wide[0] * jnp.maximum(colsum(kit * qi_wide[0]), 0.0)
        for h in range(1, IDX_HEADS):
            row = row + w_wide[h] * jnp.maximum(colsum(kit * qi_wide[h]), 0.0)
        score_scr[pl.ds(p, 1), :] = row
        return carry

    lax.fori_loop(0, n_pages, score_page, 0)
    sc_new = w_idx[0] * jnp.maximum(colsum(ki_new * qi_cols[0]), 0.0)
    for h in range(1, IDX_HEADS):
        sc_new = sc_new + w_idx[h] * jnp.maximum(colsum(ki_new * qi_cols[h]), 0.0)
    key = order_key(unzero(score_scr[...]))
    key_new = order_key(unzero(sc_new))

    kf = float(k_sel)

    def count(c, strict):
        past = jnp.where((key > c) if strict else (key >= c), 1.0, 0.0)
        new = jnp.where((key_new > c) if strict else (key_new >= c), 1.0, 0.0)
        return total(past) + new

    int_min = jnp.int32(-2 ** 31)
    tau = jnp.where(count(jnp.zeros((1, 1), jnp.int32), False) >= kf, jnp.int32(0), int_min)

    def bit_step(i, tau):
        cand = tau + lax.shift_left(jnp.int32(1), 30 - i)
        return jnp.where(count(cand, False) >= kf, cand, tau)

    tau = lax.fori_loop(0, 31, bit_step, tau)
    need = kf - count(tau, True)
    eq = key == tau
    eqf = jnp.where(eq, 1.0, 0.0).astype(BF16)
    ri = lax.broadcasted_iota(jnp.int32, (ps, ps), 0)
    ci = lax.broadcasted_iota(jnp.int32, (ps, ps), 1)
    in_row = jnp.dot(eqf, jnp.where(ri <= ci, 1.0, 0.0).astype(BF16), preferred_element_type=F32)
    row_tot = jnp.broadcast_to(in_row[:, ps - 1:ps], (n_pages, ps)).astype(BF16)
    pi = lax.broadcasted_iota(jnp.int32, (n_pages, n_pages), 0)
    pj = lax.broadcasted_iota(jnp.int32, (n_pages, n_pages), 1)
    rows_before = jnp.dot(jnp.where(pj < pi, 1.0, 0.0).astype(BF16), row_tot, preferred_element_type=F32)
    sel = (key > tau) | (eq & (in_row + rows_before <= need))
    eq_new = key_new == tau
    sel_new = (key_new > tau) | (eq_new & (total(jnp.where(eq, 1.0, 0.0)) + 1.0 <= need))

    q_wide = [jnp.broadcast_to(c, (HEAD_DIM, ps)) for c in q_cols]

    def qk_page(p, carry):
        kt = kvbuf[slot, p, 0]
        for h in range(N_HEADS):
            s_scr[h, pl.ds(p, 1), :] = colsum(kt * q_wide[h])
        return carry

    lax.fori_loop(0, n_pages, qk_page, 0)
    kpos = (lax.broadcasted_iota(jnp.int32, (n_pages, ps), 0) * ps
            + lax.broadcasted_iota(jnp.int32, (n_pages, ps), 1)).astype(F32)
    e_new, inv = [], []
    for h in range(N_HEADS):
        s = jnp.where(sel, s_scr[h] + slopes[h] * kpos, NEG_INF)
        s_n = jnp.where(sel_new, colsum(k_new * q_cols[h]) + slopes[h] * float(n_pages * ps), NEG_INF)
        m = jnp.maximum(jnp.max(jnp.max(s, axis=0, keepdims=True), axis=1, keepdims=True), s_n)
        e = jnp.exp(s - m)
        e_n = jnp.exp(s_n - m)
        p_scr[h] = e
        e_new.append(e_n)
        inv.append(1.0 / (total(e) + e_n))

    def pv_page(p, accs):
        vt = kvbuf[slot, p, 1]
        return tuple(accs[h] + vt * p_scr[h, pl.ds(p, 1), :] for h in range(N_HEADS))

    accs = lax.fori_loop(0, n_pages, pv_page, tuple(jnp.zeros((HEAD_DIM, ps), F32) for _ in range(N_HEADS)))
    outs = [(jnp.sum(accs[h], axis=1, keepdims=True) + e_new[h] * v_new) * inv[h] for h in range(N_HEADS)]
    o_ref[0] = jnp.concatenate(outs, axis=0)


def _dsa_sample(zc, cache_dsa_kv, cache_dsa_idx, layer, page_table, slopes):
    b, t, c_pad = zc.shape
    assert t == 1
    n_pages = page_table.shape[1]
    n_keys = n_pages * PAGE_SIZE + t
    k_sel = min(DSA_TOPK, n_keys // 4)
    kv_t = cache_dsa_kv.transpose(0, 1, 3, 4, 2)
    idx_t = cache_dsa_idx.transpose(0, 1, 3, 2)
    grid_spec = pltpu.PrefetchScalarGridSpec(
        num_scalar_prefetch=1,
        grid=(b,),
        in_specs=[pl.BlockSpec((1, c_pad, 1), lambda i, pt: (i, 0, 0)),
                  pl.BlockSpec(memory_space=pl.ANY), pl.BlockSpec(memory_space=pl.ANY)],
        out_specs=pl.BlockSpec((1, GROUP_WIDTH, 1), lambda i, pt: (i, 0, 0)),
        scratch_shapes=[pltpu.VMEM((2, n_pages, 2, HEAD_DIM, PAGE_SIZE), F32),
                        pltpu.VMEM((2, n_pages, IDX_DIM, PAGE_SIZE), F32),
                        pltpu.SemaphoreType.DMA((2, 2)),
                        pltpu.VMEM((n_pages, PAGE_SIZE), F32),
                        pltpu.VMEM((N_HEADS, n_pages, PAGE_SIZE), F32),
                        pltpu.VMEM((N_HEADS, n_pages, PAGE_SIZE), F32)],
    )
    out = pl.pallas_call(
        functools.partial(_dsa_sample_kernel, layer=layer, n_pages=n_pages, k_sel=k_sel, slopes=slopes),
        grid_spec=grid_spec,
        out_shape=jax.ShapeDtypeStruct((b, GROUP_WIDTH, 1), F32),
        compiler_params=_cparams(("arbitrary",)),
        name="dsa_sample",
    )(page_table, zc.transpose(0, 2, 1), kv_t, idx_t)
    return out.transpose(0, 2, 1)


def _kv_rows(u):
    return u.reshape(u.shape[0], u.shape[1], 2, HEAD_DIM)


def _mixers_prompt(za, zb, zc, zd, lp, sl_nsa, sl_dsa):
    b, t, _ = za.shape
    gw = GROUP_WIDTH
    ya, s_rwkv, shift = _rwkv_mixer(za, jnp.zeros((b, A_PAD), F32),
                                    jnp.zeros((b, N_HEADS, HEAD_DIM, HEAD_DIM), F32), lp)
    kv_c = _kv_rows(zb[..., gw:gw + 2 * HEAD_DIM])
    kv_s = _kv_rows(zb[..., gw + 2 * HEAD_DIM:gw + 4 * HEAD_DIM])
    kv_w = _kv_rows(zb[..., gw + 4 * HEAD_DIM:gw + 6 * HEAD_DIM])
    kvc = _nsa_compress(kv_c, lp).reshape(b, -1, 2 * HEAD_DIM)
    nc = kvc.shape[1]
    kvc = jnp.pad(kvc, ((0, 0), (0, LANES - nc), (0, 0)))
    kvc = jnp.concatenate([kvc[:, 0::2], kvc[:, 1::2]], axis=1)
    yb = _nsa_prompt(zb, kvc, sl_nsa)
    yc = _dsa_prompt(zc, sl_dsa)
    kv_d = _kv_rows(zc[..., gw:gw + 2 * HEAD_DIM])
    o = gw + 2 * HEAD_DIM + IDX_HEADS * IDX_DIM
    ki = zc[..., o:o + IDX_DIM]
    yd, conv_buf = _conv_mixer(zd, jnp.zeros((b, CONV_K - 1, gw), F32), lp['conv_w'])
    return (ya, yb, yc, yd), (s_rwkv, shift, kv_c, kv_s, kv_w[:, -min(NSA_WINDOW, t):], kv_d, ki, conv_buf)


def _mixers_sample(za, zb, zc, zd, lp, sl_nsa, sl_dsa, s_rwkv, s_shift, cache_nsa_cmp, cache_nsa_slc, s_win,
                   cache_dsa_kv, cache_dsa_idx, layer, s_conv, page_table):
    b, t, _ = za.shape
    gw = GROUP_WIDTH
    n_past = page_table.shape[1] * PAGE_SIZE
    n_buf = s_win.shape[1]
    qpos = n_past + jnp.arange(t)
    shift_prev = jnp.pad(s_shift, ((0, 0), (0, A_PAD - A_COLS)))
    ya, s_rwkv_new, shift = _rwkv_mixer(za, shift_prev, s_rwkv, lp)
    q = zb[..., :gw].reshape(b, t, N_HEADS, HEAD_DIM)
    kv_c = _kv_rows(zb[..., gw:gw + 2 * HEAD_DIM])
    kv_s = _kv_rows(zb[..., gw + 2 * HEAD_DIM:gw + 4 * HEAD_DIM])
    kv_w = _kv_rows(zb[..., gw + 4 * HEAD_DIM:gw + 6 * HEAD_DIM])
    o = gw + 6 * HEAD_DIM
    gates = jax.nn.sigmoid(zb[..., o:o + 3 * N_HEADS]).reshape(b, t, N_HEADS, 3)
    assert (n_past + t) // NSA_CMP_BLOCK * NSA_CMP_BLOCK == n_past
    slabs = lambda c: c.transpose(0, 1, 3, 4, 2).reshape(-1, 2, HEAD_DIM, PAGE_SIZE)
    layer_pages = page_table + layer * cache_nsa_cmp.shape[1]
    kvc = _nsa_compress_pages(slabs(cache_nsa_cmp)[layer_pages], lp)
    kvw = jnp.concatenate([s_win, kv_w], axis=1)
    wpos = n_past - n_buf + jnp.arange(n_buf + t)
    yb = _nsa_attend_sample(q, gates, qpos, kvc, slabs(cache_nsa_slc), layer_pages, kv_s,
                            kvw, wpos, sl_nsa, n_past + t)
    kv_d = _kv_rows(zc[..., gw:gw + 2 * HEAD_DIM])
    o = gw + 2 * HEAD_DIM + IDX_HEADS * IDX_DIM
    ki = zc[..., o:o + IDX_DIM]
    yc = _dsa_sample(zc, cache_dsa_kv, cache_dsa_idx, layer, page_table, sl_dsa)
    yd, conv_buf = _conv_mixer(zd, s_conv, lp['conv_w'])
    flat = lambda u: u.reshape(b, t, gw)
    return (ya, flat(yb), flat(yc), yd), (s_rwkv_new, shift, kv_c, kv_s, kvw[:, -n_buf:], kv_d, ki, conv_buf)


def _layer(x, mod, lp, pw, mixer_fn, alpha):
    b, t, d = x.shape
    sh1, sc1, g1, sh2, sc2, g2 = mod
    zs = _inproj(x, sc1, sh1, pw['w_in'])
    ys, states = mixer_fn(*zs)
    x1, h2, logits = _outproj(ys, x, g1, sc2, sh2, pw['w_out'], pw['ln1_g'], pw['ln1_b'],
                              pw['wr'], pw['br'], alpha)
    n_exp = lp['moe_wr'].shape[1]
    blk = _moe_block_rows(b * t * TOP_K, n_exp)
    picked, gate = _moe(h2.reshape(b * t, d), logits.reshape(b * t, LANES)[:, :n_exp], pw['moe'], pw['layer'], blk)
    x2 = _resln(x1, picked, gate, g2, pw['ln2_g'], pw['ln2_b'], alpha)
    return x2, states


def _prep_weights(lp):
    d = lp['w_in'].shape[0]
    cols, o = [], 0
    for c, p in ((A_COLS, A_PAD), (B_COLS, B_PAD), (C_COLS, C_PAD), (D_COLS, D_PAD)):
        cols.append(jnp.pad(lp['w_in'][:, o:o + c], ((0, 0), (0, p - c))))
        o += c
    n_exp = lp['moe_wr'].shape[1]
    row = lambda u: u.reshape(1, -1)
    return {
        'w_in': jnp.concatenate(cols, axis=1).astype(BF16),
        'w_out': lp['w_out'].reshape(4, GROUP_WIDTH, d).astype(BF16),
        'ln1_g': row(lp['ln1_g']), 'ln1_b': row(lp['ln1_b']),
        'ln2_g': row(lp['ln2_g']), 'ln2_b': row(lp['ln2_b']),
        'wr': jnp.pad(lp['moe_wr'], ((0, 0), (0, LANES - n_exp))),
        'br': jnp.pad(lp['moe_br'], (0, LANES - n_exp)).reshape(1, LANES),
    }


def kernel(x_prompt, x_sample, state_rwkv, state_rwkv_shift, cache_nsa_cmp, cache_nsa_slc, state_nsa_win, cache_dsa_kv, cache_dsa_idx, state_conv, page_table, c_prompt, c_sample, w_ada, b_ada, w_in, w_out, ln1_g, ln1_b, ln2_g, ln2_b, rwkv_mu, rwkv_w0, rwkv_w2, rwkv_a0, rwkv_a2, rwkv_g2, rwkv_kk, rwkv_ka, rwkv_rk, rwkv_gn_g, rwkv_gn_b, nsa_pe, nsa_w1, nsa_w2, conv_w, moe_wr, moe_br, moe_wgu, moe_bgu, moe_wd, moe_bd):
    depth = w_in.shape[0]
    alpha = (2 * depth) ** 0.25
    sl_nsa, sl_dsa = _alibi_slopes()
    bp = x_prompt.shape[0]
    hp, hs = x_prompt, x_sample
    c_all = jnp.concatenate([c_prompt, c_sample], axis=0)
    flat_le = lambda u: u.reshape((-1,) + u.shape[2:])
    moe_all = (flat_le(moe_wgu), flat_le(moe_bgu), flat_le(moe_wd), flat_le(moe_bd))
    states_p, states_s = [], []
    for l in range(depth):
        lp = {
            'w_in': w_in[l], 'w_out': w_out[l], 'ln1_g': ln1_g[l], 'ln1_b': ln1_b[l],
            'ln2_g': ln2_g[l], 'ln2_b': ln2_b[l],
            'rwkv_mu': jnp.pad(rwkv_mu[l], (0, A_PAD - A_COLS)), 'rwkv_w0': rwkv_w0[l], 'rwkv_w2': rwkv_w2[l],
            'rwkv_a0': rwkv_a0[l], 'rwkv_a2': rwkv_a2[l], 'rwkv_g2': rwkv_g2[l],
            'rwkv_kk': rwkv_kk[l], 'rwkv_ka': rwkv_ka[l], 'rwkv_rk': rwkv_rk[l],
            'rwkv_gn_g': rwkv_gn_g[l], 'rwkv_gn_b': rwkv_gn_b[l],
            'nsa_pe': nsa_pe[l], 'nsa_w1': nsa_w1[l], 'nsa_w2': nsa_w2[l], 'conv_w': conv_w[l],
            'moe_wr': moe_wr[l], 'moe_br': moe_br[l],
        }
        pw = _prep_weights(lp)
        pw['moe'], pw['layer'] = moe_all, l
        mod_all = jax.nn.silu(c_all) @ w_ada[l] + b_ada[l]
        mod_p = [u[:, None, :] for u in jnp.split(mod_all[:bp], 6, axis=-1)]
        mod_s = [u[:, None, :] for u in jnp.split(mod_all[bp:], 6, axis=-1)]
        mix_p = functools.partial(_mixers_prompt, lp=lp, sl_nsa=sl_nsa, sl_dsa=sl_dsa)
        mix_s = functools.partial(
            _mixers_sample, lp=lp, sl_nsa=sl_nsa, sl_dsa=sl_dsa, s_rwkv=state_rwkv[l],
            s_shift=state_rwkv_shift[l], cache_nsa_cmp=cache_nsa_cmp, cache_nsa_slc=cache_nsa_slc,
            s_win=state_nsa_win[l], cache_dsa_kv=cache_dsa_kv, cache_dsa_idx=cache_dsa_idx, layer=l,
            s_conv=state_conv[l], page_table=page_table)
        hs, st_s = _layer(hs, mod_s, lp, pw, mix_s, alpha)
        hp, st_p = _layer(hp, mod_p, lp, pw, mix_p, alpha)
        states_p.append(st_p)
        states_s.append(st_s)
    outs_p = [jnp.stack(v) for v in zip(*states_p)]
    outs_s = [jnp.stack(v) for v in zip(*states_s)]
    return (hp, hs, *outs_p, *outs_s)
```

```python
import functools

import jax
import jax.numpy as jnp
import numpy as np
from jax import lax
from jax.experimental import pallas as pl
from jax.experimental.pallas import tpu as pltpu

F32 = jnp.float32
BF16 = jnp.bfloat16

HEAD_DIM = 64
N_HEADS = 4
GROUP_WIDTH = N_HEADS * HEAD_DIM
RWKV_DECAY_SCALE = 0.606531
RWKV_GN_EPS = 64e-5
NSA_CMP_BLOCK = 32
NSA_SEL_BLOCK = 64
NSA_TOPN = 16
NSA_LOCAL_BLOCKS = 2
NSA_WINDOW = 512
DSA_TOPK = 256
IDX_HEADS = 4
IDX_DIM = 32
CONV_K = 3
TOP_K = 4
SWIGLU_LIMIT = 7.0
SWIGLU_ALPHA = 1.702
NEG_INF = -1e30
FORCE_SCORE = 1e4
Q_BLOCK = 128
PAGE_SIZE = 128

LANES = 128
A_COLS = 3 * GROUP_WIDTH + 32 + 32 + 64
B_COLS = GROUP_WIDTH + 6 * HEAD_DIM + 3 * N_HEADS
C_COLS = GROUP_WIDTH + 2 * HEAD_DIM + IDX_HEADS * IDX_DIM + IDX_DIM + IDX_HEADS
D_COLS = 3 * GROUP_WIDTH


def _pad_to(n, m):
    return -(-n // m) * m


A_PAD, B_PAD, C_PAD, D_PAD = (_pad_to(c, LANES) for c in (A_COLS, B_COLS, C_COLS, D_COLS))
VMEM_LIMIT = 56 * 1024 * 1024

_NT = (((1,), (1,)), ((), ()))


def _alibi_slopes():
    n = 2 * N_HEADS
    s = [2.0 ** (-8.0 * i / n) for i in range(1, n + 1)]
    return s[0::2], s[1::2]


def _ln_rows(x, eps=1e-5):
    mu = jnp.mean(x, axis=-1, keepdims=True)
    xc = x - mu
    var = jnp.mean(xc * xc, axis=-1, keepdims=True)
    return xc * lax.rsqrt(var + eps)


def _cparams(sem):
    return pltpu.CompilerParams(dimension_semantics=sem, vmem_limit_bytes=VMEM_LIMIT)


def _inproj_kernel(x_ref, sc_ref, sh_ref, w_ref, za_ref, zb_ref, zc_ref, zd_ref):
    h = _ln_rows(x_ref[0]) * (1.0 + sc_ref[0]) + sh_ref[0]
    z = jnp.dot(h.astype(BF16), w_ref[...], preferred_element_type=F32)
    o = 0
    for ref, w in ((za_ref, A_PAD), (zb_ref, B_PAD), (zc_ref, C_PAD), (zd_ref, D_PAD)):
        ref[0] = z[:, o:o + w]
        o += w


def _inproj(x, sc, sh, w_pad):
    b, t, d = x.shape
    tm = min(t, 512)
    mod_rows = sc.shape[1]
    mod_blk = (1, tm, d) if mod_rows == t and t > 1 else (1, 1, d)
    mod_map = (lambda i, j: (i, j, 0)) if mod_rows == t and t > 1 else (lambda i, j: (i, 0, 0))
    widths = (A_PAD, B_PAD, C_PAD, D_PAD)
    return pl.pallas_call(
        _inproj_kernel,
        grid=(b, t // tm),
        in_specs=[pl.BlockSpec((1, tm, d), lambda i, j: (i, j, 0)),
                  pl.BlockSpec(mod_blk, mod_map),
                  pl.BlockSpec(mod_blk, mod_map),
                  pl.BlockSpec(w_pad.shape, lambda i, j: (0, 0))],
        out_specs=[pl.BlockSpec((1, tm, w), lambda i, j: (i, j, 0)) for w in widths],
        out_shape=[jax.ShapeDtypeStruct((b, t, w), F32) for w in widths],
        compiler_params=_cparams(("parallel", "parallel")),
        name="inproj",
    )(x, sc, sh, w_pad)


def _rwkv_scan_kernel(r_ref, w_ref, k_ref, v_ref, kk_ref, kka_ref, s0_ref, y_ref, sfin_ref,
                      s_scr, y_scr, *, bg, tc, unroll):
    c = pl.program_id(1)
    rows = bg * HEAD_DIM

    @pl.when(c == 0)
    def _():
        s_scr[...] = s0_ref[...].reshape(rows, GROUP_WIDTH)

    li = lax.broadcasted_iota(jnp.int32, (2 * GROUP_WIDTH, GROUP_WIDTH), 0)
    lj = lax.broadcasted_iota(jnp.int32, (2 * GROUP_WIDTH, GROUP_WIDTH), 1)
    seg_ones2 = jnp.where((li % GROUP_WIDTH) // HEAD_DIM == lj // HEAD_DIM, 1.0, 0.0).astype(BF16)
    seg_ones1 = seg_ones2[:GROUP_WIDTH]
    lane = lax.broadcasted_iota(jnp.int32, (rows, GROUP_WIDTH), 1)
    row = lax.broadcasted_iota(jnp.int32, (rows, GROUP_WIDTH), 0)
    diag = jnp.where(row % HEAD_DIM == lane % HEAD_DIM, 1.0, 0.0)

    def seg_sum(x):
        hi = x.astype(BF16)
        lo = (x - hi.astype(F32)).astype(BF16)
        return jnp.dot(jnp.concatenate([hi, lo], axis=1), seg_ones2, preferred_element_type=F32)

    def bcast_rows(ref, t):
        return jnp.concatenate(
            [jnp.broadcast_to(ref[b, pl.ds(t, 1), :], (HEAD_DIM, GROUP_WIDTH)) for b in range(bg)], axis=0)

    def step(t):
        s = s_scr[...]
        kk = bcast_rows(kk_ref, t)
        sa = seg_sum(s * kk)
        vcol = jnp.dot((bcast_rows(v_ref, t) * diag).astype(BF16), seg_ones1, preferred_element_type=F32)
        s = s * bcast_rows(w_ref, t) - sa * bcast_rows(kka_ref, t) + vcol * bcast_rows(k_ref, t)
        s_scr[...] = s
        y = jnp.dot((s * bcast_rows(r_ref, t)).astype(BF16), seg_ones1, preferred_element_type=F32)
        y_scr[...] = jnp.where(lane % HEAD_DIM == t, y, y_scr[...])

    def outer(i, carry):
        for u in range(unroll):
            step(i * unroll + u)
        return carry

    y_scr[...] = jnp.zeros(y_scr.shape, F32)
    lax.fori_loop(0, tc // unroll, outer, 0)
    y_ref[...] = y_scr[...].reshape(bg, 1, HEAD_DIM, GROUP_WIDTH)

    @pl.when(c == pl.num_programs(1) - 1)
    def _():
        sfin_ref[...] = s_scr[...].reshape(bg, HEAD_DIM, GROUP_WIDTH)


def _rwkv_scan(r, w, k, v, kk, kka, s0):
    b, t, gw = r.shape
    bg = 8 if b % 8 == 0 else b
    tc = min(t, HEAD_DIM)
    unroll = 8 if tc % 8 == 0 else 1
    nc = t // tc
    s0l = s0.transpose(0, 2, 1, 3).reshape(b, HEAD_DIM, gw)
    seq = pl.BlockSpec((bg, tc, gw), lambda i, c: (i, c, 0))
    st = pl.BlockSpec((bg, HEAD_DIM, gw), lambda i, c: (i, 0, 0))
    y_raw, s_fin = pl.pallas_call(
        functools.partial(_rwkv_scan_kernel, bg=bg, tc=tc, unroll=unroll),
        grid=(b // bg, nc),
        in_specs=[seq] * 6 + [st],
        out_specs=[pl.BlockSpec((bg, 1, HEAD_DIM, gw), lambda i, c: (i, c, 0, 0)), st],
        out_shape=[jax.ShapeDtypeStruct((b, nc, HEAD_DIM, gw), F32),
                   jax.ShapeDtypeStruct((b, HEAD_DIM, gw), F32)],
        scratch_shapes=[pltpu.VMEM((bg * HEAD_DIM, gw), F32), pltpu.VMEM((bg * HEAD_DIM, gw), F32)],
        compiler_params=_cparams(("parallel", "arbitrary")),
        name="rwkv_scan",
    )(r, w, k, v, kk, kka, s0l)
    y = y_raw.reshape(b, nc, HEAD_DIM, N_HEADS, HEAD_DIM)[..., :tc]
    y = y.transpose(0, 1, 4, 3, 2).reshape(b, t, N_HEADS, HEAD_DIM)
    s_fin = s_fin.reshape(b, HEAD_DIM, N_HEADS, HEAD_DIM).transpose(0, 2, 1, 3)
    return y, s_fin


def _rwkv_mixer(za, shift_prev, s0, lp):
    b, t, _ = za.shape
    z_prev = jnp.concatenate([shift_prev[:, None], za[:, :-1]], axis=1)
    zs = za + lp['rwkv_mu'] * (z_prev - za)
    gw = GROUP_WIDTH
    r, k, v = zs[..., :gw], zs[..., gw:2 * gw], zs[..., 2 * gw:3 * gw]
    o = 3 * gw
    w_lo, a_lo, g_lo = zs[..., o:o + 32], zs[..., o + 32:o + 64], zs[..., o + 64:o + 128]
    decay = jnp.exp(-RWKV_DECAY_SCALE * jax.nn.sigmoid(lp['rwkv_w0'] + jnp.tanh(w_lo) @ lp['rwkv_w2']))
    a = jax.nn.sigmoid(lp['rwkv_a0'] + a_lo @ lp['rwkv_a2'])
    g = jax.nn.sigmoid(g_lo) @ lp['rwkv_g2']
    heads = lambda u: u.reshape(b, t, N_HEADS, HEAD_DIM)
    kk = heads(k) * lp['rwkv_kk']
    kk = kk / jnp.maximum(jnp.sqrt(jnp.sum(kk * kk, axis=-1, keepdims=True)), 1e-12)
    kk = kk.reshape(b, t, gw)
    k = k * (1.0 + (a - 1.0) * lp['rwkv_ka'].reshape(gw))
    y, s_fin = _rwkv_scan(r, decay, k, v, kk, kk * a, s0)
    mu = jnp.mean(y, axis=-1, keepdims=True)
    var = jnp.mean(jnp.square(y - mu), axis=-1, keepdims=True)
    yn = ((y - mu) * lax.rsqrt(var + RWKV_GN_EPS) * lp['rwkv_gn_g'].reshape(N_HEADS, HEAD_DIM)
          + lp['rwkv_gn_b'].reshape(N_HEADS, HEAD_DIM))
    bonus = jnp.sum(heads(r) * heads(k) * lp['rwkv_rk'], axis=-1, keepdims=True) * heads(v)
    out = (yn + bonus).reshape(b, t, gw) * g
    return out, s_fin, za[:, -1, :A_COLS]


KEY_TILE = 512
POS_SPLIT = 128
M_FLOOR = -1e29


def _aug_keys(kv, pos):
    lane = lax.broadcasted_iota(jnp.int32, kv.shape, 1)
    hi = (pos // POS_SPLIT).astype(F32)
    lo = (pos % POS_SPLIT).astype(F32)
    extra = jnp.where(lane == 0, hi, jnp.where(lane == 1, lo, 0.0))
    return jnp.concatenate([kv.astype(BF16), extra.astype(BF16)], axis=1)


def _aug_queries(q2, slopes, lane):
    rows = []
    for h in range(N_HEADS):
        pair = q2[:, (h // 2) * LANES:(h // 2 + 1) * LANES]
        if h % 2:
            pair = pltpu.roll(pair, HEAD_DIM, 1)
        extra = jnp.where(lane == 0, POS_SPLIT * slopes[h], jnp.where(lane == 1, slopes[h], 0.0))
        rows.append(jnp.concatenate([jnp.where(lane < HEAD_DIM, pair, 0.0), extra], axis=1))
    return jnp.concatenate(rows, axis=0).astype(BF16)


def _tile_heads(x):
    return jnp.concatenate([x] * N_HEADS, axis=0)


def _softmax_block(qa, ka, madd):
    s = lax.dot_general(qa, ka, _NT, preferred_element_type=F32) + _tile_heads(madd)
    m = jnp.maximum(jnp.max(s, axis=1, keepdims=True), M_FLOOR)
    e = jnp.exp(s - m)
    inv = 1.0 / jnp.maximum(jnp.sum(e, axis=1, keepdims=True), 1e-30)
    o = jnp.dot(e.astype(BF16), ka, preferred_element_type=F32) * inv
    return e, inv, o


def _flash_step(qa, ka, madd, m, l, acc):
    s = lax.dot_general(qa, ka, _NT, preferred_element_type=F32) + _tile_heads(madd)
    m_new = jnp.maximum(m, jnp.max(s, axis=1, keepdims=True))
    alpha = jnp.exp(m - m_new)
    p = jnp.exp(s - m_new)
    l = alpha * l + jnp.sum(p, axis=1, keepdims=True)
    acc = alpha * acc + jnp.dot(p.astype(BF16), ka, preferred_element_type=F32)
    return m_new, l, acc


def _flash_init(qb=Q_BLOCK):
    rows = N_HEADS * qb
    return (jnp.full((rows, 1), M_FLOOR, F32), jnp.zeros((rows, 1), F32), jnp.zeros((rows, 2 * LANES), F32))


def _heads_out(per_head, lane):
    pair = lambda a, b: jnp.where(lane < HEAD_DIM, pltpu.roll(a, HEAD_DIM, 1), b)
    return jnp.concatenate([pair(per_head[0], per_head[1]), pair(per_head[2], per_head[3])], axis=1)


def _head_rows(o, h, qb=Q_BLOCK):
    return o[h * qb:(h + 1) * qb, :LANES]


def _nsa_kernel(q_ref, ksv_ref, kwv_ref, gl_ref, kvc_ref, exp_ref, o_ref, ks_scr, kw_scr,
                *, t, n_win, n_top, slopes):
    qb = Q_BLOCK
    j = pl.program_id(1)
    q0 = j * qb

    @pl.when(j == 0)
    def _():
        for c in range(t // KEY_TILE):
            rows = slice(c * KEY_TILE, (c + 1) * KEY_TILE)
            pos = c * KEY_TILE + lax.broadcasted_iota(jnp.int32, (KEY_TILE, 1), 0)
            ks_scr[rows, :] = _aug_keys(ksv_ref[0, rows, :], pos)
            kw_scr[rows, :] = _aug_keys(kwv_ref[0, rows, :], pos)

    lane = lax.broadcasted_iota(jnp.int32, (qb, LANES), 1)
    qpos = q0 + lax.broadcasted_iota(jnp.int32, (qb, 1), 0)
    qa = _aug_queries(q_ref[0] * (HEAD_DIM ** -0.5), slopes, lane)
    gates = jax.nn.sigmoid(gl_ref[0])

    perm_end = lambda i: (2 * (i % HEAD_DIM) + i // HEAD_DIM) * NSA_CMP_BLOCK + (NSA_CMP_BLOCK - 1)
    kca = _aug_keys(kvc_ref[0], perm_end(lax.broadcasted_iota(jnp.int32, (LANES, 1), 0)))
    cend = perm_end(lax.broadcasted_iota(jnp.int32, (1, LANES), 1))
    e, inv, o_c = _softmax_block(qa, kca, jnp.where(cend <= qpos, 0.0, NEG_INF))
    p_c = e * inv
    imp = _head_rows(p_c, 0)
    for h in range(1, N_HEADS):
        imp = imp + _head_rows(p_c, h)
    imp = imp + pltpu.roll(imp, HEAD_DIM, 1)

    cur = qpos // NSA_SEL_BLOCK
    forced = (lane == 0) | (lane > cur - NSA_LOCAL_BLOCKS)
    score = jnp.where(lane > cur, NEG_INF, jnp.where(forced, FORCE_SCORE, imp))
    score = jnp.where(lane < HEAD_DIM, score, -3e38)
    rank = jnp.zeros((qb, LANES), F32)
    for i in range(NSA_SEL_BLOCK):
        col = score[:, i:i + 1]
        beats = (col > score) | ((col == score) & (lane > i))
        rank = rank + jnp.where(beats, 1.0, 0.0)
    sel = jnp.where((rank < n_top) & (lane < HEAD_DIM), 1.0, 0.0).astype(BF16)

    def sel_tile(kt, carry):
        k0 = pl.multiple_of(kt * KEY_TILE, KEY_TILE)
        kpos = k0 + lax.broadcasted_iota(jnp.int32, (1, KEY_TILE), 1)
        member = jnp.dot(sel, exp_ref[kt], preferred_element_type=F32) > 0.5
        madd = jnp.where(member & (kpos <= qpos), 0.0, NEG_INF)
        return _flash_step(qa, ks_scr[pl.ds(k0, KEY_TILE), :], madd, *carry)

    n_tiles = (q0 + qb - 1) // KEY_TILE + 1
    _, l_s, acc_s = lax.fori_loop(0, n_tiles, sel_tile, _flash_init())
    o_s = acc_s * (1.0 / jnp.maximum(l_s, 1e-30))

    kst = pl.multiple_of(jnp.maximum(q0 + qb - n_win, 0), qb)
    dw = qpos - (kst + lax.broadcasted_iota(jnp.int32, (1, n_win), 1))
    wadd = jnp.where((dw >= 0) & (dw <= NSA_WINDOW), 0.0, NEG_INF)
    _, _, o_w = _softmax_block(qa, kw_scr[pl.ds(kst, n_win), :], wadd)

    outs = []
    for h in range(N_HEADS):
        g = [gates[:, 3 * h + i:3 * h + i + 1] for i in range(3)]
        outs.append(g[0] * _head_rows(o_c, h) + g[1] * _head_rows(o_s, h) + g[2] * _head_rows(o_w, h))
    o_ref[0] = _heads_out(outs, lane)


def _nsa_prompt(zb, kvc_perm, slopes):
    b, t, _ = zb.shape
    assert t % KEY_TILE == 0 and t // NSA_CMP_BLOCK <= LANES and t <= 256 * POS_SPLIT
    n_win = min(NSA_WINDOW + Q_BLOCK, t)
    n_sel = -(-t // NSA_SEL_BLOCK)
    n_kt = t // KEY_TILE
    key_blk = (np.arange(t) // NSA_SEL_BLOCK).reshape(n_kt, 1, KEY_TILE)
    expand = jnp.asarray((np.arange(LANES)[None, :, None] == key_blk).astype(np.float32), BF16)
    full = lambda c: pl.BlockSpec((1, t, LANES), lambda i, j, c=c: (i, 0, c))
    return pl.pallas_call(
        functools.partial(_nsa_kernel, t=t, n_win=n_win, n_top=min(NSA_TOPN, n_sel), slopes=slopes),
        grid=(b, t // Q_BLOCK),
        in_specs=[pl.BlockSpec((1, Q_BLOCK, 2 * LANES), lambda i, j: (i, j, 0)),
                  full(3), full(4),
                  pl.BlockSpec((1, Q_BLOCK, LANES), lambda i, j: (i, j, 5)),
                  pl.BlockSpec((1, LANES, LANES), lambda i, j: (i, 0, 0)),
                  pl.BlockSpec((n_kt, LANES, KEY_TILE), lambda i, j: (0, 0, 0))],
        out_specs=pl.BlockSpec((1, Q_BLOCK, GROUP_WIDTH), lambda i, j: (i, j, 0)),
        out_shape=jax.ShapeDtypeStruct((b, t, GROUP_WIDTH), F32),
        scratch_shapes=[pltpu.VMEM((t, 2 * LANES), BF16), pltpu.VMEM((t, 2 * LANES), BF16)],
        compiler_params=_cparams(("parallel", "arbitrary")),
        name="nsa_prompt",
    )(zb, zb, zb, zb, kvc_perm, expand)


def _nsa_compress(rows, lp):
    b, l = rows.shape[:2]
    nc = l // NSA_CMP_BLOCK
    blk = rows[:, :nc * NSA_CMP_BLOCK].reshape(b, nc, NSA_CMP_BLOCK, 2, HEAD_DIM) + lp['nsa_pe']
    flat = jnp.swapaxes(blk, 2, 3).reshape(b, nc, 2, NSA_CMP_BLOCK * HEAD_DIM)
    hid = jax.nn.gelu(jnp.einsum('bnci,cih->bnch', flat, lp['nsa_w1']))
    return jnp.einsum('bnch,chd->bncd', hid, lp['nsa_w2'])


def _dsa_keyprep_kernel(kv_ref, ki_ref, ka_ref, kip_ref):
    pos = pl.program_id(1) * KEY_TILE + lax.broadcasted_iota(jnp.int32, (KEY_TILE, 1), 0)
    ka_ref[0] = _aug_keys(kv_ref[0], pos)
    ki = ki_ref[0]
    if ki.shape[1] == LANES:
        ki = jnp.where(lax.broadcasted_iota(jnp.int32, ki.shape, 1) < IDX_DIM, ki, 0.0)
    else:
        ki = jnp.concatenate([ki, jnp.zeros((KEY_TILE, LANES - IDX_DIM), F32)], axis=1)
    hi = ki.astype(BF16).astype(F32)
    kip_ref[0] = (hi + pltpu.roll(ki - hi, IDX_DIM, 1) + pltpu.roll(hi, 2 * IDX_DIM, 1)).astype(BF16)


def _dsa_keyprep(kv_src, kv_col, ki_src, ki_col):
    b, t, _ = kv_src.shape
    ki_w = min(ki_src.shape[2], LANES)
    return pl.pallas_call(
        _dsa_keyprep_kernel,
        grid=(b, t // KEY_TILE),
        in_specs=[pl.BlockSpec((1, KEY_TILE, LANES), lambda i, c: (i, c, kv_col)),
                  pl.BlockSpec((1, KEY_TILE, ki_w), lambda i, c: (i, c, ki_col))],
        out_specs=[pl.BlockSpec((1, KEY_TILE, 2 * LANES), lambda i, c: (i, c, 0)),
                   pl.BlockSpec((1, KEY_TILE, LANES), lambda i, c: (i, c, 0))],
        out_shape=[jax.ShapeDtypeStruct((b, t, 2 * LANES), BF16), jax.ShapeDtypeStruct((b, t, LANES), BF16)],
        compiler_params=_cparams(("parallel", "parallel")),
        name="dsa_keyprep",
    )(kv_src, ki_src)


def _dsa_kernel(q_ref, ka_ref, qi_ref, kip_ref, wiq_ref, o_ref, skey_scr, *, t, qb, q_base, k_sel, slopes):
    n_sub = KEY_TILE // LANES
    ka_scr, kip_scr = ka_ref.at[0], kip_ref.at[0]
    lane = lax.broadcasted_iota(jnp.int32, (qb, LANES), 1)
    if q_base is None:
        q0 = pl.program_id(1) * qb
        qpos = q0 + lax.broadcasted_iota(jnp.int32, (qb, 1), 0)
        n_tiles = (q0 + qb - 1) // KEY_TILE + 1
    else:
        qpos = jnp.full((qb, 1), q_base, jnp.int32)
        n_tiles = t // KEY_TILE
    tile_pos = lambda kt: kt * KEY_TILE + lax.broadcasted_iota(jnp.int32, (1, KEY_TILE), 1)

    qi = qi_ref[0]
    wq = wiq_ref[0]
    packed_q = []
    for h in range(IDX_HEADS):
        qh = qi if h == 0 else pltpu.roll(qi, LANES - IDX_DIM * h, 1)
        qh = jnp.where(lane < IDX_DIM, qh, 0.0)
        hi = qh.astype(BF16).astype(F32)
        packed_q.append(hi + pltpu.roll(hi, IDX_DIM, 1) + pltpu.roll(qh - hi, 2 * IDX_DIM, 1))
    packed_q = jnp.concatenate(packed_q, axis=0).astype(BF16)
    w_cols = [wq[:, IDX_DIM + h:IDX_DIM + h + 1] for h in range(IDX_HEADS)]

    def idx_tile(kt, carry):
        k0 = pl.multiple_of(kt * KEY_TILE, KEY_TILE)
        rel = jnp.maximum(lax.dot_general(packed_q, kip_scr[pl.ds(k0, KEY_TILE), :], _NT,
                                          preferred_element_type=F32), 0.0)
        score = w_cols[0] * rel[:qb]
        for h in range(1, IDX_HEADS):
            score = score + w_cols[h] * rel[h * qb:(h + 1) * qb]
        score = jnp.where(score == 0.0, 0.0, score)
        score = jnp.where(tile_pos(kt) <= qpos, score, NEG_INF)
        bits = pltpu.bitcast(score, jnp.int32)
        skey_scr[kt] = bits ^ ((bits >> 31) & 0x7FFFFFFF)
        return carry

    lax.fori_loop(0, n_tiles, idx_tile, 0)

    kf = float(k_sel)

    def count(c, strict):
        cb = jnp.broadcast_to(c, (qb, LANES))

        def body(kt, acc):
            x = skey_scr[kt]
            for u in range(n_sub):
                xu = x[:, u * LANES:(u + 1) * LANES]
                acc = acc + jnp.where((xu > cb) if strict else (xu >= cb), 1.0, 0.0)
            return acc

        acc = lax.fori_loop(0, n_tiles, body, jnp.zeros((qb, LANES), F32))
        return jnp.sum(acc, axis=1, keepdims=True)

    int_min = jnp.int32(-2 ** 31)
    tau = jnp.where(count(jnp.zeros((qb, 1), jnp.int32), False) >= kf, jnp.int32(0), int_min)

    def bit_step(i, tau):
        cand = tau + lax.shift_left(jnp.int32(1), 30 - i)
        return jnp.where(count(cand, False) >= kf, cand, tau)

    tau = lax.fori_loop(0, 31, bit_step, tau)
    need = kf - count(tau, True)
    taub = jnp.broadcast_to(tau, (qb, LANES))
    ui = lax.broadcasted_iota(jnp.int32, (LANES, LANES), 0)
    uj = lax.broadcasted_iota(jnp.int32, (LANES, LANES), 1)
    upper = jnp.where(ui <= uj, 1.0, 0.0).astype(BF16)
    qa = _aug_queries(q_ref[0] * (HEAD_DIM ** -0.5), slopes, lane)

    def att_tile(kt, carry):
        ties, m, l, acc = carry
        k0 = pl.multiple_of(kt * KEY_TILE, KEY_TILE)
        x = skey_scr[kt]
        parts = []
        for u in range(n_sub):
            xu = x[:, u * LANES:(u + 1) * LANES]
            eq = xu == taub
            eqf = jnp.where(eq, 1.0, 0.0)
            before = jnp.dot(eqf.astype(BF16), upper, preferred_element_type=F32) + ties
            parts.append(jnp.where((xu > taub) | (eq & (before <= need)), 0.0, NEG_INF))
            ties = ties + jnp.sum(eqf, axis=1, keepdims=True)
        madd = jnp.where(tile_pos(kt) <= qpos, jnp.concatenate(parts, axis=1), NEG_INF)
        return (ties,) + _flash_step(qa, ka_scr[pl.ds(k0, KEY_TILE), :], madd, m, l, acc)

    _, _, l, acc = lax.fori_loop(0, n_tiles, att_tile, (jnp.zeros((qb, 1), F32),) + _flash_init(qb))
    o = acc * (1.0 / jnp.maximum(l, 1e-30))
    o_ref[0] = _heads_out([_head_rows(o, h, qb) for h in range(N_HEADS)], lane)


def _dsa_attend(zq, ka, kip, slopes, *, qb, q_base, n_keys):
    b, tq, _ = zq.shape
    t = ka.shape[1]
    assert t % KEY_TILE == 0 and t <= 256 * POS_SPLIT and tq % qb == 0
    k_sel = min(DSA_TOPK, n_keys // 4)
    assert k_sel <= KEY_TILE
    full = lambda w: pl.BlockSpec((1, t, w), lambda i, j: (i, 0, 0))
    qblk = lambda c: pl.BlockSpec((1, qb, LANES), lambda i, j, c=c: (i, j, c))
    return pl.pallas_call(
        functools.partial(_dsa_kernel, t=t, qb=qb, q_base=q_base, k_sel=k_sel, slopes=slopes),
        grid=(b, tq // qb),
        in_specs=[pl.BlockSpec((1, qb, 2 * LANES), lambda i, j: (i, j, 0)),
                  full(2 * LANES), qblk(3), full(LANES), qblk(4)],
        out_specs=pl.BlockSpec((1, qb, GROUP_WIDTH), lambda i, j: (i, j, 0)),
        out_shape=jax.ShapeDtypeStruct((b, tq, GROUP_WIDTH), F32),
        scratch_shapes=[pltpu.VMEM((t // KEY_TILE, qb, KEY_TILE), jnp.int32)],
        compiler_params=_cparams(("parallel", "parallel")),
        name="dsa_attend",
    )(zq, ka, zq, kip, zq)


def _dsa_prompt(zc, slopes):
    ka, kip = _dsa_keyprep(zc, 2, zc, 4)
    return _dsa_attend(zc, ka, kip, slopes, qb=Q_BLOCK, q_base=None, n_keys=zc.shape[1])


def _outproj_kernel(ya_ref, yb_ref, yc_ref, yd_ref, x_ref, g1_ref, sc2_ref, sh2_ref, wo_ref, lg_ref, lb_ref,
                    wr_ref, br_ref, x1_ref, h2_ref, logit_ref, *, alpha):
    mix = jnp.zeros(x_ref.shape[1:], F32)
    for i, ref in enumerate((ya_ref, yb_ref, yc_ref, yd_ref)):
        mix = mix + jnp.dot(ref[0].astype(BF16), wo_ref[i], preferred_element_type=F32)
    x1 = _ln_rows(alpha * x_ref[0] + g1_ref[0] * mix) * lg_ref[...] + lb_ref[...]
    x1_ref[0] = x1
    h2 = _ln_rows(x1) * (1.0 + sc2_ref[0]) + sh2_ref[0]
    h2_ref[0] = h2.astype(BF16)
    logit_ref[0] = jnp.dot(h2, wr_ref[...], preferred_element_type=F32,
                           precision=lax.Precision.HIGHEST) + br_ref[...]


def _outproj(ys, x, g1, sc2, sh2, wo, ln_g, ln_b, wr_pad, br_pad, alpha):
    b, t, d = x.shape
    tm = min(t, 512)
    per_row = g1.shape[1] == t and t > 1
    mod_blk = (1, tm, d) if per_row else (1, 1, d)
    mod_map = (lambda i, j: (i, j, 0)) if per_row else (lambda i, j: (i, 0, 0))
    mod = pl.BlockSpec(mod_blk, mod_map)
    row = lambda w: pl.BlockSpec((1, tm, w), lambda i, j: (i, j, 0))
    const = lambda a: pl.BlockSpec(a.shape, lambda i, j: (0,) * a.ndim)
    return pl.pallas_call(
        functools.partial(_outproj_kernel, alpha=alpha),
        grid=(b, t // tm),
        in_specs=[row(GROUP_WIDTH)] * 4 + [row(d), mod, mod, mod, const(wo), const(ln_g), const(ln_b),
                                           const(wr_pad), const(br_pad)],
        out_specs=[row(d), row(d), row(LANES)],
        out_shape=[jax.ShapeDtypeStruct((b, t, d), F32), jax.ShapeDtypeStruct((b, t, d), BF16),
                   jax.ShapeDtypeStruct((b, t, LANES), F32)],
        compiler_params=_cparams(("parallel", "parallel")),
        name="outproj",
    )(*ys, x, g1, sc2, sh2, wo, ln_g, ln_b, wr_pad, br_pad)


def _moe_kernel(be_ref, nu_ref, x_ref, wgu_ref, bgu_ref, wd_ref, bd_ref, o_ref, wgu_bf, wd_bf):
    i = pl.program_id(0)
    ff = wd_ref.shape[1]

    @pl.when((i == 0) | (be_ref[i] != be_ref[jnp.maximum(i - 1, 0)]))
    def _():
        wgu_bf[...] = wgu_ref[0].astype(BF16)
        wd_bf[...] = wd_ref[0].astype(BF16)

    @pl.when(i < nu_ref[0])
    def _():
        gu = jnp.dot(x_ref[...], wgu_bf[...], preferred_element_type=F32) + bgu_ref[0]
        gt = jnp.minimum(gu[:, :ff], SWIGLU_LIMIT)
        up = jnp.clip(gu[:, ff:], -SWIGLU_LIMIT, SWIGLU_LIMIT)
        act = (up + 1.0) * gt * jax.nn.sigmoid(SWIGLU_ALPHA * gt)
        o_ref[...] = jnp.dot(act.astype(BF16), wd_bf[...], preferred_element_type=F32) + bd_ref[0]

    @pl.when(i >= nu_ref[0])
    def _():
        o_ref[...] = jnp.zeros(o_ref.shape, F32)


def _moe_experts(xg, blk_exp, n_used, wgu, bgu, wd, bd, blk):
    cap, d = xg.shape
    e, _, ff2 = wgu.shape
    ff = ff2 // 2
    grid_spec = pltpu.PrefetchScalarGridSpec(
        num_scalar_prefetch=2,
        grid=(cap // blk,),
        in_specs=[pl.BlockSpec((blk, d), lambda i, be, nu: (i, 0)),
                  pl.BlockSpec((1, d, ff2), lambda i, be, nu: (be[i], 0, 0)),
                  pl.BlockSpec((1, 1, ff2), lambda i, be, nu: (be[i], 0, 0)),
                  pl.BlockSpec((1, ff, d), lambda i, be, nu: (be[i], 0, 0)),
                  pl.BlockSpec((1, 1, d), lambda i, be, nu: (be[i], 0, 0))],
        out_specs=pl.BlockSpec((blk, d), lambda i, be, nu: (i, 0)),
        scratch_shapes=[pltpu.VMEM((d, ff2), BF16), pltpu.VMEM((ff, d), BF16)],
    )
    return pl.pallas_call(
        _moe_kernel,
        grid_spec=grid_spec,
        out_shape=jax.ShapeDtypeStruct((cap, d), F32),
        compiler_params=_cparams(("arbitrary",)),
        name="moe_experts",
    )(blk_exp, n_used, xg, wgu, bgu.reshape(e, 1, ff2), wd, bd.reshape(e, 1, d))


def _slot_tokens_kernel(pos_ref, out_ref, *, chunk, shift):
    c = pl.program_id(0)

    @pl.when(c == 0)
    def _():
        def zero(i, carry):
            out_ref[i] = jnp.int32(0)
            return carry
        lax.fori_loop(0, out_ref.shape[0], zero, 0, unroll=8)

    base = c * chunk

    def body(i, carry):
        out_ref[pos_ref[i]] = lax.shift_right_logical(base + i, shift)
        return carry

    lax.fori_loop(0, chunk, body, 0, unroll=8)


def _slot_tokens(pos, cap):
    m = pos.shape[0]
    assert TOP_K & (TOP_K - 1) == 0
    chunk = min(m, 8192)
    assert m % chunk == 0
    return pl.pallas_call(
        functools.partial(_slot_tokens_kernel, chunk=chunk, shift=TOP_K.bit_length() - 1),
        grid=(m // chunk,),
        in_specs=[pl.BlockSpec((chunk,), lambda c: (c,), memory_space=pltpu.SMEM)],
        out_specs=pl.BlockSpec((cap,), lambda c: (0,), memory_space=pltpu.SMEM),
        out_shape=jax.ShapeDtypeStruct((cap,), jnp.int32),
        compiler_params=pltpu.CompilerParams(dimension_semantics=("arbitrary",)),
        name="slot_tokens",
    )(pos)


def _moe(h2, logits, mw, layer, blk):
    n, d = h2.shape
    n_exp = logits.shape[1]
    top_v, top_e = lax.top_k(logits, TOP_K)
    gate = jax.nn.softmax(top_v, axis=-1)
    m = n * TOP_K
    cap = -(-(m + n_exp * (blk - 1)) // blk) * blk
    e_flat = top_e.reshape(m)
    onehot = (e_flat[:, None] == jnp.arange(n_exp)[None, :]).astype(jnp.int32)
    csum = jnp.cumsum(onehot, axis=0)
    counts = csum[-1]
    rank = jnp.take_along_axis(csum, e_flat[:, None], axis=1)[:, 0] - 1
    padded = (counts + blk - 1) // blk * blk
    pend = jnp.cumsum(padded)
    pstart = pend - padded
    pos = (pstart[e_flat] + rank).astype(jnp.int32)
    nblk = cap // blk
    blk_start = jnp.arange(nblk, dtype=jnp.int32) * blk
    blk_exp = jnp.minimum(jnp.sum((pend[None, :] <= blk_start[:, None]).astype(jnp.int32), axis=1), n_exp - 1)
    row_tok = _slot_tokens(pos, cap)
    n_used = (pend[-1] // blk).astype(jnp.int32).reshape(1)
    xg = h2.at[row_tok].get(mode='promise_in_bounds')
    out = _moe_experts(xg, (blk_exp + layer * n_exp).astype(jnp.int32), n_used, *mw, blk)
    pos = pos.reshape(n, TOP_K)
    picked = [out.at[pos[:, j]].get(mode='promise_in_bounds') for j in range(TOP_K)]
    return picked, jnp.pad(gate, ((0, 0), (0, LANES - TOP_K)))


def _moe_block_rows(n_assign, n_exp):
    per_expert = max(1, n_assign // n_exp)
    return min(512, max(16, 1 << (per_expert.bit_length() - 1)))


def _resln_kernel(x_ref, gate_ref, *refs, alpha):
    picked, (g_ref, lg_ref, lb_ref, o_ref) = refs[:TOP_K], refs[TOP_K:]
    gate = gate_ref[0]
    y = gate[:, 0:1] * picked[0][0]
    for j in range(1, TOP_K):
        y = y + gate[:, j:j + 1] * picked[j][0]
    o_ref[0] = _ln_rows(alpha * x_ref[0] + g_ref[0] * y) * lg_ref[...] + lb_ref[...]


def _resln(x, picked, gate, g, ln_g, ln_b, alpha):
    b, t, d = x.shape
    tm = min(t, 512)
    per_row = g.shape[1] == t and t > 1
    mod = pl.BlockSpec((1, tm, d) if per_row else (1, 1, d),
                       (lambda i, j: (i, j, 0)) if per_row else (lambda i, j: (i, 0, 0)))
    row = lambda w: pl.BlockSpec((1, tm, w), lambda i, j: (i, j, 0))
    const = pl.BlockSpec((1, d), lambda i, j: (0, 0))
    return pl.pallas_call(
        functools.partial(_resln_kernel, alpha=alpha),
        grid=(b, t // tm),
        in_specs=[row(d), row(LANES)] + [row(d)] * TOP_K + [mod, const, const],
        out_specs=row(d),
        out_shape=jax.ShapeDtypeStruct((b, t, d), F32),
        compiler_params=_cparams(("parallel", "parallel")),
        name="resln",
    )(x, gate.reshape(b, t, LANES), *[p.reshape(b, t, d) for p in picked], g, ln_g, ln_b)


def _conv_mixer(zd, buf, conv_w):
    t = zd.shape[1]
    ch = GROUP_WIDTH
    b_gate, c_gate, xin = zd[..., :ch], zd[..., ch:2 * ch], zd[..., 2 * ch:3 * ch]
    ue = jnp.concatenate([buf, c_gate * xin], axis=1)
    y = sum(ue[:, j:j + t] * conv_w[j] for j in range(CONV_K))
    return b_gate * y, ue[:, t:]


def _nsa_compress_pages(pages, lp):
    b, p, _, dh, rows = pages.shape
    nj = rows // NSA_CMP_BLOCK
    x = pages.reshape(b, p, 2, dh, nj, NSA_CMP_BLOCK) + lp['nsa_pe'].transpose(1, 2, 0)[:, :, None, :]
    w1 = lp['nsa_w1'].reshape(2, NSA_CMP_BLOCK, dh, -1)
    hid = jax.nn.gelu(jnp.einsum('bpcdjr,crdh->bpjch', x, w1))
    return jnp.einsum('bpjch,chd->bpjcd', hid, lp['nsa_w2']).reshape(b, p * nj, 2, dh)


def _masked_softmax(s, valid):
    p = jax.nn.softmax(jnp.where(valid, s, NEG_INF), axis=-1)
    return jnp.where(valid, p, 0.0)


def _nsa_attend_sample(q, gates, qpos, kvc, slc_pool, page_table, new_slc, kvw, wpos, slopes, n_keys):
    b, tq = q.shape[:2]
    q = q * HEAD_DIM ** -0.5
    sl = jnp.asarray(slopes, F32)[:, None, None]
    nc = kvc.shape[1]
    cend = jnp.arange(nc) * NSA_CMP_BLOCK + (NSA_CMP_BLOCK - 1)
    s_c = jnp.einsum('bqhd,bnd->bhqn', q, kvc[:, :, 0]) - sl * (qpos[:, None] - cend[None, :]).astype(F32)
    p_c = _masked_softmax(s_c, cend[None, :] <= qpos[:, None])
    o_c = jnp.einsum('bhqn,bnd->bqhd', p_c, kvc[:, :, 1])
    ratio = NSA_SEL_BLOCK // NSA_CMP_BLOCK
    n_sel = -(-n_keys // NSA_SEL_BLOCK)
    imp = jnp.pad(p_c.sum(1), ((0, 0), (0, 0), (0, n_sel * ratio - nc))).reshape(b, tq, n_sel, ratio).sum(-1)
    blk = jnp.arange(n_sel)[None, :]
    cur = (qpos // NSA_SEL_BLOCK)[:, None]
    forced = (blk == 0) | (blk > cur - NSA_LOCAL_BLOCKS)
    score = jnp.where(blk > cur, NEG_INF, jnp.where(forced, FORCE_SCORE, imp))
    n_top = min(NSA_TOPN, n_sel)
    _, sel = lax.top_k(score, n_top)
    assert tq == 1 and PAGE_SIZE == 2 * NSA_SEL_BLOCK
    sel = sel[:, 0]
    n_pages = page_table.shape[1]
    n_past = n_pages * PAGE_SIZE
    pages = jnp.take_along_axis(page_table, jnp.minimum(sel // 2, n_pages - 1), axis=1)
    slabs = slc_pool[pages]
    odd = (sel % 2 == 1)[:, :, None, None, None]
    half = jnp.where(odd, slabs[..., NSA_SEL_BLOCK:], slabs[..., :NSA_SEL_BLOCK])
    kpos = sel[:, :, None] * NSA_SEL_BLOCK + jnp.arange(NSA_SEL_BLOCK)
    q1 = q[:, 0]
    s_past = jnp.einsum('bhd,bndr->bhnr', q1, half[:, :, 0]) + sl * kpos[:, None].astype(F32)
    s_past = jnp.where((kpos < n_past)[:, None], s_past, NEG_INF).reshape(b, N_HEADS, -1)
    new_sel = jnp.any(sel == n_past // NSA_SEL_BLOCK, axis=1)[:, None, None]
    s_new = jnp.einsum('bhd,bd->bh', q1, new_slc[:, 0, 0])[..., None] + sl[None, :, 0] * float(n_past)
    s_all = jnp.concatenate([s_past, jnp.where(new_sel, s_new, NEG_INF)], axis=-1)
    p_s = _masked_softmax(s_all, s_all > 0.5 * NEG_INF)
    o_s = (jnp.einsum('bhnr,bndr->bhd', p_s[..., :-1].reshape(b, N_HEADS, n_top, NSA_SEL_BLOCK), half[:, :, 1])
           + p_s[..., -1:] * new_slc[:, 0, 1][:, None, :])[:, None]
    dw = qpos[:, None] - wpos[None, :]
    s_w = jnp.einsum('bqhd,bnd->bhqn', q, kvw[:, :, 0]) - sl * dw.astype(F32)
    p_w = _masked_softmax(s_w, (dw >= 0) & (dw <= NSA_WINDOW) & (wpos[None, :] >= 0))
    o_w = jnp.einsum('bhqn,bnd->bqhd', p_w, kvw[:, :, 1])
    return gates[..., 0:1] * o_c + gates[..., 1:2] * o_s + gates[..., 2:3] * o_w


def _dsa_sample_kernel(pt_ref, zt_ref, kv_hbm, idx_hbm, o_ref, kvbuf, idxbuf, sem, score_scr, s_scr, p_scr,
                       *, layer, n_pages, k_sel, slopes):
    b = pl.program_id(0)
    nb = pl.num_programs(0)
    slot = b % 2
    ps = PAGE_SIZE
    gw = GROUP_WIDTH

    def page_copies(seq, sl, p):
        pg = pt_ref[seq, p]
        return (pltpu.make_async_copy(kv_hbm.at[layer, pg], kvbuf.at[sl, p], sem.at[0, sl]),
                pltpu.make_async_copy(idx_hbm.at[layer, pg], idxbuf.at[sl, p], sem.at[1, sl]))

    def fetch(seq, sl):
        def body(p, carry):
            for cp in page_copies(seq, sl, p):
                cp.start()
            return carry
        lax.fori_loop(0, n_pages, body, 0)

    @pl.when(b == 0)
    def _():
        fetch(0, 0)

    @pl.when(b + 1 < nb)
    def _():
        fetch(b + 1, 1 - slot)

    def wait_body(p, carry):
        for cp in page_copies(b, slot, p):
            cp.wait()
        return carry

    lax.fori_loop(0, n_pages, wait_body, 0)

    zt = zt_ref[0]
    o_kv = gw
    o_qi = gw + 2 * HEAD_DIM
    o_ki = o_qi + IDX_HEADS * IDX_DIM
    o_wi = o_ki + IDX_DIM
    q_cols = [zt[h * HEAD_DIM:(h + 1) * HEAD_DIM] * (HEAD_DIM ** -0.5) for h in range(N_HEADS)]
    qi_cols = [zt[o_qi + h * IDX_DIM:o_qi + (h + 1) * IDX_DIM] for h in range(IDX_HEADS)]
    w_idx = [zt[o_wi + h:o_wi + h + 1] for h in range(IDX_HEADS)]
    k_new, v_new = zt[o_kv:o_kv + HEAD_DIM], zt[o_kv + HEAD_DIM:o_kv + 2 * HEAD_DIM]
    ki_new = zt[o_ki:o_ki + IDX_DIM]
    colsum = lambda x: jnp.sum(x, axis=0, keepdims=True)
    total = lambda x: jnp.sum(colsum(x), axis=1, keepdims=True)
    unzero = lambda x: jnp.where(x == 0.0, 0.0, x)

    def order_key(x):
        bits = pltpu.bitcast(x, jnp.int32)
        return bits ^ ((bits >> 31) & 0x7FFFFFFF)

    qi_wide = [jnp.broadcast_to(c, (IDX_DIM, ps)) for c in qi_cols]
    w_wide = [jnp.broadcast_to(w, (1, ps)) for w in w_idx]

    def score_page(p, carry):
        kit = idxbuf[slot, p]
        row = w_wide[0] * jnp.maximum(colsum(kit * qi_wide[0]), 0.0)
        for h in range(1, IDX_HEADS):
            row = row + w_wide[h] * jnp.maximum(colsum(kit * qi_wide[h]), 0.0)
        score_scr[pl.ds(p, 1), :] = row
        return carry

    lax.fori_loop(0, n_pages, score_page, 0)
    sc_new = w_idx[0] * jnp.maximum(colsum(ki_new * qi_cols[0]), 0.0)
    for h in range(1, IDX_HEADS):
        sc_new = sc_new + w_idx[h] * jnp.maximum(colsum(ki_new * qi_cols[h]), 0.0)
    key = order_key(unzero(score_scr[...]))
    key_new = order_key(unzero(sc_new))

    kf = float(k_sel)

    def count(c, strict):
        past = jnp.where((key > c) if strict else (key >= c), 1.0, 0.0)
        new = jnp.where((key_new > c) if strict else (key_new >= c), 1.0, 0.0)
        return total(past) + new

    int_min = jnp.int32(-2 ** 31)
    tau = jnp.where(count(jnp.zeros((1, 1), jnp.int32), False) >= kf, jnp.int32(0), int_min)

    def bit_step(i, tau):
        cand = tau + lax.shift_left(jnp.int32(1), 30 - i)
        return jnp.where(count(cand, False) >= kf, cand, tau)

    tau = lax.fori_loop(0, 31, bit_step, tau)
    need = kf - count(tau, True)
    eq = key == tau
    eqf = jnp.where(eq, 1.0, 0.0).astype(BF16)
    ri = lax.broadcasted_iota(jnp.int32, (ps, ps), 0)
    ci = lax.broadcasted_iota(jnp.int32, (ps, ps), 1)
    in_row = jnp.dot(eqf, jnp.where(ri <= ci, 1.0, 0.0).astype(BF16), preferred_element_type=F32)
    row_tot = jnp.broadcast_to(in_row[:, ps - 1:ps], (n_pages, ps)).astype(BF16)
    pi = lax.broadcasted_iota(jnp.int32, (n_pages, n_pages), 0)
    pj = lax.broadcasted_iota(jnp.int32, (n_pages, n_pages), 1)
    rows_before = jnp.dot(jnp.where(pj < pi, 1.0, 0.0).astype(BF16), row_tot, preferred_element_type=F32)
    sel = (key > tau) | (eq & (in_row + rows_before <= need))
    eq_new = key_new == tau
    sel_new = (key_new > tau) | (eq_new & (total(jnp.where(eq, 1.0, 0.0)) + 1.0 <= need))

    q_wide = [jnp.broadcast_to(c, (HEAD_DIM, ps)) for c in q_cols]

    def qk_page(p, carry):
        kt = kvbuf[slot, p, 0]
        for h in range(N_HEADS):
            s_scr[h, pl.ds(p, 1), :] = colsum(kt * q_wide[h])
        return carry

    lax.fori_loop(0, n_pages, qk_page, 0)
    kpos = (lax.broadcasted_iota(jnp.int32, (n_pages, ps), 0) * ps
            + lax.broadcasted_iota(jnp.int32, (n_pages, ps), 1)).astype(F32)
    e_new, inv = [], []
    for h in range(N_HEADS):
        s = jnp.where(sel, s_scr[h] + slopes[h] * kpos, NEG_INF)
        s_n = jnp.where(sel_new, colsum(k_new * q_cols[h]) + slopes[h] * float(n_pages * ps), NEG_INF)
        m = jnp.maximum(jnp.max(jnp.max(s, axis=0, keepdims=True), axis=1, keepdims=True), s_n)
        e = jnp.exp(s - m)
        e_n = jnp.exp(s_n - m)
        p_scr[h] = e
        e_new.append(e_n)
        inv.append(1.0 / (total(e) + e_n))

    def pv_page(p, accs):
        vt = kvbuf[slot, p, 1]
        return tuple(accs[h] + vt * p_scr[h, pl.ds(p, 1), :] for h in range(N_HEADS))

    accs = lax.fori_loop(0, n_pages, pv_page, tuple(jnp.zeros((HEAD_DIM, ps), F32) for _ in range(N_HEADS)))
    outs = [(jnp.sum(accs[h], axis=1, keepdims=True) + e_new[h] * v_new) * inv[h] for h in range(N_HEADS)]
    o_ref[0] = jnp.concatenate(outs, axis=0)


def _dsa_sample(zc, cache_dsa_kv, cache_dsa_idx, layer, page_table, slopes):
    b, t, c_pad = zc.shape
    assert t == 1
    n_pages = page_table.shape[1]
    n_keys = n_pages * PAGE_SIZE + t
    k_sel = min(DSA_TOPK, n_keys // 4)
    kv_t = cache_dsa_kv.transpose(0, 1, 3, 4, 2)
    idx_t = cache_dsa_idx.transpose(0, 1, 3, 2)
    grid_spec = pltpu.PrefetchScalarGridSpec(
        num_scalar_prefetch=1,
        grid=(b,),
        in_specs=[pl.BlockSpec((1, c_pad, 1), lambda i, pt: (i, 0, 0)),
                  pl.BlockSpec(memory_space=pl.ANY), pl.BlockSpec(memory_space=pl.ANY)],
        out_specs=pl.BlockSpec((1, GROUP_WIDTH, 1), lambda i, pt: (i, 0, 0)),
        scratch_shapes=[pltpu.VMEM((2, n_pages, 2, HEAD_DIM, PAGE_SIZE), F32),
                        pltpu.VMEM((2, n_pages, IDX_DIM, PAGE_SIZE), F32),
                        pltpu.SemaphoreType.DMA((2, 2)),
                        pltpu.VMEM((n_pages, PAGE_SIZE), F32),
                        pltpu.VMEM((N_HEADS, n_pages, PAGE_SIZE), F32),
                        pltpu.VMEM((N_HEADS, n_pages, PAGE_SIZE), F32)],
    )
    out = pl.pallas_call(
        functools.partial(_dsa_sample_kernel, layer=layer, n_pages=n_pages, k_sel=k_sel, slopes=slopes),
        grid_spec=grid_spec,
        out_shape=jax.ShapeDtypeStruct((b, GROUP_WIDTH, 1), F32),
        compiler_params=_cparams(("arbitrary",)),
        name="dsa_sample",
    )(page_table, zc.transpose(0, 2, 1), kv_t, idx_t)
    return out.transpose(0, 2, 1)


def _kv_rows(u):
    return u.reshape(u.shape[0], u.shape[1], 2, HEAD_DIM)


def _mixers_prompt(za, zb, zc, zd, lp, sl_nsa, sl_dsa):
    b, t, _ = za.shape
    gw = GROUP_WIDTH
    ya, s_rwkv, shift = _rwkv_mixer(za, jnp.zeros((b, A_PAD), F32),
                                    jnp.zeros((b, N_HEADS, HEAD_DIM, HEAD_DIM), F32), lp)
    kv_c = _kv_rows(zb[..., gw:gw + 2 * HEAD_DIM])
    kv_s = _kv_rows(zb[..., gw + 2 * HEAD_DIM:gw + 4 * HEAD_DIM])
    kv_w = _kv_rows(zb[..., gw + 4 * HEAD_DIM:gw + 6 * HEAD_DIM])
    kvc = _nsa_compress(kv_c, lp).reshape(b, -1, 2 * HEAD_DIM)
    nc = kvc.shape[1]
    kvc = jnp.pad(kvc, ((0, 0), (0, LANES - nc), (0, 0)))
    kvc = jnp.concatenate([kvc[:, 0::2], kvc[:, 1::2]], axis=1)
    yb = _nsa_prompt(zb, kvc, sl_nsa)
    yc = _dsa_prompt(zc, sl_dsa)
    kv_d = _kv_rows(zc[..., gw:gw + 2 * HEAD_DIM])
    o = gw + 2 * HEAD_DIM + IDX_HEADS * IDX_DIM
    ki = zc[..., o:o + IDX_DIM]
    yd, conv_buf = _conv_mixer(zd, jnp.zeros((b, CONV_K - 1, gw), F32), lp['conv_w'])
    return (ya, yb, yc, yd), (s_rwkv, shift, kv_c, kv_s, kv_w[:, -min(NSA_WINDOW, t):], kv_d, ki, conv_buf)


def _mixers_sample(za, zb, zc, zd, lp, sl_nsa, sl_dsa, s_rwkv, s_shift, cache_nsa_cmp, cache_nsa_slc, s_win,
                   cache_dsa_kv, cache_dsa_idx, layer, s_conv, page_table):
    b, t, _ = za.shape
    gw = GROUP_WIDTH
    n_past = page_table.shape[1] * PAGE_SIZE
    n_buf = s_win.shape[1]
    qpos = n_past + jnp.arange(t)
    shift_prev = jnp.pad(s_shift, ((0, 0), (0, A_PAD - A_COLS)))
    ya, s_rwkv_new, shift = _rwkv_mixer(za, shift_prev, s_rwkv, lp)
    q = zb[..., :gw].reshape(b, t, N_HEADS, HEAD_DIM)
    kv_c = _kv_rows(zb[..., gw:gw + 2 * HEAD_DIM])
    kv_s = _kv_rows(zb[..., gw + 2 * HEAD_DIM:gw + 4 * HEAD_DIM])
    kv_w = _kv_rows(zb[..., gw + 4 * HEAD_DIM:gw + 6 * HEAD_DIM])
    o = gw + 6 * HEAD_DIM
    gates = jax.nn.sigmoid(zb[..., o:o + 3 * N_HEADS]).reshape(b, t, N_HEADS, 3)
    assert (n_past + t) // NSA_CMP_BLOCK * NSA_CMP_BLOCK == n_past
    slabs = lambda c: c.transpose(0, 1, 3, 4, 2).reshape(-1, 2, HEAD_DIM, PAGE_SIZE)
    layer_pages = page_table + layer * cache_nsa_cmp.shape[1]
    kvc = _nsa_compress_pages(slabs(cache_nsa_cmp)[layer_pages], lp)
    kvw = jnp.concatenate([s_win, kv_w], axis=1)
    wpos = n_past - n_buf + jnp.arange(n_buf + t)
    yb = _nsa_attend_sample(q, gates, qpos, kvc, slabs(cache_nsa_slc), layer_pages, kv_s,
                            kvw, wpos, sl_nsa, n_past + t)
    kv_d = _kv_rows(zc[..., gw:gw + 2 * HEAD_DIM])
    o = gw + 2 * HEAD_DIM + IDX_HEADS * IDX_DIM
    ki = zc[..., o:o + IDX_DIM]
    yc = _dsa_sample(zc, cache_dsa_kv, cache_dsa_idx, layer, page_table, sl_dsa)
    yd, conv_buf = _conv_mixer(zd, s_conv, lp['conv_w'])
    flat = lambda u: u.reshape(b, t, gw)
    return (ya, flat(yb), flat(yc), yd), (s_rwkv_new, shift, kv_c, kv_s, kvw[:, -n_buf:], kv_d, ki, conv_buf)


def _layer(x, mod, lp, pw, mixer_fn, alpha):
    b, t, d = x.shape
    sh1, sc1, g1, sh2, sc2, g2 = mod
    zs = _inproj(x, sc1, sh1, pw['w_in'])
    ys, states = mixer_fn(*zs)
    x1, h2, logits = _outproj(ys, x, g1, sc2, sh2, pw['w_out'], pw['ln1_g'], pw['ln1_b'],
                              pw['wr'], pw['br'], alpha)
    n_exp = lp['moe_wr'].shape[1]
    blk = _moe_block_rows(b * t * TOP_K, n_exp)
    picked, gate = _moe(h2.reshape(b * t, d), logits.reshape(b * t, LANES)[:, :n_exp], pw['moe'], pw['layer'], blk)
    x2 = _resln(x1, picked, gate, g2, pw['ln2_g'], pw['ln2_b'], alpha)
    return x2, states


def _prep_weights(lp):
    d = lp['w_in'].shape[0]
    cols, o = [], 0
    for c, p in ((A_COLS, A_PAD), (B_COLS, B_PAD), (C_COLS, C_PAD), (D_COLS, D_PAD)):
        cols.append(jnp.pad(lp['w_in'][:, o:o + c], ((0, 0), (0, p - c))))
        o += c
    n_exp = lp['moe_wr'].shape[1]
    row = lambda u: u.reshape(1, -1)
    return {
        'w_in': jnp.concatenate(cols, axis=1).astype(BF16),
        'w_out': lp['w_out'].reshape(4, GROUP_WIDTH, d).astype(BF16),
        'ln1_g': row(lp['ln1_g']), 'ln1_b': row(lp['ln1_b']),
        'ln2_g': row(lp['ln2_g']), 'ln2_b': row(lp['ln2_b']),
        'wr': jnp.pad(lp['moe_wr'], ((0, 0), (0, LANES - n_exp))),
        'br': jnp.pad(lp['moe_br'], (0, LANES - n_exp)).reshape(1, LANES),
    }


def kernel(x_prompt, x_sample, state_rwkv, state_rwkv_shift, cache_nsa_cmp, cache_nsa_slc, state_nsa_win, cache_dsa_kv, cache_dsa_idx, state_conv, page_table, c_prompt, c_sample, w_ada, b_ada, w_in, w_out, ln1_g, ln1_b, ln2_g, ln2_b, rwkv_mu, rwkv_w0, rwkv_w2, rwkv_a0, rwkv_a2, rwkv_g2, rwkv_kk, rwkv_ka, rwkv_rk, rwkv_gn_g, rwkv_gn_b, nsa_pe, nsa_w1, nsa_w2, conv_w, moe_wr, moe_br, moe_wgu, moe_bgu, moe_wd, moe_bd):
    depth = w_in.shape[0]
    alpha = (2 * depth) ** 0.25
    sl_nsa, sl_dsa = _alibi_slopes()
    bp = x_prompt.shape[0]
    hp, hs = x_prompt, x_sample
    c_all = jnp.concatenate([c_prompt, c_sample], axis=0)
    flat_le = lambda u: u.reshape((-1,) + u.shape[2:])
    moe_all = (flat_le(moe_wgu), flat_le(moe_bgu), flat_le(moe_wd), flat_le(moe_bd))
    states_p, states_s = [], []
    for l in range(depth):
        lp = {
            'w_in': w_in[l], 'w_out': w_out[l], 'ln1_g': ln1_g[l], 'ln1_b': ln1_b[l],
            'ln2_g': ln2_g[l], 'ln2_b': ln2_b[l],
            'rwkv_mu': jnp.pad(rwkv_mu[l], (0, A_PAD - A_COLS)), 'rwkv_w0': rwkv_w0[l], 'rwkv_w2': rwkv_w2[l],
            'rwkv_a0': rwkv_a0[l], 'rwkv_a2': rwkv_a2[l], 'rwkv_g2': rwkv_g2[l],
            'rwkv_kk': rwkv_kk[l], 'rwkv_ka': rwkv_ka[l], 'rwkv_rk': rwkv_rk[l],
            'rwkv_gn_g': rwkv_gn_g[l], 'rwkv_gn_b': rwkv_gn_b[l],
            'nsa_pe': nsa_pe[l], 'nsa_w1': nsa_w1[l], 'nsa_w2': nsa_w2[l], 'conv_w': conv_w[l],
            'moe_wr': moe_wr[l], 'moe_br': moe_br[l],
        }
        pw = _prep_weights(lp)
        pw['moe'], pw['layer'] = moe_all, l
        mod_all = jax.nn.silu(c_all) @ w_ada[l] + b_ada[l]
        mod_p = [u[:, None, :] for u in jnp.split(mod_all[:bp], 6, axis=-1)]
        mod_s = [u[:, None, :] for u in jnp.split(mod_all[bp:], 6, axis=-1)]
        mix_p = functools.partial(_mixers_prompt, lp=lp, sl_nsa=sl_nsa, sl_dsa=sl_dsa)
        mix_s = functools.partial(
            _mixers_sample, lp=lp, sl_nsa=sl_nsa, sl_dsa=sl_dsa, s_rwkv=state_rwkv[l],
            s_shift=state_rwkv_shift[l], cache_nsa_cmp=cache_nsa_cmp, cache_nsa_slc=cache_nsa_slc,
            s_win=state_nsa_win[l], cache_dsa_kv=cache_dsa_kv, cache_dsa_idx=cache_dsa_idx, layer=l,
            s_conv=state_conv[l], page_table=page_table)
        hs, st_s = _layer(hs, mod_s, lp, pw, mix_s, alpha)
        hp, st_p = _layer(hp, mod_p, lp, pw, mix_p, alpha)
        states_p.append(st_p)
        states_s.append(st_s)
    outs_p = [jnp.stack(v) for v in zip(*states_p)]
    outs_s = [jnp.stack(v) for v in zip(*states_s)]
    return (hp, hs, *outs_p, *outs_s)
```
